```python
import math
import jax, jax.numpy as jnp
from jax import lax
import numpy as np

D_MODEL = 1024
BATCH = 8
SEQ = 2048
DEPTH = 2
DEC_BATCH = 128
DEC_SEQ = 4
PAST_LEN = 16384
PAGE_SIZE = 128

N_MIXERS = 2
N_A_LAYERS = (DEPTH + N_MIXERS - 1) // N_MIXERS
N_B_LAYERS = DEPTH // N_MIXERS
BRANCH = 3 * D_MODEL // 2
N_XHEADS = 4
XHEAD_DIM = 128
XATT = N_XHEADS * XHEAD_DIM
MIX_WIDTH = BRANCH + XATT
N_MEM = 256
CHUNK = 128
A_GROUPS = 8
A_GDIM = BRANCH // A_GROUPS
SSM_GCH = 16
SSM_GROUPS = BRANCH // SSM_GCH
SSM_STATE = 64
DT_MIN = 1e-3
DT_MAX = 1e-1
EPS = 1e-6

kernel_name = "hybrid_gmlp_s5_memxattn_step"


def rms_norm(x, g):
    xf = x.astype(jnp.float32)
    y = xf * lax.rsqrt(jnp.mean(xf * xf, axis=-1, keepdims=True) + EPS)
    return (y * g.astype(jnp.float32)).astype(x.dtype)


def layer_norm(x, g, b):
    xf = x.astype(jnp.float32)
    mu = jnp.mean(xf, axis=-1, keepdims=True)
    var = jnp.mean(jnp.square(xf - mu), axis=-1, keepdims=True)
    y = (xf - mu) * lax.rsqrt(var + EPS) * g.astype(jnp.float32) + b.astype(jnp.float32)
    return y.astype(x.dtype)


def memory_kv(mem, g, w_k, w_v):
    m = rms_norm(mem, g)
    k = (m @ w_k).reshape(mem.shape[0], mem.shape[1], N_XHEADS, XHEAD_DIM)
    v = (m @ w_v).reshape(mem.shape[0], mem.shape[1], N_XHEADS, XHEAD_DIM)
    return k, v


def cross_attend(q, k, v):
    s = jnp.einsum("blhd,bmhd->bhlm", q.astype(jnp.float32), k.astype(jnp.float32)) * (XHEAD_DIM ** -0.5)
    p = jax.nn.softmax(s, axis=-1)
    o = jnp.einsum("bhlm,bmhd->blhd", p, v.astype(jnp.float32))
    return o.reshape(q.shape[0], q.shape[1], XATT).astype(q.dtype)


def chunk_gating(z_uv, ln_g, ln_b, w_s, b_s):
    uv = jax.nn.gelu(z_uv)
    u, v = uv[..., :BRANCH], uv[..., BRANCH:]
    v = layer_norm(v, ln_g, ln_b)
    bsz, l, _ = v.shape
    n = min(l, CHUNK)
    nc = l // n
    mask = jnp.tril(jnp.ones((n, n), dtype=bool))
    w = jnp.where(mask, w_s[:, :n, :n], jnp.zeros((), w_s.dtype))
    vc = v.reshape(bsz, nc, n, A_GROUPS, A_GDIM)
    mixed = jnp.einsum("gts,bcsgd->bctgd", w, vc) + jnp.transpose(b_s[:, :n])[:, :, None]
    return u * mixed.reshape(bsz, l, BRANCH), v


def _combine(e1, e2):
    a1, b1 = e1
    a2, b2 = e2
    return a1 * a2, a2 * b1 + b2


def s5_branch(u, h0, lam_re, lam_im, log_dt, b_re, b_im, c_re, c_im, d, w_glu, b_glu):
    f32 = jnp.float32
    lam = lax.complex(lam_re.astype(f32), lam_im.astype(f32))
    dt = jnp.exp(log_dt.astype(f32))[:, None]
    lam_bar = jnp.exp(lam * dt)
    b_bar = ((lam_bar - 1.0) / lam)[..., None] * lax.complex(b_re.astype(f32), b_im.astype(f32))
    cr, ci = c_re.astype(f32), c_im.astype(f32)
    dg = d.astype(f32).reshape(SSM_GROUPS, SSM_GCH)
    bsz, l, _ = u.shape
    n = min(l, CHUNK)
    nc = l // n
    u_blocks = u.astype(f32).reshape(bsz, nc, n, SSM_GROUPS, SSM_GCH).transpose(1, 2, 0, 3, 4)
    a = jnp.broadcast_to(lam_bar, (n, 1, SSM_GROUPS, SSM_STATE))

    def step(h, uc):
        bu = jnp.einsum("tbgc,gpc->tbgp", uc.astype(jnp.complex64), b_bar)
        a_cum, hs = lax.associative_scan(_combine, (a, bu), axis=0)
        hs = hs + a_cum * h[None]
        yc = (jnp.einsum("tbgp,gcp->tbgc", hs.real, cr)
              - jnp.einsum("tbgp,gcp->tbgc", hs.imag, ci) + dg * uc)
        return hs[-1], yc

    h_last, ys = lax.scan(step, h0, u_blocks)
    y = ys.transpose(2, 0, 1, 3, 4).reshape(bsz, l, BRANCH)
    y = jax.nn.gelu(y)
    y = y * jax.nn.sigmoid(y @ w_glu.astype(f32) + b_glu.astype(f32))
    return y.astype(u.dtype), h_last


def setup_inputs(seed: int = 0) -> dict:
    key = jax.random.key(seed)
    ks = jax.random.split(key, 32)
    f32 = jnp.float32
    nrm = lambda k, s, sc: jax.random.normal(k, s, f32) * sc
    in_a = 2 * BRANCH + XATT + MIX_WIDTH
    in_b = BRANCH + XATT + MIX_WIDTH
    lam_im = jnp.broadcast_to(math.pi * jnp.arange(SSM_STATE, dtype=f32), (N_B_LAYERS, SSM_GROUPS, SSM_STATE))
    return {
        "x_prompt": nrm(ks[0], (BATCH, SEQ, D_MODEL), 1.0),
        "x_sample": nrm(ks[1], (DEC_BATCH, DEC_SEQ, D_MODEL), 1.0),
        "cache_mem_k": nrm(ks[2], (DEPTH, DEC_BATCH, N_MEM, N_XHEADS, XHEAD_DIM), 1.0),
        "cache_mem_v": nrm(ks[3], (DEPTH, DEC_BATCH, N_MEM, N_XHEADS, XHEAD_DIM), 1.0),
        "state_ssm_re": nrm(ks[4], (N_B_LAYERS, DEC_BATCH, SSM_GROUPS, SSM_STATE), 0.5),
        "state_ssm_im": nrm(ks[5], (N_B_LAYERS, DEC_BATCH, SSM_GROUPS, SSM_STATE), 0.5),
        "mem_prompt": nrm(ks[6], (BATCH, N_MEM, D_MODEL), 1.0),
        "w_in_a": nrm(ks[7], (N_A_LAYERS, D_MODEL, in_a), D_MODEL ** -0.5),
        "ln_v_g": 1.0 + nrm(ks[8], (N_A_LAYERS, BRANCH), 0.02),
        "ln_v_b": nrm(ks[9], (N_A_LAYERS, BRANCH), 0.02),
        "w_spatial": nrm(ks[10], (N_A_LAYERS, A_GROUPS, CHUNK, CHUNK), CHUNK ** -0.5),
        "b_spatial": 1.0 + nrm(ks[11], (N_A_LAYERS, A_GROUPS, CHUNK), 0.02),
        "w_in_b": nrm(ks[12], (N_B_LAYERS, D_MODEL, in_b), D_MODEL ** -0.5),
        "ssm_lambda_re": -0.5 * jnp.exp(nrm(ks[13], (N_B_LAYERS, SSM_GROUPS, SSM_STATE), 0.05)),
        "ssm_lambda_im": lam_im + nrm(ks[14], (N_B_LAYERS, SSM_GROUPS, SSM_STATE), 0.01),
        "ssm_log_dt": jax.random.uniform(ks[15], (N_B_LAYERS, SSM_GROUPS), f32, math.log(DT_MIN), math.log(DT_MAX)),
        "ssm_b_re": nrm(ks[16], (N_B_LAYERS, SSM_GROUPS, SSM_STATE, SSM_GCH), (2 * SSM_GCH) ** -0.5),
        "ssm_b_im": nrm(ks[17], (N_B_LAYERS, SSM_GROUPS, SSM_STATE, SSM_GCH), (2 * SSM_GCH) ** -0.5),
        "ssm_c_re": nrm(ks[18], (N_B_LAYERS, SSM_GROUPS, SSM_GCH, SSM_STATE), (2 * SSM_STATE) ** -0.5),
        "ssm_c_im": nrm(ks[19], (N_B_LAYERS, SSM_GROUPS, SSM_GCH, SSM_STATE), (2 * SSM_STATE) ** -0.5),
        "ssm_d": nrm(ks[20], (N_B_LAYERS, BRANCH), 1.0),
        "w_glu": nrm(ks[21], (N_B_LAYERS, BRANCH, BRANCH), BRANCH ** -0.5),
        "b_glu": nrm(ks[22], (N_B_LAYERS, BRANCH), 0.01),
        "mem_norm_g": 1.0 + nrm(ks[23], (DEPTH, D_MODEL), 0.02),
        "w_mem_k": nrm(ks[24], (DEPTH, D_MODEL, XATT), D_MODEL ** -0.5),
        "w_mem_v": nrm(ks[25], (DEPTH, D_MODEL, XATT), D_MODEL ** -0.5),
        "w_out": nrm(ks[26], (DEPTH, MIX_WIDTH, D_MODEL), MIX_WIDTH ** -0.5),
        "pre_norm_g": 1.0 + nrm(ks[27], (DEPTH, D_MODEL), 0.02),
        "post_norm_g": 1.0 + nrm(ks[28], (DEPTH, D_MODEL), 0.02),
    }


def reference(x_prompt, x_sample, cache_mem_k, cache_mem_v, state_ssm_re, state_ssm_im, mem_prompt,
              w_in_a, ln_v_g, ln_v_b, w_spatial, b_spatial,
              w_in_b, ssm_lambda_re, ssm_lambda_im, ssm_log_dt, ssm_b_re, ssm_b_im, ssm_c_re, ssm_c_im,
              ssm_d, w_glu, b_glu,
              mem_norm_g, w_mem_k, w_mem_v, w_out, pre_norm_g, post_norm_g):
    f32 = jnp.float32

    def layer(i, x, k_mem, v_mem, h0):
        j = i // N_MIXERS
        h = rms_norm(x, pre_norm_g[i])
        if i % N_MIXERS == 0:
            z = h @ w_in_a[j]
            branch, extra = chunk_gating(z[..., :2 * BRANCH], ln_v_g[j], ln_v_b[j], w_spatial[j], b_spatial[j])
            off = 2 * BRANCH
        else:
            z = h @ w_in_b[j]
            branch, extra = s5_branch(z[..., :BRANCH], h0, ssm_lambda_re[j], ssm_lambda_im[j], ssm_log_dt[j],
                                      ssm_b_re[j], ssm_b_im[j], ssm_c_re[j], ssm_c_im[j], ssm_d[j],
                                      w_glu[j], b_glu[j])
            off = BRANCH
        q = z[..., off:off + XATT].reshape(x.shape[0], x.shape[1], N_XHEADS, XHEAD_DIM)
        att = cross_attend(q, k_mem, v_mem)
        mixed = jnp.concatenate([branch, att], axis=-1) * jax.nn.silu(z[..., off + XATT:])
        return x + rms_norm(mixed @ w_out[i], post_norm_g[i]), extra

    yp, ys = x_prompt, x_sample
    mk_p, mv_p, hp_re, hp_im, hs_re, hs_im, v_s = [], [], [], [], [], [], []
    for i in range(DEPTH):
        j = i // N_MIXERS
        kp, vp = memory_kv(mem_prompt, mem_norm_g[i], w_mem_k[i], w_mem_v[i])
        mk_p.append(kp)
        mv_p.append(vp)
        if i % N_MIXERS == 0:
            yp, _ = layer(i, yp, kp, vp, None)
            ys, v_rows = layer(i, ys, cache_mem_k[i], cache_mem_v[i], None)
            v_s.append(v_rows)
        else:
            h0p = jnp.zeros((x_prompt.shape[0], SSM_GROUPS, SSM_STATE), jnp.complex64)
            h0s = lax.complex(state_ssm_re[j].astype(f32), state_ssm_im[j].astype(f32))
            yp, hlp = layer(i, yp, kp, vp, h0p)
            ys, hls = layer(i, ys, cache_mem_k[i], cache_mem_v[i], h0s)
            hp_re.append(hlp.real)
            hp_im.append(hlp.imag)
            hs_re.append(hls.real)
            hs_im.append(hls.imag)

    return (yp, ys, jnp.stack(mk_p), jnp.stack(mv_p), jnp.stack(hp_re), jnp.stack(hp_im),
            jnp.stack(hs_re), jnp.stack(hs_im), jnp.stack(v_s))
```

```python
import functools
import math

import jax
import jax.numpy as jnp
from jax import lax
from jax.experimental import pallas as pl
from jax.experimental.pallas import tpu as pltpu

D_MODEL = 1024
BATCH = 8
SEQ = 2048
DEC_BATCH = 128
DEC_SEQ = 4
BRANCH = 1536
N_XHEADS = 4
XHEAD_DIM = 128
XATT = N_XHEADS * XHEAD_DIM
MIX_WIDTH = BRANCH + XATT
N_MEM = 256
CHUNK = 128
A_GROUPS = 8
A_GDIM = BRANCH // A_GROUPS
SSM_GCH = 16
SSM_GROUPS = 96
SSM_STATE = 64
EPS = 1e-6

LANES = 128
S5_TILE_GROUPS = LANES // SSM_GCH
S5_TILES = SSM_GROUPS // S5_TILE_GROUPS
S5_HALF = S5_TILE_GROUPS * SSM_STATE
VMEM_LIMIT = 56 * 1024 * 1024

F32 = jnp.float32
BF16 = jnp.bfloat16


def _dot(a, b):
    return jnp.dot(a, b, preferred_element_type=F32)


def _rms(x, g):
    return x * lax.rsqrt(jnp.mean(x * x, axis=-1, keepdims=True) + EPS) * g


def _layer_norm(x, g, b):
    mu = jnp.mean(x, axis=-1, keepdims=True)
    xc = x - mu
    var = jnp.mean(xc * xc, axis=-1, keepdims=True)
    return xc * lax.rsqrt(var + EPS) * g + b


def _gelu(x):
    c = math.sqrt(2.0 / math.pi)
    return x * (0.5 * (1.0 + jnp.tanh(c * (x + 0.044715 * (x * x * x)))))


def _sigmoid(x):
    return 1.0 / (1.0 + jnp.exp(-x))


def _silu(x):
    return x * _sigmoid(x)


def _xattn(q, k, v):
    outs = []
    for h in range(N_XHEADS):
        hs = slice(h * XHEAD_DIM, (h + 1) * XHEAD_DIM)
        s = lax.dot_general(q[:, hs].astype(BF16), k[:, hs], (((1,), (1,)), ((), ())),
                            preferred_element_type=F32) * (XHEAD_DIM ** -0.5)
        p = jnp.exp(s - jnp.max(s, axis=-1, keepdims=True))
        l = jnp.sum(p, axis=-1, keepdims=True)
        outs.append(_dot(p.astype(BF16), v[:, hs]) / l)
    return jnp.concatenate(outs, axis=-1)


def _const_spec(shape):
    nd = len(shape)
    return pl.BlockSpec(shape, lambda *_: (0,) * nd, pipeline_mode=pl.Buffered(1))


def _s5_prep_kernel(lr_ref, li_ref, ldt_ref, br_ref, bi_ref, cr_ref, ci_ref,
                    ar_ref, ai_ref, bbd_ref, cbd_ref):
    lr = lr_ref[0]
    li = li_ref[0]
    dt = jnp.exp(ldt_ref[0])
    e = jnp.exp(lr * dt)
    ar = e * jnp.cos(li * dt)
    ai = e * jnp.sin(li * dt)
    ar_ref[0] = ar
    ai_ref[0] = ai
    nr = ar - 1.0
    den = lr * lr + li * li
    kr = (nr * lr + ai * li) / den
    ki = (ai * lr - nr * li) / den
    br = br_ref[0]
    bi = bi_ref[0]
    row_g = lax.broadcasted_iota(jnp.int32, br.shape, 0) // SSM_GCH
    col_g = lax.broadcasted_iota(jnp.int32, br.shape, 1) // SSM_STATE
    own = row_g == col_g
    bbd_ref[0, :, 0:S5_HALF] = jnp.where(own, kr * br - ki * bi, 0.0).astype(BF16)
    bbd_ref[0, :, S5_HALF:] = jnp.where(own, kr * bi + ki * br, 0.0).astype(BF16)
    cr = cr_ref[0]
    ci = ci_ref[0]
    row_g = lax.broadcasted_iota(jnp.int32, cr.shape, 0) // SSM_STATE
    col_g = lax.broadcasted_iota(jnp.int32, cr.shape, 1) // SSM_GCH
    own = row_g == col_g
    cbd_ref[0, 0:S5_HALF, :] = jnp.where(own, cr, 0.0).astype(BF16)
    cbd_ref[0, S5_HALF:, :] = jnp.where(own, -ci, 0.0).astype(BF16)


def _s5_prep(lam_re, lam_im, log_dt, b_re, b_im, c_re, c_im):
    nt, tg = S5_TILES, S5_TILE_GROUPS
    lr = lam_re.reshape(nt, 1, S5_HALF)
    li = lam_im.reshape(nt, 1, S5_HALF)
    ldt = jnp.broadcast_to(log_dt[:, None], (SSM_GROUPS, SSM_STATE)).reshape(nt, 1, S5_HALF)

    def expand_b(b):
        t = b.reshape(nt, tg, SSM_STATE, SSM_GCH).transpose(0, 3, 1, 2)
        t = jnp.broadcast_to(t[:, None], (nt, tg, SSM_GCH, tg, SSM_STATE))
        return t.reshape(nt, LANES, S5_HALF)

    def expand_c(c):
        t = c.reshape(nt, tg, SSM_GCH, SSM_STATE).transpose(0, 1, 3, 2)
        t = jnp.broadcast_to(t[:, :, :, None], (nt, tg, SSM_STATE, tg, SSM_GCH))
        return t.reshape(nt, S5_HALF, LANES)

    vec = pl.BlockSpec((1, 1, S5_HALF), lambda j: (j, 0, 0))
    bsp = pl.BlockSpec((1, LANES, S5_HALF), lambda j: (j, 0, 0))
    csp = pl.BlockSpec((1, S5_HALF, LANES), lambda j: (j, 0, 0))
    return pl.pallas_call(
        _s5_prep_kernel,
        grid=(nt,),
        in_specs=[vec, vec, vec, bsp, bsp, csp, csp],
        out_specs=[vec, vec,
                   pl.BlockSpec((1, LANES, 2 * S5_HALF), lambda j: (j, 0, 0)),
                   pl.BlockSpec((1, 2 * S5_HALF, LANES), lambda j: (j, 0, 0))],
        out_shape=[jax.ShapeDtypeStruct((nt, 1, S5_HALF), F32),
                   jax.ShapeDtypeStruct((nt, 1, S5_HALF), F32),
                   jax.ShapeDtypeStruct((nt, LANES, 2 * S5_HALF), BF16),
                   jax.ShapeDtypeStruct((nt, 2 * S5_HALF, LANES), BF16)],
        name="s5_prep",
    )(lr, li, ldt, expand_b(b_re), expand_b(b_im), expand_c(c_re), expand_c(c_im))


def _mem_kv_kernel(mem_ref, g_ref, wk_ref, wv_ref, k_ref, v_ref, kb_ref, vb_ref):
    m = _rms(mem_ref[...], g_ref[...]).astype(BF16)
    k = _dot(m, wk_ref[...])
    v = _dot(m, wv_ref[...])
    k_ref[...] = k
    v_ref[...] = v
    kb_ref[...] = k.astype(BF16)
    vb_ref[...] = v.astype(BF16)


def _mem_kv(mem, g, wk, wv):
    rows = mem.shape[0]
    depth = g.shape[0]
    tm = 512
    out = pl.BlockSpec((None, tm, XATT), lambda i, r: (i, r, 0))
    wsp = pl.BlockSpec((None, D_MODEL, XATT), lambda i, r: (i, 0, 0))
    return pl.pallas_call(
        _mem_kv_kernel,
        grid=(depth, rows // tm),
        in_specs=[pl.BlockSpec((tm, D_MODEL), lambda i, r: (r, 0)),
                  pl.BlockSpec((None, 1, D_MODEL), lambda i, r: (i, 0, 0)),
                  wsp, wsp],
        out_specs=[out, out, out, out],
        out_shape=[jax.ShapeDtypeStruct((depth, rows, XATT), F32)] * 2
        + [jax.ShapeDtypeStruct((depth, rows, XATT), BF16)] * 2,
        name="mem_kv",
    )(mem, g, wk, wv)


def _masked_spatial(ws_ref):
    row = lax.broadcasted_iota(jnp.int32, (CHUNK, CHUNK), 0)
    col = lax.broadcasted_iota(jnp.int32, (CHUNK, CHUNK), 1)
    return [jnp.where(row >= col, ws_ref[g], 0.0).astype(BF16) for g in range(A_GROUPS)]


def _gate_chunk(wm, vc, pair):
    base = pair * 2 * A_GDIM
    even = _dot(wm[2 * pair], vc[:, base:base + 2 * LANES])
    odd = _dot(wm[2 * pair + 1], vc[:, base + LANES:base + 3 * LANES])
    lane = lax.broadcasted_iota(jnp.int32, (CHUNK, LANES), 1)
    mid = jnp.where(lane < A_GDIM - LANES, even[:, LANES:], odd[:, :LANES])
    return jnp.concatenate([even[:, :LANES], mid, odd[:, LANES:]], axis=1)


def _s5_tile(u, hr, hi, ar, ai, bbd, cbd, dvec, bu_ref, hs_ref, steps, nb):
    bu_ref[...] = _dot(u.astype(BF16), bbd)
    ar = jnp.broadcast_to(ar, (nb, S5_HALF))
    ai = jnp.broadcast_to(ai, (nb, S5_HALF))

    def step(t, carry):
        hr, hi = carry
        r0 = pl.multiple_of(t * nb, nb)
        nr = ar * hr - ai * hi + bu_ref[pl.ds(r0, nb), 0:S5_HALF]
        ni = ar * hi + ai * hr + bu_ref[pl.ds(r0, nb), S5_HALF:]
        hs_ref[pl.ds(r0, nb), 0:S5_HALF] = nr
        hs_ref[pl.ds(r0, nb), S5_HALF:] = ni
        return nr, ni

    hr, hi = lax.fori_loop(0, steps, step, (hr, hi))
    y = _dot(hs_ref[...].astype(BF16), cbd) + dvec * u
    return y, hr, hi


def _layer_a_prompt_kernel(x_ref, k_ref, v_ref, win_ref, lng_ref, lnb_ref, ws_ref, bias_ref,
                           wout_ref, pre_ref, post_ref, o_ref, m_ref):
    tm = x_ref.shape[0]
    x = x_ref[...]
    h = _rms(x, pre_ref[...]).astype(BF16)
    gate = _silu(_dot(h, win_ref[:, 2 * BRANCH + XATT:]))
    u = _gelu(_dot(h, win_ref[:, 0:BRANCH]))
    v = _layer_norm(_gelu(_dot(h, win_ref[:, BRANCH:2 * BRANCH])), lng_ref[...], lnb_ref[...])
    vb = v.astype(BF16)
    wm = _masked_spatial(ws_ref)
    for c in range(tm // CHUNK):
        rs = slice(c * CHUNK, (c + 1) * CHUNK)
        vc = vb[rs]
        for pair in range(A_GROUPS // 2):
            cs = slice(pair * 2 * A_GDIM, (pair + 1) * 2 * A_GDIM)
            mixed = _gate_chunk(wm, vc, pair) + bias_ref[:, cs]
            m_ref[rs, cs] = (u[rs, cs] * mixed * gate[rs, cs]).astype(BF16)
    q = _dot(h, win_ref[:, 2 * BRANCH:2 * BRANCH + XATT])
    att = _xattn(q, k_ref[...], v_ref[...])
    m_ref[:, BRANCH:] = (att * gate[:, BRANCH:]).astype(BF16)
    out = _dot(m_ref[...], wout_ref[...])
    o_ref[...] = x + _rms(out, post_ref[...])


def _layer_a_prompt(x, kb, vb, win, lng, lnb, ws, bias, wout, pre, post):
    tm = 256
    nb, seq, _ = x.shape
    xs = pl.BlockSpec((None, tm, D_MODEL), lambda b, t: (b, t, 0))
    kv = pl.BlockSpec((None, N_MEM, XATT), lambda b, t: (b, 0, 0))
    return pl.pallas_call(
        _layer_a_prompt_kernel,
        grid=(nb, seq // tm),
        in_specs=[xs, kv, kv, _const_spec(win.shape), _const_spec(lng.shape), _const_spec(lnb.shape),
                  _const_spec(ws.shape), _const_spec(bias.shape), _const_spec(wout.shape),
                  _const_spec(pre.shape), _const_spec(post.shape)],
        out_specs=xs,
        out_shape=jax.ShapeDtypeStruct(x.shape, F32),
        scratch_shapes=[pltpu.VMEM((tm, MIX_WIDTH), BF16)],
        compiler_params=pltpu.CompilerParams(
            dimension_semantics=("arbitrary", "arbitrary"), vmem_limit_bytes=VMEM_LIMIT),
        name="layer_a_prompt",
    )(x, kb, vb, win, lng, lnb, ws, bias, wout, pre, post)


def _layer_b_prompt_kernel(x_ref, k_ref, v_ref, win_ref, ar_ref, ai_ref, bbd_ref, cbd_ref, d_ref,
                           wglu_ref, bglu_ref, wout_ref, pre_ref, post_ref,
                           o_ref, hre_ref, him_ref,
                           ut_ref, bu_ref, hs_ref, yt_ref, y_ref, m_ref):
    nb, tc, _ = x_ref.shape
    tm = nb * tc

    @pl.when(pl.program_id(0) == 0)
    def _():
        hre_ref[...] = jnp.zeros_like(hre_ref)
        him_ref[...] = jnp.zeros_like(him_ref)

    x = x_ref[...].reshape(tm, D_MODEL)
    h = _rms(x, pre_ref[...]).astype(BF16)
    u = _dot(h, win_ref[:, 0:BRANCH])
    for j in range(S5_TILES):
        for b in range(nb):
            ut_ref[j, pl.ds(b, tc, stride=nb), :] = u[b * tc:(b + 1) * tc, j * LANES:(j + 1) * LANES]
    for j in range(S5_TILES):
        ss = slice(j * S5_HALF, (j + 1) * S5_HALF)
        y, hr, hi = _s5_tile(ut_ref[j], hre_ref[:, ss], him_ref[:, ss], ar_ref[j], ai_ref[j],
                             bbd_ref[j], cbd_ref[j], d_ref[j], bu_ref, hs_ref, tc, nb)
        hre_ref[:, ss] = hr
        him_ref[:, ss] = hi
        yt_ref[...] = y
        for b in range(nb):
            y_ref[b * tc:(b + 1) * tc, j * LANES:(j + 1) * LANES] = yt_ref[pl.ds(b, tc, stride=nb), :]
    y = _gelu(y_ref[...])
    branch = y * _sigmoid(_dot(y.astype(BF16), wglu_ref[...]) + bglu_ref[...])
    gate = _silu(_dot(h, win_ref[:, BRANCH + XATT:]))
    m_ref[:, 0:BRANCH] = (branch * gate[:, 0:BRANCH]).astype(BF16)
    q = _dot(h, win_ref[:, BRANCH:BRANCH + XATT])
    for b in range(nb):
        rs = slice(b * tc, (b + 1) * tc)
        ms = slice(b * N_MEM, (b + 1) * N_MEM)
        att = _xattn(q[rs], k_ref[ms, :], v_ref[ms, :])
        m_ref[rs, BRANCH:] = (att * gate[rs, BRANCH:]).astype(BF16)
    out = _dot(m_ref[...], wout_ref[...])
    o_ref[...] = (x + _rms(out, post_ref[...])).reshape(nb, tc, D_MODEL)


def _layer_b_prompt(x, kb, vb, win, ar, ai, bbd, cbd, dvec, wglu, bglu, wout, pre, post):
    tc = 32
    nb, seq, _ = x.shape
    tm = nb * tc
    xs = pl.BlockSpec((nb, tc, D_MODEL), lambda t: (0, t, 0))
    hs = pl.BlockSpec((nb, SSM_GROUPS * SSM_STATE), lambda t: (0, 0))
    consts = [kb, vb, win, ar, ai, bbd, cbd, dvec, wglu, bglu, wout, pre, post]
    return pl.pallas_call(
        _layer_b_prompt_kernel,
        grid=(seq // tc,),
        in_specs=[xs] + [_const_spec(c.shape) for c in consts],
        out_specs=[xs, hs, hs],
        out_shape=[jax.ShapeDtypeStruct(x.shape, F32),
                   jax.ShapeDtypeStruct((nb, SSM_GROUPS * SSM_STATE), F32),
                   jax.ShapeDtypeStruct((nb, SSM_GROUPS * SSM_STATE), F32)],
        scratch_shapes=[pltpu.VMEM((S5_TILES, tm, LANES), F32),
                        pltpu.VMEM((tm, 2 * S5_HALF), F32),
                        pltpu.VMEM((tm, 2 * S5_HALF), F32),
                        pltpu.VMEM((tm, LANES), F32),
                        pltpu.VMEM((tm, BRANCH), F32),
                        pltpu.VMEM((tm, MIX_WIDTH), BF16)],
        compiler_params=pltpu.CompilerParams(
            dimension_semantics=("arbitrary",), vmem_limit_bytes=VMEM_LIMIT),
        name="layer_b_prompt",
    )(x, *consts)


def _sample_pre_a_kernel(x_ref, win_ref, lng_ref, lnb_ref, wrow_ref, brow_ref, pre_ref,
                         v_ref, mbr_ref, q_ref, gatt_ref):
    nb = DEC_BATCH
    h = _rms(x_ref[...], pre_ref[...]).astype(BF16)
    gate = _silu(_dot(h, win_ref[:, 2 * BRANCH + XATT:]))
    u = _gelu(_dot(h, win_ref[:, 0:BRANCH]))
    v = _layer_norm(_gelu(_dot(h, win_ref[:, BRANCH:2 * BRANCH])), lng_ref[...], lnb_ref[...])
    v_ref[...] = v
    for t in range(DEC_SEQ):
        rs = slice(t * nb, (t + 1) * nb)
        mixed = brow_ref[t:t + 1, :]
        for s in range(t + 1):
            mixed = mixed + wrow_ref[t * DEC_SEQ + s:t * DEC_SEQ + s + 1, :] * v[s * nb:(s + 1) * nb]
        mbr_ref[rs, :] = u[rs] * mixed * gate[rs, 0:BRANCH]
    q_ref[...] = _dot(h, win_ref[:, 2 * BRANCH:2 * BRANCH + XATT])
    gatt_ref[...] = gate[:, BRANCH:]


def _sample_pre_a(x, win, lng, lnb, wrow, brow, pre):
    rows = x.shape[0]
    ins = [x, win, lng, lnb, wrow, brow, pre]
    shapes = [(rows, BRANCH), (rows, BRANCH), (rows, XATT), (rows, XATT)]
    return pl.pallas_call(
        _sample_pre_a_kernel,
        grid=(1,),
        in_specs=[_const_spec(a.shape) for a in ins],
        out_specs=[pl.BlockSpec(s, lambda i: (0, 0)) for s in shapes],
        out_shape=[jax.ShapeDtypeStruct(s, F32) for s in shapes],
        compiler_params=pltpu.CompilerParams(
            dimension_semantics=("arbitrary",), vmem_limit_bytes=VMEM_LIMIT),
        name="sample_pre_a",
    )(*ins)


def _sample_pre_b_kernel(x_ref, win_ref, sre_ref, sim_ref, ar_ref, ai_ref, bbd_ref, cbd_ref, d_ref,
                         wglu_ref, bglu_ref, pre_ref,
                         mbr_ref, q_ref, gatt_ref, hre_ref, him_ref,
                         bu_ref, hs_ref, y_ref):
    nb = DEC_BATCH
    h = _rms(x_ref[...], pre_ref[...]).astype(BF16)
    u = _dot(h, win_ref[:, 0:BRANCH])
    for j in range(S5_TILES):
        ss = slice(j * S5_HALF, (j + 1) * S5_HALF)
        ls = slice(j * LANES, (j + 1) * LANES)
        y, hr, hi = _s5_tile(u[:, ls], sre_ref[:, ss], sim_ref[:, ss], ar_ref[j], ai_ref[j],
                             bbd_ref[j], cbd_ref[j], d_ref[j], bu_ref, hs_ref, DEC_SEQ, nb)
        hre_ref[:, ss] = hr
        him_ref[:, ss] = hi
        y_ref[:, ls] = y
    y = _gelu(y_ref[...])
    branch = y * _sigmoid(_dot(y.astype(BF16), wglu_ref[...]) + bglu_ref[...])
    gate = _silu(_dot(h, win_ref[:, BRANCH + XATT:]))
    mbr_ref[...] = branch * gate[:, 0:BRANCH]
    q_ref[...] = _dot(h, win_ref[:, BRANCH:BRANCH + XATT])
    gatt_ref[...] = gate[:, BRANCH:]


def _sample_pre_b(x, win, sre, sim, ar, ai, bbd, cbd, dvec, wglu, bglu, pre):
    rows = x.shape[0]
    ins = [x, win, sre, sim, ar, ai, bbd, cbd, dvec, wglu, bglu, pre]
    shapes = [(rows, BRANCH), (rows, XATT), (rows, XATT), sre.shape, sim.shape]
    return pl.pallas_call(
        _sample_pre_b_kernel,
        grid=(1,),
        in_specs=[_const_spec(a.shape) for a in ins],
        out_specs=[pl.BlockSpec(s, lambda i: (0, 0)) for s in shapes],
        out_shape=[jax.ShapeDtypeStruct(s, F32) for s in shapes],
        scratch_shapes=[pltpu.VMEM((rows, 2 * S5_HALF), F32),
                        pltpu.VMEM((rows, 2 * S5_HALF), F32),
                        pltpu.VMEM((rows, BRANCH), F32)],
        compiler_params=pltpu.CompilerParams(
            dimension_semantics=("arbitrary",), vmem_limit_bytes=VMEM_LIMIT),
        name="sample_pre_b",
    )(*ins)


def _sample_attn_kernel(q_ref, g_ref, k_ref, v_ref, o_ref):
    for j in range(q_ref.shape[0]):
        att = _xattn(q_ref[j], k_ref[j].astype(BF16), v_ref[j].astype(BF16))
        o_ref[j] = att * g_ref[j]


def _sample_attn(q, gatt, cache_k, cache_v, layer):
    bb = 8
    qs = pl.BlockSpec((bb, DEC_SEQ, XATT), lambda i: (i, 0, 0))
    cs = pl.BlockSpec((None, bb, N_MEM, XATT), lambda i: (layer, i, 0, 0))
    return pl.pallas_call(
        _sample_attn_kernel,
        grid=(q.shape[0] // bb,),
        in_specs=[qs, qs, cs, cs],
        out_specs=qs,
        out_shape=jax.ShapeDtypeStruct(q.shape, F32),
        compiler_params=pltpu.CompilerParams(
            dimension_semantics=("arbitrary",), vmem_limit_bytes=VMEM_LIMIT),
        name="sample_attn",
    )(q, gatt, cache_k, cache_v)


def _sample_post_kernel(x_ref, mbr_ref, matt_ref, wout_ref, post_ref, o_ref):
    out = (_dot(mbr_ref[...].astype(BF16), wout_ref[0:BRANCH, :])
           + _dot(matt_ref[...].astype(BF16), wout_ref[BRANCH:, :]))
    o_ref[...] = x_ref[...] + _rms(out, post_ref[...])


def _sample_post(x, mbr, matt, wout, post):
    ins = [x, mbr, matt, wout, post]
    return pl.pallas_call(
        _sample_post_kernel,
        grid=(1,),
        in_specs=[_const_spec(a.shape) for a in ins],
        out_specs=pl.BlockSpec(x.shape, lambda i: (0, 0)),
        out_shape=jax.ShapeDtypeStruct(x.shape, F32),
        compiler_params=pltpu.CompilerParams(
            dimension_semantics=("arbitrary",), vmem_limit_bytes=VMEM_LIMIT),
        name="sample_post",
    )(*ins)


def _to_bt(a):
    return a.reshape(DEC_SEQ, DEC_BATCH, a.shape[-1]).transpose(1, 0, 2)


def _to_tb(a):
    return a.transpose(1, 0, 2).reshape(DEC_SEQ * DEC_BATCH, a.shape[-1])


def kernel(x_prompt, x_sample, cache_mem_k, cache_mem_v, state_ssm_re, state_ssm_im, mem_prompt,
           w_in_a, ln_v_g, ln_v_b, w_spatial, b_spatial,
           w_in_b, ssm_lambda_re, ssm_lambda_im, ssm_log_dt, ssm_b_re, ssm_b_im, ssm_c_re, ssm_c_im,
           ssm_d, w_glu, b_glu,
           mem_norm_g, w_mem_k, w_mem_v, w_out, pre_norm_g, post_norm_g):
    depth = w_out.shape[0]
    win_a = w_in_a[0].astype(BF16)
    win_b = w_in_b[0].astype(BF16)
    wglu = w_glu[0].astype(BF16)
    wout = w_out.astype(BF16)
    wk = w_mem_k.astype(BF16)
    wv = w_mem_v.astype(BF16)
    pre = pre_norm_g.reshape(depth, 1, D_MODEL)
    post = post_norm_g.reshape(depth, 1, D_MODEL)
    lng = ln_v_g[0].reshape(1, BRANCH)
    lnb = ln_v_b[0].reshape(1, BRANCH)
    bglu = b_glu[0].reshape(1, BRANCH)
    dvec = ssm_d[0].reshape(S5_TILES, 1, LANES)
    bias = jnp.repeat(b_spatial[0].T, A_GDIM, axis=1)
    ws4 = w_spatial[0][:, :DEC_SEQ, :DEC_SEQ]
    wrow = jnp.repeat(ws4.transpose(1, 2, 0).reshape(DEC_SEQ * DEC_SEQ, A_GROUPS), A_GDIM, axis=1)
    brow = bias[:DEC_SEQ]

    ar, ai, bbd, cbd = _s5_prep(ssm_lambda_re[0], ssm_lambda_im[0], ssm_log_dt[0],
                                ssm_b_re[0], ssm_b_im[0], ssm_c_re[0], ssm_c_im[0])

    mem = mem_prompt.reshape(BATCH * N_MEM, D_MODEL)
    mk, mv, mkb, mvb = _mem_kv(mem, mem_norm_g.reshape(depth, 1, D_MODEL), wk, wv)

    yp = _layer_a_prompt(x_prompt, mkb[0].reshape(BATCH, N_MEM, XATT), mvb[0].reshape(BATCH, N_MEM, XATT),
                         win_a, lng, lnb, w_spatial[0], bias, wout[0], pre[0], post[0])
    yp, hp_re, hp_im = _layer_b_prompt(yp, mkb[1], mvb[1], win_b, ar, ai, bbd, cbd, dvec,
                                       wglu, bglu, wout[1], pre[1], post[1])

    ck = cache_mem_k.reshape(depth, DEC_BATCH, N_MEM, XATT)
    cv = cache_mem_v.reshape(depth, DEC_BATCH, N_MEM, XATT)
    xs = _to_tb(x_sample)
    v_rows, mbr, q, gatt = _sample_pre_a(xs, win_a, lng, lnb, wrow, brow, pre[0])
    matt = _sample_attn(_to_bt(q), _to_bt(gatt), ck, cv, 0)
    xs = _sample_post(xs, mbr, _to_tb(matt), wout[0], post[0])
    sre = state_ssm_re[0].reshape(DEC_BATCH, SSM_GROUPS * SSM_STATE)
    sim = state_ssm_im[0].reshape(DEC_BATCH, SSM_GROUPS * SSM_STATE)
    mbr, q, gatt, hs_re, hs_im = _sample_pre_b(xs, win_b, sre, sim, ar, ai, bbd, cbd, dvec,
                                               wglu, bglu, pre[1])
    matt = _sample_attn(_to_bt(q), _to_bt(gatt), ck, cv, 1)
    xs = _sample_post(xs, mbr, _to_tb(matt), wout[1], post[1])

    kv_shape = (depth, BATCH, N_MEM, N_XHEADS, XHEAD_DIM)
    st_p = (1, BATCH, SSM_GROUPS, SSM_STATE)
    st_s = (1, DEC_BATCH, SSM_GROUPS, SSM_STATE)
    return (yp, _to_bt(xs), mk.reshape(kv_shape), mv.reshape(kv_shape),
            hp_re.reshape(st_p), hp_im.reshape(st_p), hs_re.reshape(st_s), hs_im.reshape(st_s),
            _to_bt(v_rows)[None])
```

```python
import functools
import math

import jax
import jax.numpy as jnp
from jax import lax
from jax.experimental import pallas as pl
from jax.experimental.pallas import tpu as pltpu

D_MODEL = 1024
BATCH = 8
SEQ = 2048
DEC_BATCH = 128
DEC_SEQ = 4
BRANCH = 1536
N_XHEADS = 4
XHEAD_DIM = 128
XATT = N_XHEADS * XHEAD_DIM
MIX_WIDTH = BRANCH + XATT
N_MEM = 256
CHUNK = 128
A_GROUPS = 8
A_GDIM = BRANCH // A_GROUPS
SSM_GCH = 16
SSM_GROUPS = 96
SSM_STATE = 64
EPS = 1e-6

LANES = 128
S5_TILE_GROUPS = LANES // SSM_GCH
S5_TILES = SSM_GROUPS // S5_TILE_GROUPS
S5_HALF = S5_TILE_GROUPS * SSM_STATE
VMEM_LIMIT = 56 * 1024 * 1024

F32 = jnp.float32
BF16 = jnp.bfloat16


def _dot(a, b):
    return jnp.dot(a, b, preferred_element_type=F32)


def _rms(x, g):
    return x * lax.rsqrt(jnp.mean(x * x, axis=-1, keepdims=True) + EPS) * g


def _layer_norm(x, g, b):
    mu = jnp.mean(x, axis=-1, keepdims=True)
    xc = x - mu
    var = jnp.mean(xc * xc, axis=-1, keepdims=True)
    return xc * lax.rsqrt(var + EPS) * g + b


def _gelu(x):
    c = math.sqrt(2.0 / math.pi)
    return x * (0.5 * (1.0 + jnp.tanh(c * (x + 0.044715 * (x * x * x)))))


def _sigmoid(x):
    return 1.0 / (1.0 + jnp.exp(-x))


def _silu(x):
    return x * _sigmoid(x)


def _dot_nt(a, b):
    return lax.dot_general(a, b, (((1,), (1,)), ((), ())), preferred_element_type=F32)


def _attend(qs, ks, vs):
    r = qs[0].shape[0]
    s = jnp.concatenate([_dot_nt(q.astype(BF16), k) for q, k in zip(qs, ks)], axis=0)
    s = s * (XHEAD_DIM ** -0.5)
    p = jnp.exp(s - jnp.max(s, axis=-1, keepdims=True))
    inv = 1.0 / jnp.sum(p, axis=-1, keepdims=True)
    pb = p.astype(BF16)
    return [_dot(pb[i * r:(i + 1) * r], v) * inv[i * r:(i + 1) * r] for i, v in enumerate(vs)]


def _head(a, h):
    return a[:, h * XHEAD_DIM:(h + 1) * XHEAD_DIM]


def _const_spec(shape):
    nd = len(shape)
    return pl.BlockSpec(shape, lambda *_: (0,) * nd, pipeline_mode=pl.Buffered(1))


def _s5_prep_kernel(lr_ref, li_ref, ldt_ref, br_ref, bi_ref, cr_ref, ci_ref,
                    ar_ref, ai_ref, bbd_ref, cbd_ref):
    lr = lr_ref[0]
    li = li_ref[0]
    dt = jnp.exp(ldt_ref[0])
    e = jnp.exp(lr * dt)
    ar = e * jnp.cos(li * dt)
    ai = e * jnp.sin(li * dt)
    ar_ref[0] = ar
    ai_ref[0] = ai
    nr = ar - 1.0
    den = lr * lr + li * li
    kr = (nr * lr + ai * li) / den
    ki = (ai * lr - nr * li) / den
    br = br_ref[0]
    bi = bi_ref[0]
    row_g = lax.broadcasted_iota(jnp.int32, br.shape, 0) // SSM_GCH
    col_g = lax.broadcasted_iota(jnp.int32, br.shape, 1) // SSM_STATE
    own = row_g == col_g
    bbd_ref[0, :, 0:S5_HALF] = jnp.where(own, kr * br - ki * bi, 0.0).astype(BF16)
    bbd_ref[0, :, S5_HALF:] = jnp.where(own, kr * bi + ki * br, 0.0).astype(BF16)
    cr = cr_ref[0]
    ci = ci_ref[0]
    row_g = lax.broadcasted_iota(jnp.int32, cr.shape, 0) // SSM_STATE
    col_g = lax.broadcasted_iota(jnp.int32, cr.shape, 1) // SSM_GCH
    own = row_g == col_g
    cbd_ref[0, 0:S5_HALF, :] = jnp.where(own, cr, 0.0).astype(BF16)
    cbd_ref[0, S5_HALF:, :] = jnp.where(own, -ci, 0.0).astype(BF16)


def _s5_prep(lam_re, lam_im, log_dt, b_re, b_im, c_re, c_im):
    nt, tg = S5_TILES, S5_TILE_GROUPS
    lr = lam_re.reshape(nt, 1, S5_HALF)
    li = lam_im.reshape(nt, 1, S5_HALF)
    ldt = jnp.broadcast_to(log_dt[:, None], (SSM_GROUPS, SSM_STATE)).reshape(nt, 1, S5_HALF)

    def expand_b(b):
        t = b.reshape(nt, tg, SSM_STATE, SSM_GCH).transpose(0, 3, 1, 2)
        t = jnp.broadcast_to(t[:, None], (nt, tg, SSM_GCH, tg, SSM_STATE))
        return t.reshape(nt, LANES, S5_HALF)

    def expand_c(c):
        t = c.reshape(nt, tg, SSM_GCH, SSM_STATE).transpose(0, 1, 3, 2)
        t = jnp.broadcast_to(t[:, :, :, None], (nt, tg, SSM_STATE, tg, SSM_GCH))
        return t.reshape(nt, S5_HALF, LANES)

    vec = pl.BlockSpec((1, 1, S5_HALF), lambda j: (j, 0, 0))
    bsp = pl.BlockSpec((1, LANES, S5_HALF), lambda j: (j, 0, 0))
    csp = pl.BlockSpec((1, S5_HALF, LANES), lambda j: (j, 0, 0))
    return pl.pallas_call(
        _s5_prep_kernel,
        grid=(nt,),
        in_specs=[vec, vec, vec, bsp, bsp, csp, csp],
        out_specs=[vec, vec,
                   pl.BlockSpec((1, LANES, 2 * S5_HALF), lambda j: (j, 0, 0)),
                   pl.BlockSpec((1, 2 * S5_HALF, LANES), lambda j: (j, 0, 0))],
        out_shape=[jax.ShapeDtypeStruct((nt, 1, S5_HALF), F32),
                   jax.ShapeDtypeStruct((nt, 1, S5_HALF), F32),
                   jax.ShapeDtypeStruct((nt, LANES, 2 * S5_HALF), BF16),
                   jax.ShapeDtypeStruct((nt, 2 * S5_HALF, LANES), BF16)],
        name="s5_prep",
    )(lr, li, ldt, expand_b(b_re), expand_b(b_im), expand_c(c_re), expand_c(c_im))


def _mem_kv_kernel(mem_ref, g_ref, wk_ref, wv_ref, k_ref, v_ref, kb_ref, vb_ref):
    m = _rms(mem_ref[...], g_ref[...]).astype(BF16)
    k = _dot(m, wk_ref[...])
    v = _dot(m, wv_ref[...])
    kb_ref[...] = k.astype(BF16)
    vb_ref[...] = v.astype(BF16)
    for b in range(k_ref.shape[0]):
        for h in range(N_XHEADS):
            rows = pl.ds(h, N_MEM, stride=N_XHEADS)
            k_ref[b, rows, :] = _head(k[b * N_MEM:(b + 1) * N_MEM], h)
            v_ref[b, rows, :] = _head(v[b * N_MEM:(b + 1) * N_MEM], h)


def _mem_kv(mem, g, wk, wv):
    rows = mem.shape[0]
    depth = g.shape[0]
    nb = 2
    tm = nb * N_MEM
    out = pl.BlockSpec((None, nb, N_MEM * N_XHEADS, XHEAD_DIM), lambda i, r: (i, r, 0, 0))
    outb = pl.BlockSpec((None, tm, XATT), lambda i, r: (i, r, 0))
    wsp = pl.BlockSpec((None, D_MODEL, XATT), lambda i, r: (i, 0, 0))
    return pl.pallas_call(
        _mem_kv_kernel,
        grid=(depth, rows // tm),
        in_specs=[pl.BlockSpec((tm, D_MODEL), lambda i, r: (r, 0)),
                  pl.BlockSpec((None, 1, D_MODEL), lambda i, r: (i, 0, 0)),
                  wsp, wsp],
        out_specs=[out, out, outb, outb],
        out_shape=[jax.ShapeDtypeStruct((depth, rows // N_MEM, N_MEM * N_XHEADS, XHEAD_DIM), F32)] * 2
        + [jax.ShapeDtypeStruct((depth, rows, XATT), BF16)] * 2,
        name="mem_kv",
    )(mem, g, wk, wv)


def _masked_spatial(ws_ref):
    row = lax.broadcasted_iota(jnp.int32, (CHUNK, CHUNK), 0)
    col = lax.broadcasted_iota(jnp.int32, (CHUNK, CHUNK), 1)
    return [jnp.where(row >= col, ws_ref[g], 0.0).astype(BF16) for g in range(A_GROUPS)]


def _gate_chunk(wm, vc, pair):
    base = pair * 2 * A_GDIM
    even = _dot(wm[2 * pair], vc[:, base:base + 2 * LANES])
    odd = _dot(wm[2 * pair + 1], vc[:, base + LANES:base + 3 * LANES])
    lane = lax.broadcasted_iota(jnp.int32, (CHUNK, LANES), 1)
    mid = jnp.where(lane < A_GDIM - LANES, even[:, LANES:], odd[:, :LANES])
    return jnp.concatenate([even[:, :LANES], mid, odd[:, LANES:]], axis=1)


def _s5_tile(u, hr, hi, ar, ai, bbd, cbd, dvec, bu_ref, hs_ref, steps, nb):
    bu_ref[...] = _dot(u.astype(BF16), bbd)
    ar = jnp.broadcast_to(ar, (nb, S5_HALF))
    ai = jnp.broadcast_to(ai, (nb, S5_HALF))

    def step(t, carry):
        hr, hi = carry
        r0 = pl.multiple_of(t * nb, nb)
        nr = ar * hr - ai * hi + bu_ref[pl.ds(r0, nb), 0:S5_HALF]
        ni = ar * hi + ai * hr + bu_ref[pl.ds(r0, nb), S5_HALF:]
        hs_ref[pl.ds(r0, nb), 0:S5_HALF] = nr
        hs_ref[pl.ds(r0, nb), S5_HALF:] = ni
        return nr, ni

    hr, hi = lax.fori_loop(0, steps, step, (hr, hi))
    y = _dot(hs_ref[...].astype(BF16), cbd) + dvec * u
    return y, hr, hi


def _layer_a_prompt_kernel(x_ref, k_ref, v_ref, win_ref, lng_ref, lnb_ref, ws_ref, bias_ref,
                           wout_ref, pre_ref, post_ref, o_ref, m_ref):
    tm = x_ref.shape[0]
    x = x_ref[...]
    h = _rms(x, pre_ref[...]).astype(BF16)
    gate = _silu(_dot(h, win_ref[:, 2 * BRANCH + XATT:]))
    u = _gelu(_dot(h, win_ref[:, 0:BRANCH]))
    v = _layer_norm(_gelu(_dot(h, win_ref[:, BRANCH:2 * BRANCH])), lng_ref[...], lnb_ref[...])
    vb = v.astype(BF16)
    wm = _masked_spatial(ws_ref)
    for c in range(tm // CHUNK):
        rs = slice(c * CHUNK, (c + 1) * CHUNK)
        vc = vb[rs]
        for pair in range(A_GROUPS // 2):
            cs = slice(pair * 2 * A_GDIM, (pair + 1) * 2 * A_GDIM)
            mixed = _gate_chunk(wm, vc, pair) + bias_ref[:, cs]
            m_ref[rs, cs] = (u[rs, cs] * mixed * gate[rs, cs]).astype(BF16)
    q = _dot(h, win_ref[:, 2 * BRANCH:2 * BRANCH + XATT])
    heads = range(N_XHEADS)
    k = k_ref[...]
    v = v_ref[...]
    att = _attend([_head(q, h) for h in heads], [_head(k, h) for h in heads], [_head(v, h) for h in heads])
    for h in heads:
        cs = slice(BRANCH + h * XHEAD_DIM, BRANCH + (h + 1) * XHEAD_DIM)
        m_ref[:, cs] = (att[h] * gate[:, cs]).astype(BF16)
    out = _dot(m_ref[...], wout_ref[...])
    o_ref[...] = x + _rms(out, post_ref[...])


def _layer_a_prompt(x, kb, vb, win, lng, lnb, ws, bias, wout, pre, post):
    tm = 256
    nb, seq, _ = x.shape
    xs = pl.BlockSpec((None, tm, D_MODEL), lambda b, t: (b, t, 0))
    kv = pl.BlockSpec((None, N_MEM, XATT), lambda b, t: (b, 0, 0))
    return pl.pallas_call(
        _layer_a_prompt_kernel,
        grid=(nb, seq // tm),
        in_specs=[xs, kv, kv, _const_spec(win.shape), _const_spec(lng.shape), _const_spec(lnb.shape),
                  _const_spec(ws.shape), _const_spec(bias.shape), _const_spec(wout.shape),
                  _const_spec(pre.shape), _const_spec(post.shape)],
        out_specs=xs,
        out_shape=jax.ShapeDtypeStruct(x.shape, F32),
        scratch_shapes=[pltpu.VMEM((tm, MIX_WIDTH), BF16)],
        compiler_params=pltpu.CompilerParams(
            dimension_semantics=("arbitrary", "arbitrary"), vmem_limit_bytes=VMEM_LIMIT),
        name="layer_a_prompt",
    )(x, kb, vb, win, lng, lnb, ws, bias, wout, pre, post)


def _layer_b_prompt_kernel(x_ref, k_ref, v_ref, win_ref, ar_ref, ai_ref, bbd_ref, cbd_ref, d_ref,
                           wglu_ref, bglu_ref, wout_ref, pre_ref, post_ref,
                           o_ref, hre_ref, him_ref,
                           ut_ref, bu_ref, hs_ref, yt_ref, y_ref, m_ref):
    nb, tc, _ = x_ref.shape
    tm = nb * tc

    @pl.when(pl.program_id(0) == 0)
    def _():
        hre_ref[...] = jnp.zeros_like(hre_ref)
        him_ref[...] = jnp.zeros_like(him_ref)

    x = x_ref[...].reshape(tm, D_MODEL)
    h = _rms(x, pre_ref[...]).astype(BF16)
    u = _dot(h, win_ref[:, 0:BRANCH])
    for j in range(S5_TILES):
        for b in range(nb):
            ut_ref[j, pl.ds(b, tc, stride=nb), :] = u[b * tc:(b + 1) * tc, j * LANES:(j + 1) * LANES]
    for j in range(S5_TILES):
        ss = slice(j * S5_HALF, (j + 1) * S5_HALF)
        y, hr, hi = _s5_tile(ut_ref[j], hre_ref[:, ss], him_ref[:, ss], ar_ref[j], ai_ref[j],
                             bbd_ref[j], cbd_ref[j], d_ref[j], bu_ref, hs_ref, tc, nb)
        hre_ref[:, ss] = hr
        him_ref[:, ss] = hi
        yt_ref[...] = y
        for b in range(nb):
            y_ref[b * tc:(b + 1) * tc, j * LANES:(j + 1) * LANES] = yt_ref[pl.ds(b, tc, stride=nb), :]
    y = _gelu(y_ref[...])
    branch = y * _sigmoid(_dot(y.astype(BF16), wglu_ref[...]) + bglu_ref[...])
    gate = _silu(_dot(h, win_ref[:, BRANCH + XATT:]))
    m_ref[:, 0:BRANCH] = (branch * gate[:, 0:BRANCH]).astype(BF16)
    q = _dot(h, win_ref[:, BRANCH:BRANCH + XATT])
    pieces = [(b, h) for b in range(nb) for h in range(N_XHEADS)]
    att = _attend([_head(q[b * tc:(b + 1) * tc], h) for b, h in pieces],
                  [_head(k_ref[b * N_MEM:(b + 1) * N_MEM, :], h) for b, h in pieces],
                  [_head(v_ref[b * N_MEM:(b + 1) * N_MEM, :], h) for b, h in pieces])
    for (b, h), a in zip(pieces, att):
        rs = slice(b * tc, (b + 1) * tc)
        cs = slice(BRANCH + h * XHEAD_DIM, BRANCH + (h + 1) * XHEAD_DIM)
        m_ref[rs, cs] = (a * gate[rs, cs]).astype(BF16)
    out = _dot(m_ref[...], wout_ref[...])
    o_ref[...] = (x + _rms(out, post_ref[...])).reshape(nb, tc, D_MODEL)


def _layer_b_prompt(x, kb, vb, win, ar, ai, bbd, cbd, dvec, wglu, bglu, wout, pre, post):
    tc = 32
    nb, seq, _ = x.shape
    tm = nb * tc
    xs = pl.BlockSpec((nb, tc, D_MODEL), lambda t: (0, t, 0))
    hs = pl.BlockSpec((nb, SSM_GROUPS * SSM_STATE), lambda t: (0, 0))
    consts = [kb, vb, win, ar, ai, bbd, cbd, dvec, wglu, bglu, wout, pre, post]
    return pl.pallas_call(
        _layer_b_prompt_kernel,
        grid=(seq // tc,),
        in_specs=[xs] + [_const_spec(c.shape) for c in consts],
        out_specs=[xs, hs, hs],
        out_shape=[jax.ShapeDtypeStruct(x.shape, F32),
                   jax.ShapeDtypeStruct((nb, SSM_GROUPS * SSM_STATE), F32),
                   jax.ShapeDtypeStruct((nb, SSM_GROUPS * SSM_STATE), F32)],
        scratch_shapes=[pltpu.VMEM((S5_TILES, tm, LANES), F32),
                        pltpu.VMEM((tm, 2 * S5_HALF), F32),
                        pltpu.VMEM((tm, 2 * S5_HALF), F32),
                        pltpu.VMEM((tm, LANES), F32),
                        pltpu.VMEM((tm, BRANCH), F32),
                        pltpu.VMEM((tm, MIX_WIDTH), BF16)],
        compiler_params=pltpu.CompilerParams(
            dimension_semantics=("arbitrary",), vmem_limit_bytes=VMEM_LIMIT),
        name="layer_b_prompt",
    )(x, *consts)


def _sample_pre_a_kernel(x_ref, win_ref, lng_ref, lnb_ref, wrow_ref, brow_ref, pre_ref,
                         v_ref, mbr_ref, q_ref, gatt_ref):
    nb = DEC_BATCH
    h = _rms(x_ref[...], pre_ref[...]).astype(BF16)
    gate = _silu(_dot(h, win_ref[:, 2 * BRANCH + XATT:]))
    u = _gelu(_dot(h, win_ref[:, 0:BRANCH]))
    v = _layer_norm(_gelu(_dot(h, win_ref[:, BRANCH:2 * BRANCH])), lng_ref[...], lnb_ref[...])
    v_ref[...] = v
    for t in range(DEC_SEQ):
        rs = slice(t * nb, (t + 1) * nb)
        mixed = brow_ref[t:t + 1, :]
        for s in range(t + 1):
            mixed = mixed + wrow_ref[t * DEC_SEQ + s:t * DEC_SEQ + s + 1, :] * v[s * nb:(s + 1) * nb]
        mbr_ref[rs, :] = u[rs] * mixed * gate[rs, 0:BRANCH]
    q_ref[...] = _dot(h, win_ref[:, 2 * BRANCH:2 * BRANCH + XATT])
    gatt_ref[...] = gate[:, BRANCH:]


def _sample_pre_a(x, win, lng, lnb, wrow, brow, pre):
    rows = x.shape[0]
    ins = [x, win, lng, lnb, wrow, brow, pre]
    shapes = [(rows, BRANCH), (rows, BRANCH), (rows, XATT), (rows, XATT)]
    return pl.pallas_call(
        _sample_pre_a_kernel,
        grid=(1,),
        in_specs=[_const_spec(a.shape) for a in ins],
        out_specs=[pl.BlockSpec(s, lambda i: (0, 0)) for s in shapes],
        out_shape=[jax.ShapeDtypeStruct(s, F32) for s in shapes],
        compiler_params=pltpu.CompilerParams(
            dimension_semantics=("arbitrary",), vmem_limit_bytes=VMEM_LIMIT),
        name="sample_pre_a",
    )(*ins)


def _sample_pre_b_kernel(x_ref, win_ref, sre_ref, sim_ref, ar_ref, ai_ref, bbd_ref, cbd_ref, d_ref,
                         wglu_ref, bglu_ref, pre_ref,
                         mbr_ref, q_ref, gatt_ref, hre_ref, him_ref,
                         bu_ref, hs_ref, y_ref):
    nb = DEC_BATCH
    h = _rms(x_ref[...], pre_ref[...]).astype(BF16)
    u = _dot(h, win_ref[:, 0:BRANCH])
    for j in range(S5_TILES):
        ss = slice(j * S5_HALF, (j + 1) * S5_HALF)
        ls = slice(j * LANES, (j + 1) * LANES)
        y, hr, hi = _s5_tile(u[:, ls], sre_ref[:, ss], sim_ref[:, ss], ar_ref[j], ai_ref[j],
                             bbd_ref[j], cbd_ref[j], d_ref[j], bu_ref, hs_ref, DEC_SEQ, nb)
        hre_ref[:, ss] = hr
        him_ref[:, ss] = hi
        y_ref[:, ls] = y
    y = _gelu(y_ref[...])
    branch = y * _sigmoid(_dot(y.astype(BF16), wglu_ref[...]) + bglu_ref[...])
    gate = _silu(_dot(h, win_ref[:, BRANCH + XATT:]))
    mbr_ref[...] = branch * gate[:, 0:BRANCH]
    q_ref[...] = _dot(h, win_ref[:, BRANCH:BRANCH + XATT])
    gatt_ref[...] = gate[:, BRANCH:]


def _sample_pre_b(x, win, sre, sim, ar, ai, bbd, cbd, dvec, wglu, bglu, pre):
    rows = x.shape[0]
    ins = [x, win, sre, sim, ar, ai, bbd, cbd, dvec, wglu, bglu, pre]
    shapes = [(rows, BRANCH), (rows, XATT), (rows, XATT), sre.shape, sim.shape]
    return pl.pallas_call(
        _sample_pre_b_kernel,
        grid=(1,),
        in_specs=[_const_spec(a.shape) for a in ins],
        out_specs=[pl.BlockSpec(s, lambda i: (0, 0)) for s in shapes],
        out_shape=[jax.ShapeDtypeStruct(s, F32) for s in shapes],
        scratch_shapes=[pltpu.VMEM((rows, 2 * S5_HALF), F32),
                        pltpu.VMEM((rows, 2 * S5_HALF), F32),
                        pltpu.VMEM((rows, BRANCH), F32)],
        compiler_params=pltpu.CompilerParams(
            dimension_semantics=("arbitrary",), vmem_limit_bytes=VMEM_LIMIT),
        name="sample_pre_b",
    )(*ins)


def _sample_attn_kernel(q_ref, g_ref, k_ref, v_ref, o_ref):
    bb, rows, _ = q_ref.shape
    cols = k_ref.shape[1]
    row_h = (lax.broadcasted_iota(jnp.int32, (bb * rows, cols), 0) // DEC_SEQ) % N_XHEADS
    col_h = lax.broadcasted_iota(jnp.int32, (bb * rows, cols), 1) % N_XHEADS
    s = jnp.concatenate([_dot_nt(q_ref[j].astype(BF16), k_ref[j].astype(BF16)) for j in range(bb)], axis=0)
    s = jnp.where(row_h == col_h, s * (XHEAD_DIM ** -0.5), -1e30)
    p = jnp.exp(s - jnp.max(s, axis=-1, keepdims=True))
    inv = 1.0 / jnp.sum(p, axis=-1, keepdims=True)
    pb = p.astype(BF16)
    for j in range(bb):
        rs = slice(j * rows, (j + 1) * rows)
        o_ref[j] = _dot(pb[rs], v_ref[j].astype(BF16)) * inv[rs] * g_ref[j]


def _sample_attn(q, gatt, cache_k, cache_v, layer):
    bb = 8
    qs = pl.BlockSpec((bb,) + q.shape[1:], lambda i: (i, 0, 0))
    cs = pl.BlockSpec((None, bb) + cache_k.shape[2:], lambda i: (layer, i, 0, 0))
    return pl.pallas_call(
        _sample_attn_kernel,
        grid=(q.shape[0] // bb,),
        in_specs=[qs, qs, cs, cs],
        out_specs=qs,
        out_shape=jax.ShapeDtypeStruct(q.shape, F32),
        compiler_params=pltpu.CompilerParams(
            dimension_semantics=("arbitrary",), vmem_limit_bytes=VMEM_LIMIT),
        name="sample_attn",
    )(q, gatt, cache_k, cache_v)


def _sample_post_kernel(x_ref, mbr_ref, matt_ref, wout_ref, post_ref, o_ref):
    out = (_dot(mbr_ref[...].astype(BF16), wout_ref[0:BRANCH, :])
           + _dot(matt_ref[...].astype(BF16), wout_ref[BRANCH:, :]))
    o_ref[...] = x_ref[...] + _rms(out, post_ref[...])


def _sample_post(x, mbr, matt, wout, post):
    ins = [x, mbr, matt, wout, post]
    return pl.pallas_call(
        _sample_post_kernel,
        grid=(1,),
        in_specs=[_const_spec(a.shape) for a in ins],
        out_specs=pl.BlockSpec(x.shape, lambda i: (0, 0)),
        out_shape=jax.ShapeDtypeStruct(x.shape, F32),
        compiler_params=pltpu.CompilerParams(
            dimension_semantics=("arbitrary",), vmem_limit_bytes=VMEM_LIMIT),
        name="sample_post",
    )(*ins)


def _to_bt(a):
    return a.reshape(DEC_SEQ, DEC_BATCH, a.shape[-1]).transpose(1, 0, 2)


def _to_tb(a):
    return a.transpose(1, 0, 2).reshape(DEC_SEQ * DEC_BATCH, a.shape[-1])


def _to_bht(a):
    a = a.reshape(DEC_SEQ, DEC_BATCH, N_XHEADS, XHEAD_DIM).transpose(1, 2, 0, 3)
    return a.reshape(DEC_BATCH, N_XHEADS * DEC_SEQ, XHEAD_DIM)


def _from_bht(a):
    a = a.reshape(DEC_BATCH, N_XHEADS, DEC_SEQ, XHEAD_DIM).transpose(2, 0, 1, 3)
    return a.reshape(DEC_SEQ * DEC_BATCH, XATT)


def kernel(x_prompt, x_sample, cache_mem_k, cache_mem_v, state_ssm_re, state_ssm_im, mem_prompt,
           w_in_a, ln_v_g, ln_v_b, w_spatial, b_spatial,
           w_in_b, ssm_lambda_re, ssm_lambda_im, ssm_log_dt, ssm_b_re, ssm_b_im, ssm_c_re, ssm_c_im,
           ssm_d, w_glu, b_glu,
           mem_norm_g, w_mem_k, w_mem_v, w_out, pre_norm_g, post_norm_g):
    depth = w_out.shape[0]
    win_a = w_in_a[0].astype(BF16)
    win_b = w_in_b[0].astype(BF16)
    wglu = w_glu[0].astype(BF16)
    wout = w_out.astype(BF16)
    wk = w_mem_k.astype(BF16)
    wv = w_mem_v.astype(BF16)
    pre = pre_norm_g.reshape(depth, 1, D_MODEL)
    post = post_norm_g.reshape(depth, 1, D_MODEL)
    lng = ln_v_g[0].reshape(1, BRANCH)
    lnb = ln_v_b[0].reshape(1, BRANCH)
    bglu = b_glu[0].reshape(1, BRANCH)
    dvec = ssm_d[0].reshape(S5_TILES, 1, LANES)
    bias = jnp.repeat(b_spatial[0].T, A_GDIM, axis=1)
    ws4 = w_spatial[0][:, :DEC_SEQ, :DEC_SEQ]
    wrow = jnp.repeat(ws4.transpose(1, 2, 0).reshape(DEC_SEQ * DEC_SEQ, A_GROUPS), A_GDIM, axis=1)
    brow = bias[:DEC_SEQ]

    ar, ai, bbd, cbd = _s5_prep(ssm_lambda_re[0], ssm_lambda_im[0], ssm_log_dt[0],
                                ssm_b_re[0], ssm_b_im[0], ssm_c_re[0], ssm_c_im[0])

    mem = mem_prompt.reshape(BATCH * N_MEM, D_MODEL)
    mk, mv, mkb, mvb = _mem_kv(mem, mem_norm_g.reshape(depth, 1, D_MODEL), wk, wv)

    yp = _layer_a_prompt(x_prompt, mkb[0].reshape(BATCH, N_MEM, XATT), mvb[0].reshape(BATCH, N_MEM, XATT),
                         win_a, lng, lnb, w_spatial[0], bias, wout[0], pre[0], post[0])
    yp, hp_re, hp_im = _layer_b_prompt(yp, mkb[1], mvb[1], win_b, ar, ai, bbd, cbd, dvec,
                                       wglu, bglu, wout[1], pre[1], post[1])

    ck = cache_mem_k.reshape(depth, DEC_BATCH, N_MEM * N_XHEADS, XHEAD_DIM)
    cv = cache_mem_v.reshape(depth, DEC_BATCH, N_MEM * N_XHEADS, XHEAD_DIM)
    xs = _to_tb(x_sample)
    v_rows, mbr, q, gatt = _sample_pre_a(xs, win_a, lng, lnb, wrow, brow, pre[0])
    matt = _sample_attn(_to_bht(q), _to_bht(gatt), ck, cv, 0)
    xs = _sample_post(xs, mbr, _from_bht(matt), wout[0], post[0])
    sre = state_ssm_re[0].reshape(DEC_BATCH, SSM_GROUPS * SSM_STATE)
    sim = state_ssm_im[0].reshape(DEC_BATCH, SSM_GROUPS * SSM_STATE)
    mbr, q, gatt, hs_re, hs_im = _sample_pre_b(xs, win_b, sre, sim, ar, ai, bbd, cbd, dvec,
                                               wglu, bglu, pre[1])
    matt = _sample_attn(_to_bht(q), _to_bht(gatt), ck, cv, 1)
    xs = _sample_post(xs, mbr, _from_bht(matt), wout[1], post[1])

    kv_shape = (depth, BATCH, N_MEM, N_XHEADS, XHEAD_DIM)
    st_p = (1, BATCH, SSM_GROUPS, SSM_STATE)
    st_s = (1, DEC_BATCH, SSM_GROUPS, SSM_STATE)
    return (yp, _to_bt(xs), mk.reshape(kv_shape), mv.reshape(kv_shape),
            hp_re.reshape(st_p), hp_im.reshape(st_p), hs_re.reshape(st_s), hs_im.reshape(st_s),
            _to_bt(v_rows)[None])
```

```python
import functools
import math

import jax
import jax.numpy as jnp
from jax import lax
from jax.experimental import pallas as pl
from jax.experimental.pallas import tpu as pltpu

D_MODEL = 1024
BATCH = 8
SEQ = 2048
DEC_BATCH = 128
DEC_SEQ = 4
BRANCH = 1536
N_XHEADS = 4
XHEAD_DIM = 128
XATT = N_XHEADS * XHEAD_DIM
MIX_WIDTH = BRANCH + XATT
N_MEM = 256
CHUNK = 128
A_GROUPS = 8
A_GDIM = BRANCH // A_GROUPS
SSM_GCH = 16
SSM_GROUPS = 96
SSM_STATE = 64
EPS = 1e-6

LANES = 128
S5_TILE_GROUPS = LANES // SSM_GCH
S5_TILES = SSM_GROUPS // S5_TILE_GROUPS
S5_HALF = S5_TILE_GROUPS * SSM_STATE
VMEM_LIMIT = 56 * 1024 * 1024

F32 = jnp.float32
BF16 = jnp.bfloat16


def _dot(a, b):
    return jnp.dot(a, b, preferred_element_type=F32)


def _rms(x, g):
    return x * lax.rsqrt(jnp.mean(x * x, axis=-1, keepdims=True) + EPS) * g


def _layer_norm(x, g, b):
    mu = jnp.mean(x, axis=-1, keepdims=True)
    xc = x - mu
    var = jnp.mean(xc * xc, axis=-1, keepdims=True)
    return xc * lax.rsqrt(var + EPS) * g + b


def _gelu(x):
    c = math.sqrt(2.0 / math.pi)
    return x * (0.5 * (1.0 + jnp.tanh(c * (x + 0.044715 * (x * x * x)))))


def _sigmoid(x):
    return 1.0 / (1.0 + jnp.exp(-x))


def _silu(x):
    return x * _sigmoid(x)


def _dot_nt(a, b):
    return lax.dot_general(a, b, (((1,), (1,)), ((), ())), preferred_element_type=F32)


def _attend(qs, ks, vs):
    r = qs[0].shape[0]
    s = jnp.concatenate([_dot_nt(q.astype(BF16), k) for q, k in zip(qs, ks)], axis=0)
    s = s * (XHEAD_DIM ** -0.5)
    p = jnp.exp(s - jnp.max(s, axis=-1, keepdims=True))
    inv = 1.0 / jnp.sum(p, axis=-1, keepdims=True)
    pb = p.astype(BF16)
    return [_dot(pb[i * r:(i + 1) * r], v) * inv[i * r:(i + 1) * r] for i, v in enumerate(vs)]


def _head(a, h):
    return a[:, h * XHEAD_DIM:(h + 1) * XHEAD_DIM]


def _const_spec(shape):
    nd = len(shape)
    return pl.BlockSpec(shape, lambda *_: (0,) * nd, pipeline_mode=pl.Buffered(1))


def _s5_prep_kernel(lr_ref, li_ref, ldt_ref, br_ref, bi_ref, cr_ref, ci_ref,
                    ar_ref, ai_ref, bbd_ref, cbd_ref):
    lr = lr_ref[0]
    li = li_ref[0]
    dt = jnp.exp(ldt_ref[0])
    e = jnp.exp(lr * dt)
    ar = e * jnp.cos(li * dt)
    ai = e * jnp.sin(li * dt)
    ar_ref[0] = ar
    ai_ref[0] = ai
    nr = ar - 1.0
    den = lr * lr + li * li
    kr = (nr * lr + ai * li) / den
    ki = (ai * lr - nr * li) / den
    br = br_ref[0]
    bi = bi_ref[0]
    row_g = lax.broadcasted_iota(jnp.int32, br.shape, 0) // SSM_GCH
    col_g = lax.broadcasted_iota(jnp.int32, br.shape, 1) // SSM_STATE
    own = row_g == col_g
    bbd_ref[0, :, 0:S5_HALF] = jnp.where(own, kr * br - ki * bi, 0.0).astype(BF16)
    bbd_ref[0, :, S5_HALF:] = jnp.where(own, kr * bi + ki * br, 0.0).astype(BF16)
    cr = cr_ref[0]
    ci = ci_ref[0]
    row_g = lax.broadcasted_iota(jnp.int32, cr.shape, 0) // SSM_STATE
    col_g = lax.broadcasted_iota(jnp.int32, cr.shape, 1) // SSM_GCH
    own = row_g == col_g
    cbd_ref[0, 0:S5_HALF, :] = jnp.where(own, cr, 0.0).astype(BF16)
    cbd_ref[0, S5_HALF:, :] = jnp.where(own, -ci, 0.0).astype(BF16)


def _s5_prep(lam_re, lam_im, log_dt, b_re, b_im, c_re, c_im):
    nt, tg = S5_TILES, S5_TILE_GROUPS
    lr = lam_re.reshape(nt, 1, S5_HALF)
    li = lam_im.reshape(nt, 1, S5_HALF)
    ldt = jnp.broadcast_to(log_dt[:, None], (SSM_GROUPS, SSM_STATE)).reshape(nt, 1, S5_HALF)

    def expand_b(b):
        t = b.reshape(nt, tg, SSM_STATE, SSM_GCH).transpose(0, 3, 1, 2)
        t = jnp.broadcast_to(t[:, None], (nt, tg, SSM_GCH, tg, SSM_STATE))
        return t.reshape(nt, LANES, S5_HALF)

    def expand_c(c):
        t = c.reshape(nt, tg, SSM_GCH, SSM_STATE).transpose(0, 1, 3, 2)
        t = jnp.broadcast_to(t[:, :, :, None], (nt, tg, SSM_STATE, tg, SSM_GCH))
        return t.reshape(nt, S5_HALF, LANES)

    vec = pl.BlockSpec((1, 1, S5_HALF), lambda j: (j, 0, 0))
    bsp = pl.BlockSpec((1, LANES, S5_HALF), lambda j: (j, 0, 0))
    csp = pl.BlockSpec((1, S5_HALF, LANES), lambda j: (j, 0, 0))
    return pl.pallas_call(
        _s5_prep_kernel,
        grid=(nt,),
        in_specs=[vec, vec, vec, bsp, bsp, csp, csp],
        out_specs=[vec, vec,
                   pl.BlockSpec((1, LANES, 2 * S5_HALF), lambda j: (j, 0, 0)),
                   pl.BlockSpec((1, 2 * S5_HALF, LANES), lambda j: (j, 0, 0))],
        out_shape=[jax.ShapeDtypeStruct((nt, 1, S5_HALF), F32),
                   jax.ShapeDtypeStruct((nt, 1, S5_HALF), F32),
                   jax.ShapeDtypeStruct((nt, LANES, 2 * S5_HALF), BF16),
                   jax.ShapeDtypeStruct((nt, 2 * S5_HALF, LANES), BF16)],
        name="s5_prep",
    )(lr, li, ldt, expand_b(b_re), expand_b(b_im), expand_c(c_re), expand_c(c_im))


def _mem_kv_kernel(mem_ref, g_ref, wk_ref, wv_ref, k_ref, v_ref, kb_ref, vb_ref):
    m = _rms(mem_ref[...], g_ref[...]).astype(BF16)
    k = _dot(m, wk_ref[...])
    v = _dot(m, wv_ref[...])
    kb_ref[...] = k.astype(BF16)
    vb_ref[...] = v.astype(BF16)
    for b in range(k_ref.shape[0]):
        for h in range(N_XHEADS):
            rows = pl.ds(h, N_MEM, stride=N_XHEADS)
            k_ref[b, rows, :] = _head(k[b * N_MEM:(b + 1) * N_MEM], h)
            v_ref[b, rows, :] = _head(v[b * N_MEM:(b + 1) * N_MEM], h)


def _mem_kv(mem, g, wk, wv):
    rows = mem.shape[0]
    depth = g.shape[0]
    nb = 2
    tm = nb * N_MEM
    out = pl.BlockSpec((None, nb, N_MEM * N_XHEADS, XHEAD_DIM), lambda i, r: (i, r, 0, 0))
    outb = pl.BlockSpec((None, tm, XATT), lambda i, r: (i, r, 0))
    wsp = pl.BlockSpec((None, D_MODEL, XATT), lambda i, r: (i, 0, 0))
    return pl.pallas_call(
        _mem_kv_kernel,
        grid=(depth, rows // tm),
        in_specs=[pl.BlockSpec((tm, D_MODEL), lambda i, r: (r, 0)),
                  pl.BlockSpec((None, 1, D_MODEL), lambda i, r: (i, 0, 0)),
                  wsp, wsp],
        out_specs=[out, out, outb, outb],
        out_shape=[jax.ShapeDtypeStruct((depth, rows // N_MEM, N_MEM * N_XHEADS, XHEAD_DIM), F32)] * 2
        + [jax.ShapeDtypeStruct((depth, rows, XATT), BF16)] * 2,
        name="mem_kv",
    )(mem, g, wk, wv)


def _masked_spatial(ws_ref):
    row = lax.broadcasted_iota(jnp.int32, (CHUNK, CHUNK), 0)
    col = lax.broadcasted_iota(jnp.int32, (CHUNK, CHUNK), 1)
    return [jnp.where(row >= col, ws_ref[g], 0.0).astype(BF16) for g in range(A_GROUPS)]


def _gate_chunk(wm, vc, pair):
    base = pair * 2 * A_GDIM
    even = _dot(wm[2 * pair], vc[:, base:base + 2 * LANES])
    odd = _dot(wm[2 * pair + 1], vc[:, base + LANES:base + 3 * LANES])
    lane = lax.broadcasted_iota(jnp.int32, (CHUNK, LANES), 1)
    mid = jnp.where(lane < A_GDIM - LANES, even[:, LANES:], odd[:, :LANES])
    return jnp.concatenate([even[:, :LANES], mid, odd[:, LANES:]], axis=1)


def _s5_tile(u, hr, hi, ar, ai, bbd, cbd, dvec, bu_ref, hs_ref, steps, nb):
    bu_ref[...] = _dot(u.astype(BF16), bbd)
    ar = jnp.broadcast_to(ar, (nb, S5_HALF))
    ai = jnp.broadcast_to(ai, (nb, S5_HALF))

    for t in range(steps):
        rows = slice(t * nb, (t + 1) * nb)
        hr, hi = (ar * hr - ai * hi + bu_ref[rows, 0:S5_HALF],
                  ar * hi + ai * hr + bu_ref[rows, S5_HALF:])
        hs_ref[rows, 0:S5_HALF] = hr
        hs_ref[rows, S5_HALF:] = hi
    y = _dot(hs_ref[...].astype(BF16), cbd) + dvec * u
    return y, hr, hi


def _layer_a_prompt_kernel(x_ref, k_ref, v_ref, win_ref, lng_ref, lnb_ref, ws_ref, bias_ref,
                           wout_ref, pre_ref, post_ref, o_ref, m_ref):
    tm = x_ref.shape[0]
    x = x_ref[...]
    h = _rms(x, pre_ref[...]).astype(BF16)
    gate = _silu(_dot(h, win_ref[:, 2 * BRANCH + XATT:]))
    u = _gelu(_dot(h, win_ref[:, 0:BRANCH]))
    v = _layer_norm(_gelu(_dot(h, win_ref[:, BRANCH:2 * BRANCH])), lng_ref[...], lnb_ref[...])
    vb = v.astype(BF16)
    wm = _masked_spatial(ws_ref)
    for c in range(tm // CHUNK):
        rs = slice(c * CHUNK, (c + 1) * CHUNK)
        vc = vb[rs]
        for pair in range(A_GROUPS // 2):
            cs = slice(pair * 2 * A_GDIM, (pair + 1) * 2 * A_GDIM)
            mixed = _gate_chunk(wm, vc, pair) + bias_ref[:, cs]
            m_ref[rs, cs] = (u[rs, cs] * mixed * gate[rs, cs]).astype(BF16)
    q = _dot(h, win_ref[:, 2 * BRANCH:2 * BRANCH + XATT])
    heads = range(N_XHEADS)
    k = k_ref[...]
    v = v_ref[...]
    att = _attend([_head(q, h) for h in heads], [_head(k, h) for h in heads], [_head(v, h) for h in heads])
    for h in heads:
        cs = slice(BRANCH + h * XHEAD_DIM, BRANCH + (h + 1) * XHEAD_DIM)
        m_ref[:, cs] = (att[h] * gate[:, cs]).astype(BF16)
    out = _dot(m_ref[...], wout_ref[...])
    o_ref[...] = x + _rms(out, post_ref[...])


def _layer_a_prompt(x, kb, vb, win, lng, lnb, ws, bias, wout, pre, post):
    tm = 256
    nb, seq, _ = x.shape
    xs = pl.BlockSpec((None, tm, D_MODEL), lambda b, t: (b, t, 0))
    kv = pl.BlockSpec((None, N_MEM, XATT), lambda b, t: (b, 0, 0))
    return pl.pallas_call(
        _layer_a_prompt_kernel,
        grid=(nb, seq // tm),
        in_specs=[xs, kv, kv, _const_spec(win.shape), _const_spec(lng.shape), _const_spec(lnb.shape),
                  _const_spec(ws.shape), _const_spec(bias.shape), _const_spec(wout.shape),
                  _const_spec(pre.shape), _const_spec(post.shape)],
        out_specs=xs,
        out_shape=jax.ShapeDtypeStruct(x.shape, F32),
        scratch_shapes=[pltpu.VMEM((tm, MIX_WIDTH), BF16)],
        compiler_params=pltpu.CompilerParams(
            dimension_semantics=("arbitrary", "arbitrary"), vmem_limit_bytes=VMEM_LIMIT),
        name="layer_a_prompt",
    )(x, kb, vb, win, lng, lnb, ws, bias, wout, pre, post)


def _layer_b_prompt_kernel(x_ref, k_ref, v_ref, win_ref, ar_ref, ai_ref, bbd_ref, cbd_ref, d_ref,
                           wglu_ref, bglu_ref, wout_ref, pre_ref, post_ref,
                           o_ref, hre_ref, him_ref,
                           ut_ref, bu_ref, hs_ref, yt_ref, y_ref, m_ref):
    nb, tc, _ = x_ref.shape
    tm = nb * tc

    @pl.when(pl.program_id(0) == 0)
    def _():
        hre_ref[...] = jnp.zeros_like(hre_ref)
        him_ref[...] = jnp.zeros_like(him_ref)

    x = x_ref[...].reshape(tm, D_MODEL)
    h = _rms(x, pre_ref[...]).astype(BF16)
    u = _dot(h, win_ref[:, 0:BRANCH])
    for j in range(S5_TILES):
        for b in range(nb):
            ut_ref[j, pl.ds(b, tc, stride=nb), :] = u[b * tc:(b + 1) * tc, j * LANES:(j + 1) * LANES]
    for j in range(S5_TILES):
        ss = slice(j * S5_HALF, (j + 1) * S5_HALF)
        par = j % 2
        y, hr, hi = _s5_tile(ut_ref[j], hre_ref[:, ss], him_ref[:, ss], ar_ref[j], ai_ref[j],
                             bbd_ref[j], cbd_ref[j], d_ref[j], bu_ref.at[par], hs_ref.at[par], tc, nb)
        hre_ref[:, ss] = hr
        him_ref[:, ss] = hi
        yt_ref[par] = y
        for b in range(nb):
            y_ref[b * tc:(b + 1) * tc, j * LANES:(j + 1) * LANES] = yt_ref[par, pl.ds(b, tc, stride=nb), :]
    y = _gelu(y_ref[...])
    branch = y * _sigmoid(_dot(y.astype(BF16), wglu_ref[...]) + bglu_ref[...])
    gate = _silu(_dot(h, win_ref[:, BRANCH + XATT:]))
    m_ref[:, 0:BRANCH] = (branch * gate[:, 0:BRANCH]).astype(BF16)
    q = _dot(h, win_ref[:, BRANCH:BRANCH + XATT])
    pieces = [(b, h) for b in range(nb) for h in range(N_XHEADS)]
    att = _attend([_head(q[b * tc:(b + 1) * tc], h) for b, h in pieces],
                  [_head(k_ref[b * N_MEM:(b + 1) * N_MEM, :], h) for b, h in pieces],
                  [_head(v_ref[b * N_MEM:(b + 1) * N_MEM, :], h) for b, h in pieces])
    for (b, h), a in zip(pieces, att):
        rs = slice(b * tc, (b + 1) * tc)
        cs = slice(BRANCH + h * XHEAD_DIM, BRANCH + (h + 1) * XHEAD_DIM)
        m_ref[rs, cs] = (a * gate[rs, cs]).astype(BF16)
    out = _dot(m_ref[...], wout_ref[...])
    o_ref[...] = (x + _rms(out, post_ref[...])).reshape(nb, tc, D_MODEL)


def _layer_b_prompt(x, kb, vb, win, ar, ai, bbd, cbd, dvec, wglu, bglu, wout, pre, post):
    tc = 64
    nb, seq, _ = x.shape
    tm = nb * tc
    xs = pl.BlockSpec((nb, tc, D_MODEL), lambda t: (0, t, 0))
    hs = pl.BlockSpec((nb, SSM_GROUPS * SSM_STATE), lambda t: (0, 0))
    consts = [kb, vb, win, ar, ai, bbd, cbd, dvec, wglu, bglu, wout, pre, post]
    return pl.pallas_call(
        _layer_b_prompt_kernel,
        grid=(seq // tc,),
        in_specs=[xs] + [_const_spec(c.shape) for c in consts],
        out_specs=[xs, hs, hs],
        out_shape=[jax.ShapeDtypeStruct(x.shape, F32),
                   jax.ShapeDtypeStruct((nb, SSM_GROUPS * SSM_STATE), F32),
                   jax.ShapeDtypeStruct((nb, SSM_GROUPS * SSM_STATE), F32)],
        scratch_shapes=[pltpu.VMEM((S5_TILES, tm, LANES), F32),
                        pltpu.VMEM((2, tm, 2 * S5_HALF), F32),
                        pltpu.VMEM((2, tm, 2 * S5_HALF), F32),
                        pltpu.VMEM((2, tm, LANES), F32),
                        pltpu.VMEM((tm, BRANCH), F32),
                        pltpu.VMEM((tm, MIX_WIDTH), BF16)],
        compiler_params=pltpu.CompilerParams(
            dimension_semantics=("arbitrary",), vmem_limit_bytes=VMEM_LIMIT),
        name="layer_b_prompt",
    )(x, *consts)


def _sample_pre_a_kernel(x_ref, win_ref, lng_ref, lnb_ref, wrow_ref, brow_ref, pre_ref,
                         v_ref, mbr_ref, q_ref, gatt_ref):
    nb = DEC_BATCH
    h = _rms(x_ref[...], pre_ref[...]).astype(BF16)
    gate = _silu(_dot(h, win_ref[:, 2 * BRANCH + XATT:]))
    u = _gelu(_dot(h, win_ref[:, 0:BRANCH]))
    v = _layer_norm(_gelu(_dot(h, win_ref[:, BRANCH:2 * BRANCH])), lng_ref[...], lnb_ref[...])
    v_ref[...] = v
    for t in range(DEC_SEQ):
        rs = slice(t * nb, (t + 1) * nb)
        mixed = brow_ref[t:t + 1, :]
        for s in range(t + 1):
            mixed = mixed + wrow_ref[t * DEC_SEQ + s:t * DEC_SEQ + s + 1, :] * v[s * nb:(s + 1) * nb]
        mbr_ref[rs, :] = u[rs] * mixed * gate[rs, 0:BRANCH]
    q_ref[...] = _dot(h, win_ref[:, 2 * BRANCH:2 * BRANCH + XATT])
    gatt_ref[...] = gate[:, BRANCH:]


def _sample_pre_a(x, win, lng, lnb, wrow, brow, pre):
    rows = x.shape[0]
    ins = [x, win, lng, lnb, wrow, brow, pre]
    shapes = [(rows, BRANCH), (rows, BRANCH), (rows, XATT), (rows, XATT)]
    return pl.pallas_call(
        _sample_pre_a_kernel,
        grid=(1,),
        in_specs=[_const_spec(a.shape) for a in ins],
        out_specs=[pl.BlockSpec(s, lambda i: (0, 0)) for s in shapes],
        out_shape=[jax.ShapeDtypeStruct(s, F32) for s in shapes],
        compiler_params=pltpu.CompilerParams(
            dimension_semantics=("arbitrary",), vmem_limit_bytes=VMEM_LIMIT),
        name="sample_pre_a",
    )(*ins)


def _sample_pre_b_kernel(x_ref, win_ref, sre_ref, sim_ref, ar_ref, ai_ref, bbd_ref, cbd_ref, d_ref,
                         wglu_ref, bglu_ref, pre_ref,
                         mbr_ref, q_ref, gatt_ref, hre_ref, him_ref,
                         bu_ref, hs_ref, y_ref):
    nb = DEC_BATCH
    h = _rms(x_ref[...], pre_ref[...]).astype(BF16)
    u = _dot(h, win_ref[:, 0:BRANCH])
    for j in range(S5_TILES):
        ss = slice(j * S5_HALF, (j + 1) * S5_HALF)
        ls = slice(j * LANES, (j + 1) * LANES)
        y, hr, hi = _s5_tile(u[:, ls], sre_ref[:, ss], sim_ref[:, ss], ar_ref[j], ai_ref[j],
                             bbd_ref[j], cbd_ref[j], d_ref[j], bu_ref, hs_ref, DEC_SEQ, nb)
        hre_ref[:, ss] = hr
        him_ref[:, ss] = hi
        y_ref[:, ls] = y
    y = _gelu(y_ref[...])
    branch = y * _sigmoid(_dot(y.astype(BF16), wglu_ref[...]) + bglu_ref[...])
    gate = _silu(_dot(h, win_ref[:, BRANCH + XATT:]))
    mbr_ref[...] = branch * gate[:, 0:BRANCH]
    q_ref[...] = _dot(h, win_ref[:, BRANCH:BRANCH + XATT])
    gatt_ref[...] = gate[:, BRANCH:]


def _sample_pre_b(x, win, sre, sim, ar, ai, bbd, cbd, dvec, wglu, bglu, pre):
    rows = x.shape[0]
    ins = [x, win, sre, sim, ar, ai, bbd, cbd, dvec, wglu, bglu, pre]
    shapes = [(rows, BRANCH), (rows, XATT), (rows, XATT), sre.shape, sim.shape]
    return pl.pallas_call(
        _sample_pre_b_kernel,
        grid=(1,),
        in_specs=[_const_spec(a.shape) for a in ins],
        out_specs=[pl.BlockSpec(s, lambda i: (0, 0)) for s in shapes],
        out_shape=[jax.ShapeDtypeStruct(s, F32) for s in shapes],
        scratch_shapes=[pltpu.VMEM((rows, 2 * S5_HALF), F32),
                        pltpu.VMEM((rows, 2 * S5_HALF), F32),
                        pltpu.VMEM((rows, BRANCH), F32)],
        compiler_params=pltpu.CompilerParams(
            dimension_semantics=("arbitrary",), vmem_limit_bytes=VMEM_LIMIT),
        name="sample_pre_b",
    )(*ins)


def _sample_attn_kernel(q_ref, g_ref, k_ref, v_ref, o_ref):
    bb, rows, _ = q_ref.shape
    cols = k_ref.shape[1]
    row_h = (lax.broadcasted_iota(jnp.int32, (bb * rows, cols), 0) // DEC_SEQ) % N_XHEADS
    col_h = lax.broadcasted_iota(jnp.int32, (bb * rows, cols), 1) % N_XHEADS
    s = jnp.concatenate([_dot_nt(q_ref[j].astype(BF16), k_ref[j].astype(BF16)) for j in range(bb)], axis=0)
    s = jnp.where(row_h == col_h, s * (XHEAD_DIM ** -0.5), -1e30)
    p = jnp.exp(s - jnp.max(s, axis=-1, keepdims=True))
    inv = 1.0 / jnp.sum(p, axis=-1, keepdims=True)
    pb = p.astype(BF16)
    for j in range(bb):
        rs = slice(j * rows, (j + 1) * rows)
        o_ref[j] = _dot(pb[rs], v_ref[j].astype(BF16)) * inv[rs] * g_ref[j]


def _sample_attn(q, gatt, cache_k, cache_v, layer):
    bb = 8
    qs = pl.BlockSpec((bb,) + q.shape[1:], lambda i: (i, 0, 0))
    cs = pl.BlockSpec((None, bb) + cache_k.shape[2:], lambda i: (layer, i, 0, 0))
    return pl.pallas_call(
        _sample_attn_kernel,
        grid=(q.shape[0] // bb,),
        in_specs=[qs, qs, cs, cs],
        out_specs=qs,
        out_shape=jax.ShapeDtypeStruct(q.shape, F32),
        compiler_params=pltpu.CompilerParams(
            dimension_semantics=("arbitrary",), vmem_limit_bytes=VMEM_LIMIT),
        name="sample_attn",
    )(q, gatt, cache_k, cache_v)


def _sample_post_kernel(x_ref, mbr_ref, matt_ref, wout_ref, post_ref, o_ref):
    out = (_dot(mbr_ref[...].astype(BF16), wout_ref[0:BRANCH, :])
           + _dot(matt_ref[...].astype(BF16), wout_ref[BRANCH:, :]))
    o_ref[...] = x_ref[...] + _rms(out, post_ref[...])


def _sample_post(x, mbr, matt, wout, post):
    ins = [x, mbr, matt, wout, post]
    return pl.pallas_call(
        _sample_post_kernel,
        grid=(1,),
        in_specs=[_const_spec(a.shape) for a in ins],
        out_specs=pl.BlockSpec(x.shape, lambda i: (0, 0)),
        out_shape=jax.ShapeDtypeStruct(x.shape, F32),
        compiler_params=pltpu.CompilerParams(
            dimension_semantics=("arbitrary",), vmem_limit_bytes=VMEM_LIMIT),
        name="sample_post",
    )(*ins)


def _to_bt(a):
    return a.reshape(DEC_SEQ, DEC_BATCH, a.shape[-1]).transpose(1, 0, 2)


def _to_tb(a):
    return a.transpose(1, 0, 2).reshape(DEC_SEQ * DEC_BATCH, a.shape[-1])


def _to_bht(a):
    a = a.reshape(DEC_SEQ, DEC_BATCH, N_XHEADS, XHEAD_DIM).transpose(1, 2, 0, 3)
    return a.reshape(DEC_BATCH, N_XHEADS * DEC_SEQ, XHEAD_DIM)


def _from_bht(a):
    a = a.reshape(DEC_BATCH, N_XHEADS, DEC_SEQ, XHEAD_DIM).transpose(2, 0, 1, 3)
    return a.reshape(DEC_SEQ * DEC_BATCH, XATT)


def kernel(x_prompt, x_sample, cache_mem_k, cache_mem_v, state_ssm_re, state_ssm_im, mem_prompt,
           w_in_a, ln_v_g, ln_v_b, w_spatial, b_spatial,
           w_in_b, ssm_lambda_re, ssm_lambda_im, ssm_log_dt, ssm_b_re, ssm_b_im, ssm_c_re, ssm_c_im,
           ssm_d, w_glu, b_glu,
           mem_norm_g, w_mem_k, w_mem_v, w_out, pre_norm_g, post_norm_g):
    depth = w_out.shape[0]
    win_a = w_in_a[0].astype(BF16)
    win_b = w_in_b[0].astype(BF16)
    wglu = w_glu[0].astype(BF16)
    wout = w_out.astype(BF16)
    wk = w_mem_k.astype(BF16)
    wv = w_mem_v.astype(BF16)
    pre = pre_norm_g.reshape(depth, 1, D_MODEL)
    post = post_norm_g.reshape(depth, 1, D_MODEL)
    lng = ln_v_g[0].reshape(1, BRANCH)
    lnb = ln_v_b[0].reshape(1, BRANCH)
    bglu = b_glu[0].reshape(1, BRANCH)
    dvec = ssm_d[0].reshape(S5_TILES, 1, LANES)
    bias = jnp.repeat(b_spatial[0].T, A_GDIM, axis=1)
    ws4 = w_spatial[0][:, :DEC_SEQ, :DEC_SEQ]
    wrow = jnp.repeat(ws4.transpose(1, 2, 0).reshape(DEC_SEQ * DEC_SEQ, A_GROUPS), A_GDIM, axis=1)
    brow = bias[:DEC_SEQ]

    ar, ai, bbd, cbd = _s5_prep(ssm_lambda_re[0], ssm_lambda_im[0], ssm_log_dt[0],
                                ssm_b_re[0], ssm_b_im[0], ssm_c_re[0], ssm_c_im[0])

    mem = mem_prompt.reshape(BATCH * N_MEM, D_MODEL)
    mk, mv, mkb, mvb = _mem_kv(mem, mem_norm_g.reshape(depth, 1, D_MODEL), wk, wv)

    yp = _layer_a_prompt(x_prompt, mkb[0].reshape(BATCH, N_MEM, XATT), mvb[0].reshape(BATCH, N_MEM, XATT),
                         win_a, lng, lnb, w_spatial[0], bias, wout[0], pre[0], post[0])
    yp, hp_re, hp_im = _layer_b_prompt(yp, mkb[1], mvb[1], win_b, ar, ai, bbd, cbd, dvec,
                                       wglu, bglu, wout[1], pre[1], post[1])

    ck = cache_mem_k.reshape(depth, DEC_BATCH, N_MEM * N_XHEADS, XHEAD_DIM)
    cv = cache_mem_v.reshape(depth, DEC_BATCH, N_MEM * N_XHEADS, XHEAD_DIM)
    xs = _to_tb(x_sample)
    v_rows, mbr, q, gatt = _sample_pre_a(xs, win_a, lng, lnb, wrow, brow, pre[0])
    matt = _sample_attn(_to_bht(q), _to_bht(gatt), ck, cv, 0)
    xs = _sample_post(xs, mbr, _from_bht(matt), wout[0], post[0])
    sre = state_ssm_re[0].reshape(DEC_BATCH, SSM_GROUPS * SSM_STATE)
    sim = state_ssm_im[0].reshape(DEC_BATCH, SSM_GROUPS * SSM_STATE)
    mbr, q, gatt, hs_re, hs_im = _sample_pre_b(xs, win_b, sre, sim, ar, ai, bbd, cbd, dvec,
                                               wglu, bglu, pre[1])
    matt = _sample_attn(_to_bht(q), _to_bht(gatt), ck, cv, 1)
    xs = _sample_post(xs, mbr, _from_bht(matt), wout[1], post[1])

    kv_shape = (depth, BATCH, N_MEM, N_XHEADS, XHEAD_DIM)
    st_p = (1, BATCH, SSM_GROUPS, SSM_STATE)
    st_s = (1, DEC_BATCH, SSM_GROUPS, SSM_STATE)
    return (yp, _to_bt(xs), mk.reshape(kv_shape), mv.reshape(kv_shape),
            hp_re.reshape(st_p), hp_im.reshape(st_p), hs_re.reshape(st_s), hs_im.reshape(st_s),
            _to_bt(v_rows)[None])
```

```python
import functools
import math

import jax
import jax.numpy as jnp
from jax import lax
from jax.experimental import pallas as pl
from jax.experimental.pallas import tpu as pltpu

D_MODEL = 1024
BATCH = 8
SEQ = 2048
DEC_BATCH = 128
DEC_SEQ = 4
BRANCH = 1536
N_XHEADS = 4
XHEAD_DIM = 128
XATT = N_XHEADS * XHEAD_DIM
MIX_WIDTH = BRANCH + XATT
N_MEM = 256
CHUNK = 128
A_GROUPS = 8
A_GDIM = BRANCH // A_GROUPS
SSM_GCH = 16
SSM_GROUPS = 96
SSM_STATE = 64
EPS = 1e-6

LANES = 128
S5_TILE_GROUPS = LANES // SSM_GCH
S5_TILES = SSM_GROUPS // S5_TILE_GROUPS
S5_HALF = S5_TILE_GROUPS * SSM_STATE
VMEM_LIMIT = 60 * 1024 * 1024

F32 = jnp.float32
BF16 = jnp.bfloat16


def _dot(a, b):
    return jnp.dot(a, b, preferred_element_type=F32)


def _rms(x, g):
    return x * lax.rsqrt(jnp.mean(x * x, axis=-1, keepdims=True) + EPS) * g


def _layer_norm(x, g, b):
    mu = jnp.mean(x, axis=-1, keepdims=True)
    xc = x - mu
    var = jnp.mean(xc * xc, axis=-1, keepdims=True)
    return xc * lax.rsqrt(var + EPS) * g + b


def _gelu(x):
    c = math.sqrt(2.0 / math.pi)
    return x * (0.5 * (1.0 + jnp.tanh(c * (x + 0.044715 * (x * x * x)))))


def _sigmoid(x):
    return 1.0 / (1.0 + jnp.exp(-x))


def _silu(x):
    return x * _sigmoid(x)


def _dot_nt(a, b):
    return lax.dot_general(a, b, (((1,), (1,)), ((), ())), preferred_element_type=F32)


def _attend(qs, ks, vs):
    r = qs[0].shape[0]
    s = jnp.concatenate([_dot_nt(q.astype(BF16), k) for q, k in zip(qs, ks)], axis=0)
    s = s * (XHEAD_DIM ** -0.5)
    p = jnp.exp(s - jnp.max(s, axis=-1, keepdims=True))
    inv = 1.0 / jnp.sum(p, axis=-1, keepdims=True)
    pb = p.astype(BF16)
    return [_dot(pb[i * r:(i + 1) * r], v) * inv[i * r:(i + 1) * r] for i, v in enumerate(vs)]


def _head(a, h):
    return a[:, h * XHEAD_DIM:(h + 1) * XHEAD_DIM]


def _const_spec(shape):
    nd = len(shape)
    return pl.BlockSpec(shape, lambda *_: (0,) * nd, pipeline_mode=pl.Buffered(1))


def _layer_spec(shape, layer):
    nd = len(shape)
    return pl.BlockSpec((None,) + tuple(shape[1:]), lambda *_: (layer,) + (0,) * (nd - 1),
                        pipeline_mode=pl.Buffered(1))


def _lam_bar(lr, li, ldt):
    dt = jnp.exp(ldt)
    e = jnp.exp(lr * dt)
    return e * jnp.cos(li * dt), e * jnp.sin(li * dt)


def _s5_prep_kernel(lr_ref, li_ref, ldt_ref, lrc_ref, lic_ref, ldtc_ref, br_ref, bi_ref, cr_ref, ci_ref,
                    a2r_ref, a2i_ref, b2_ref, c2_ref, k2_ref):
    lr = lr_ref[0]
    li = li_ref[0]
    ar, ai = _lam_bar(lr, li, ldt_ref[0])
    a2r_ref[0] = ar * ar - ai * ai
    a2i_ref[0] = 2.0 * ar * ai
    nr = ar - 1.0
    den = lr * lr + li * li
    kr = (nr * lr + ai * li) / den
    ki = (ai * lr - nr * li) / den
    br = br_ref[0]
    bi = bi_ref[0]
    row_g = lax.broadcasted_iota(jnp.int32, br.shape, 0) // SSM_GCH
    col_g = lax.broadcasted_iota(jnp.int32, br.shape, 1) // SSM_STATE
    own = row_g == col_g
    bbr = jnp.where(own, kr * br - ki * bi, 0.0)
    bbi = jnp.where(own, kr * bi + ki * br, 0.0)
    b_tok1 = jnp.concatenate([bbr, bbi], axis=1).astype(BF16)
    b_tok0 = jnp.concatenate([ar * bbr - ai * bbi, ar * bbi + ai * bbr], axis=1).astype(BF16)
    b2_ref[0, 0:LANES, :] = b_tok0
    b2_ref[0, LANES:, :] = b_tok1

    arc, aic = _lam_bar(lrc_ref[0], lic_ref[0], ldtc_ref[0])
    a2rc = arc * arc - aic * aic
    a2ic = 2.0 * arc * aic
    cr = cr_ref[0]
    ci = ci_ref[0]
    row_g = lax.broadcasted_iota(jnp.int32, cr.shape, 0) // SSM_STATE
    col_g = lax.broadcasted_iota(jnp.int32, cr.shape, 1) // SSM_GCH
    own = row_g == col_g
    cr = jnp.where(own, cr, 0.0)
    ci = jnp.where(own, ci, 0.0)
    c2_ref[0, 0:S5_HALF, 0:LANES] = (cr * arc - ci * aic).astype(BF16)
    c2_ref[0, S5_HALF:, 0:LANES] = (-(cr * aic + ci * arc)).astype(BF16)
    c2_ref[0, 0:S5_HALF, LANES:] = (cr * a2rc - ci * a2ic).astype(BF16)
    c2_ref[0, S5_HALF:, LANES:] = (-(cr * a2ic + ci * a2rc)).astype(BF16)
    c_plain = jnp.concatenate([cr, -ci], axis=0).astype(BF16)
    k0 = _dot(b_tok1, c_plain).astype(BF16)
    k1 = _dot(b_tok0, c_plain).astype(BF16)
    k2_ref[0, 0:LANES, 0:LANES] = k0
    k2_ref[0, 0:LANES, LANES:] = k1
    k2_ref[0, LANES:, 0:LANES] = jnp.zeros_like(k0)
    k2_ref[0, LANES:, LANES:] = k0


def _s5_prep(lam_re, lam_im, log_dt, b_re, b_im, c_re, c_im):
    nt, tg = S5_TILES, S5_TILE_GROUPS
    lr = lam_re.reshape(nt, 1, S5_HALF)
    li = lam_im.reshape(nt, 1, S5_HALF)
    ldt = jnp.broadcast_to(log_dt[:, None], (SSM_GROUPS, SSM_STATE)).reshape(nt, 1, S5_HALF)

    def expand_b(b):
        t = b.reshape(nt, tg, SSM_STATE, SSM_GCH).transpose(0, 3, 1, 2)
        t = jnp.broadcast_to(t[:, None], (nt, tg, SSM_GCH, tg, SSM_STATE))
        return t.reshape(nt, LANES, S5_HALF)

    def expand_c(c):
        t = c.reshape(nt, tg, SSM_GCH, SSM_STATE).transpose(0, 1, 3, 2)
        t = jnp.broadcast_to(t[:, :, :, None], (nt, tg, SSM_STATE, tg, SSM_GCH))
        return t.reshape(nt, S5_HALF, LANES)

    def spec(*shape):
        return pl.BlockSpec((1,) + shape, lambda j: (j, 0, 0))

    vec, col = spec(1, S5_HALF), spec(S5_HALF, 1)
    bsp, csp = spec(LANES, S5_HALF), spec(S5_HALF, LANES)
    out_shapes = [(1, S5_HALF), (1, S5_HALF), (2 * LANES, 2 * S5_HALF), (2 * S5_HALF, 2 * LANES),
                  (2 * LANES, 2 * LANES)]
    out_dtypes = [F32, F32, BF16, BF16, BF16]
    return pl.pallas_call(
        _s5_prep_kernel,
        grid=(nt,),
        in_specs=[vec, vec, vec, col, col, col, bsp, bsp, csp, csp],
        out_specs=[spec(*s) for s in out_shapes],
        out_shape=[jax.ShapeDtypeStruct((nt,) + s, d) for s, d in zip(out_shapes, out_dtypes)],
        name="s5_prep",
    )(lr, li, ldt, lr.reshape(nt, S5_HALF, 1), li.reshape(nt, S5_HALF, 1), ldt.reshape(nt, S5_HALF, 1),
      expand_b(b_re), expand_b(b_im), expand_c(c_re), expand_c(c_im))


def _mem_kv_kernel(mem_ref, g_ref, wk_ref, wv_ref, k_ref, v_ref, kb_ref, vb_ref):
    m = _rms(mem_ref[...], g_ref[...]).astype(BF16)
    k = _dot(m, wk_ref[...])
    v = _dot(m, wv_ref[...])
    kb_ref[...] = k.astype(BF16)
    vb_ref[...] = v.astype(BF16)
    for b in range(k_ref.shape[0]):
        for h in range(N_XHEADS):
            rows = pl.ds(h, N_MEM, stride=N_XHEADS)
            k_ref[b, rows, :] = _head(k[b * N_MEM:(b + 1) * N_MEM], h)
            v_ref[b, rows, :] = _head(v[b * N_MEM:(b + 1) * N_MEM], h)


def _mem_kv(mem, g, wk, wv):
    rows = mem.shape[0]
    depth = g.shape[0]
    nb = 2
    tm = nb * N_MEM
    out = pl.BlockSpec((None, nb, N_MEM * N_XHEADS, XHEAD_DIM), lambda i, r: (i, r, 0, 0))
    outb = pl.BlockSpec((None, tm, XATT), lambda i, r: (i, r, 0))
    wsp = pl.BlockSpec((None, D_MODEL, XATT), lambda i, r: (i, 0, 0))
    return pl.pallas_call(
        _mem_kv_kernel,
        grid=(depth, rows // tm),
        in_specs=[pl.BlockSpec((tm, D_MODEL), lambda i, r: (r, 0)),
                  pl.BlockSpec((None, 1, D_MODEL), lambda i, r: (i, 0, 0)),
                  wsp, wsp],
        out_specs=[out, out, outb, outb],
        out_shape=[jax.ShapeDtypeStruct((depth, rows // N_MEM, N_MEM * N_XHEADS, XHEAD_DIM), F32)] * 2
        + [jax.ShapeDtypeStruct((depth, rows, XATT), BF16)] * 2,
        name="mem_kv",
    )(mem, g, wk, wv)


def _masked_spatial(ws_ref):
    row = lax.broadcasted_iota(jnp.int32, (CHUNK, CHUNK), 0)
    col = lax.broadcasted_iota(jnp.int32, (CHUNK, CHUNK), 1)
    return [jnp.where(row >= col, ws_ref[g], 0.0).astype(BF16) for g in range(A_GROUPS)]


def _gate_chunk(wm, vc, pair):
    base = pair * 2 * A_GDIM
    even = _dot(wm[2 * pair], vc[:, base:base + 2 * LANES])
    odd = _dot(wm[2 * pair + 1], vc[:, base + LANES:base + 3 * LANES])
    lane = lax.broadcasted_iota(jnp.int32, (CHUNK, LANES), 1)
    mid = jnp.where(lane < A_GDIM - LANES, even[:, LANES:], odd[:, :LANES])
    return jnp.concatenate([even[:, :LANES], mid, odd[:, LANES:]], axis=1)


def _s5_tile(u, hr, hi, a2r, a2i, b2, c2, k2, bu_ref, hs_ref, steps, nb):
    pairs = steps // 2
    u3 = u.reshape(pairs, 2 * nb, LANES)
    u2 = jnp.concatenate([u3[:, 0:nb, :].reshape(pairs * nb, LANES),
                          u3[:, nb:, :].reshape(pairs * nb, LANES)], axis=1).astype(BF16)
    bu_ref[...] = _dot(u2, b2)
    a2r = jnp.broadcast_to(a2r, (nb, S5_HALF))
    a2i = jnp.broadcast_to(a2i, (nb, S5_HALF))
    for k in range(pairs):
        rows = slice(k * nb, (k + 1) * nb)
        hs_ref[rows, 0:S5_HALF] = hr
        hs_ref[rows, S5_HALF:] = hi
        hr, hi = (a2r * hr - a2i * hi + bu_ref[rows, 0:S5_HALF],
                  a2r * hi + a2i * hr + bu_ref[rows, S5_HALF:])
    y2 = _dot(hs_ref[...].astype(BF16), c2) + _dot(u2, k2)
    return y2, hr, hi


def _layer_a_prompt_kernel(x_ref, k_ref, v_ref, win_ref, lng_ref, lnb_ref, ws_ref, bias_ref,
                           wout_ref, pre_ref, post_ref, o_ref, m_ref):
    tm = x_ref.shape[0]
    x = x_ref[...]
    h = _rms(x, pre_ref[...]).astype(BF16)
    gate = _silu(_dot(h, win_ref[:, 2 * BRANCH + XATT:]))
    u = _gelu(_dot(h, win_ref[:, 0:BRANCH]))
    v = _layer_norm(_gelu(_dot(h, win_ref[:, BRANCH:2 * BRANCH])), lng_ref[...], lnb_ref[...])
    vb = v.astype(BF16)
    wm = _masked_spatial(ws_ref)
    for c in range(tm // CHUNK):
        rs = slice(c * CHUNK, (c + 1) * CHUNK)
        vc = vb[rs]
        for pair in range(A_GROUPS // 2):
            cs = slice(pair * 2 * A_GDIM, (pair + 1) * 2 * A_GDIM)
            mixed = _gate_chunk(wm, vc, pair) + bias_ref[:, cs]
            m_ref[rs, cs] = (u[rs, cs] * mixed * gate[rs, cs]).astype(BF16)
    q = _dot(h, win_ref[:, 2 * BRANCH:2 * BRANCH + XATT])
    heads = range(N_XHEADS)
    k = k_ref[...]
    v = v_ref[...]
    att = _attend([_head(q, h) for h in heads], [_head(k, h) for h in heads], [_head(v, h) for h in heads])
    for h in heads:
        cs = slice(BRANCH + h * XHEAD_DIM, BRANCH + (h + 1) * XHEAD_DIM)
        m_ref[:, cs] = (att[h] * gate[:, cs]).astype(BF16)
    out = _dot(m_ref[...], wout_ref[...])
    o_ref[...] = x + _rms(out, post_ref[...])


def _layer_a_prompt(x, kb, vb, win, lng, lnb, ws, bias, wout, pre, post, layer):
    tm = 256
    nb, seq, _ = x.shape
    xs = pl.BlockSpec((None, tm, D_MODEL), lambda b, t: (b, t, 0))
    kv = pl.BlockSpec((None, N_MEM, XATT), lambda b, t: (layer, b, 0))
    return pl.pallas_call(
        _layer_a_prompt_kernel,
        grid=(nb, seq // tm),
        in_specs=[xs, kv, kv, _const_spec(win.shape), _const_spec(lng.shape), _const_spec(lnb.shape),
                  _const_spec(ws.shape), _const_spec(bias.shape), _layer_spec(wout.shape, layer),
                  _layer_spec(pre.shape, layer), _layer_spec(post.shape, layer)],
        out_specs=xs,
        out_shape=jax.ShapeDtypeStruct(x.shape, F32),
        scratch_shapes=[pltpu.VMEM((tm, MIX_WIDTH), BF16)],
        compiler_params=pltpu.CompilerParams(
            dimension_semantics=("arbitrary", "arbitrary"), vmem_limit_bytes=VMEM_LIMIT),
        name="layer_a_prompt",
    )(x, kb, vb, win, lng, lnb, ws, bias, wout, pre, post)


def _layer_b_prompt_kernel(x_ref, k_ref, v_ref, win_ref, a2r_ref, a2i_ref, b2_ref, c2_ref, k2_ref, d_ref,
                           wglu_ref, bglu_ref, wout_ref, pre_ref, post_ref,
                           o_ref, hre_ref, him_ref,
                           ut_ref, bu_ref, hs_ref, yt_ref, y_ref, m_ref):
    nb, tc, _ = x_ref.shape
    tm = nb * tc
    half = tc // 2

    @pl.when(pl.program_id(0) == 0)
    def _():
        hre_ref[...] = jnp.zeros_like(hre_ref)
        him_ref[...] = jnp.zeros_like(him_ref)

    x = x_ref[...].reshape(tm, D_MODEL)
    h = _rms(x, pre_ref[...]).astype(BF16)
    u = _dot(h, win_ref[:, 0:BRANCH])
    for j in range(S5_TILES):
        for b in range(nb):
            ut_ref[j, pl.ds(b, tc, stride=nb), :] = u[b * tc:(b + 1) * tc, j * LANES:(j + 1) * LANES]
    for j in range(S5_TILES):
        ss = slice(j * S5_HALF, (j + 1) * S5_HALF)
        par = j % 2
        y2, hr, hi = _s5_tile(ut_ref[j], hre_ref[:, ss], him_ref[:, ss], a2r_ref[j], a2i_ref[j],
                              b2_ref[j], c2_ref[j], k2_ref[j], bu_ref.at[par], hs_ref.at[par], tc, nb)
        hre_ref[:, ss] = hr
        him_ref[:, ss] = hi
        for tp in range(2):
            yt_ref[par, tp] = y2[:, tp * LANES:(tp + 1) * LANES]
            for b in range(nb):
                y_ref[j, pl.ds(b * tc + tp, half, stride=2), :] = yt_ref[par, tp, pl.ds(b, half, stride=nb), :]
    y = _gelu(jnp.concatenate([y_ref[j] for j in range(S5_TILES)], axis=1) + d_ref[...] * u)
    branch = y * _sigmoid(_dot(y.astype(BF16), wglu_ref[...]) + bglu_ref[...])
    gate = _silu(_dot(h, win_ref[:, BRANCH + XATT:]))
    m_ref[:, 0:BRANCH] = (branch * gate[:, 0:BRANCH]).astype(BF16)
    q = _dot(h, win_ref[:, BRANCH:BRANCH + XATT])
    pieces = [(b, h) for b in range(nb) for h in range(N_XHEADS)]
    att = _attend([_head(q[b * tc:(b + 1) * tc], h) for b, h in pieces],
                  [_head(k_ref[b * N_MEM:(b + 1) * N_MEM, :], h) for b, h in pieces],
                  [_head(v_ref[b * N_MEM:(b + 1) * N_MEM, :], h) for b, h in pieces])
    for (b, h), a in zip(pieces, att):
        rs = slice(b * tc, (b + 1) * tc)
        cs = slice(BRANCH + h * XHEAD_DIM, BRANCH + (h + 1) * XHEAD_DIM)
        m_ref[rs, cs] = (a * gate[rs, cs]).astype(BF16)
    out = _dot(m_ref[...], wout_ref[...])
    o_ref[...] = (x + _rms(out, post_ref[...])).reshape(nb, tc, D_MODEL)


def _layer_b_prompt(x, kb, vb, win, s5, dvec, wglu, bglu, wout, pre, post, layer):
    tc = 64
    nb, seq, _ = x.shape
    tm = nb * tc
    xs = pl.BlockSpec((nb, tc, D_MODEL), lambda t: (0, t, 0))
    hs = pl.BlockSpec((nb, SSM_GROUPS * SSM_STATE), lambda t: (0, 0))
    ins = [kb, vb, win, *s5, dvec, wglu, bglu, wout, pre, post]
    stacked = [True, True] + [False] * (len(s5) + 4) + [True, True, True]
    return pl.pallas_call(
        _layer_b_prompt_kernel,
        grid=(seq // tc,),
        in_specs=[xs] + [_layer_spec(a.shape, layer) if st else _const_spec(a.shape)
                         for a, st in zip(ins, stacked)],
        out_specs=[xs, hs, hs],
        out_shape=[jax.ShapeDtypeStruct(x.shape, F32),
                   jax.ShapeDtypeStruct((nb, SSM_GROUPS * SSM_STATE), F32),
                   jax.ShapeDtypeStruct((nb, SSM_GROUPS * SSM_STATE), F32)],
        scratch_shapes=[pltpu.VMEM((S5_TILES, tm, LANES), F32),
                        pltpu.VMEM((2, tm // 2, 2 * S5_HALF), F32),
                        pltpu.VMEM((2, tm // 2, 2 * S5_HALF), F32),
                        pltpu.VMEM((2, 2, tm // 2, LANES), F32),
                        pltpu.VMEM((S5_TILES, tm, LANES), F32),
                        pltpu.VMEM((tm, MIX_WIDTH), BF16)],
        compiler_params=pltpu.CompilerParams(
            dimension_semantics=("arbitrary",), vmem_limit_bytes=VMEM_LIMIT),
        name="layer_b_prompt",
    )(x, *ins)


def _sample_pre_a_kernel(x_ref, win_ref, lng_ref, lnb_ref, wrow_ref, brow_ref, pre_ref,
                         v_ref, mbr_ref, q_ref, gatt_ref):
    nb = DEC_BATCH
    h = _rms(x_ref[...], pre_ref[...]).astype(BF16)
    gate = _silu(_dot(h, win_ref[:, 2 * BRANCH + XATT:]))
    u = _gelu(_dot(h, win_ref[:, 0:BRANCH]))
    v = _layer_norm(_gelu(_dot(h, win_ref[:, BRANCH:2 * BRANCH])), lng_ref[...], lnb_ref[...])
    v_ref[...] = v
    for t in range(DEC_SEQ):
        rs = slice(t * nb, (t + 1) * nb)
        mixed = brow_ref[t:t + 1, :]
        for s in range(t + 1):
            mixed = mixed + wrow_ref[t * DEC_SEQ + s:t * DEC_SEQ + s + 1, :] * v[s * nb:(s + 1) * nb]
        mbr_ref[rs, :] = u[rs] * mixed * gate[rs, 0:BRANCH]
    q_ref[...] = _dot(h, win_ref[:, 2 * BRANCH:2 * BRANCH + XATT])
    gatt_ref[...] = gate[:, BRANCH:]


def _sample_pre_a(x, win, lng, lnb, wrow, brow, pre, layer):
    rows = x.shape[0]
    ins = [x, win, lng, lnb, wrow, brow, pre]
    shapes = [(rows, BRANCH), (rows, BRANCH), (rows, XATT), (rows, XATT)]
    return pl.pallas_call(
        _sample_pre_a_kernel,
        grid=(1,),
        in_specs=[_const_spec(a.shape) for a in ins[:-1]] + [_layer_spec(pre.shape, layer)],
        out_specs=[pl.BlockSpec(s, lambda i: (0, 0)) for s in shapes],
        out_shape=[jax.ShapeDtypeStruct(s, F32) for s in shapes],
        compiler_params=pltpu.CompilerParams(
            dimension_semantics=("arbitrary",), vmem_limit_bytes=VMEM_LIMIT),
        name="sample_pre_a",
    )(*ins)


def _sample_pre_b_kernel(x_ref, win_ref, sre_ref, sim_ref, a2r_ref, a2i_ref, b2_ref, c2_ref, k2_ref, d_ref,
                         wglu_ref, bglu_ref, pre_ref,
                         mbr_ref, q_ref, gatt_ref, hre_ref, him_ref,
                         bu_ref, hs_ref, y_ref):
    nb = DEC_BATCH
    h = _rms(x_ref[...], pre_ref[...]).astype(BF16)
    u = _dot(h, win_ref[:, 0:BRANCH])
    for j in range(S5_TILES):
        ss = slice(j * S5_HALF, (j + 1) * S5_HALF)
        ls = slice(j * LANES, (j + 1) * LANES)
        y2, hr, hi = _s5_tile(u[:, ls], sre_ref[:, ss], sim_ref[:, ss], a2r_ref[j], a2i_ref[j],
                              b2_ref[j], c2_ref[j], k2_ref[j], bu_ref, hs_ref, DEC_SEQ, nb)
        hre_ref[:, ss] = hr
        him_ref[:, ss] = hi
        for pair in range(DEC_SEQ // 2):
            for tp in range(2):
                t = 2 * pair + tp
                y_ref[t * nb:(t + 1) * nb, ls] = y2[pair * nb:(pair + 1) * nb, tp * LANES:(tp + 1) * LANES]
    y = _gelu(y_ref[...] + d_ref[...] * u)
    branch = y * _sigmoid(_dot(y.astype(BF16), wglu_ref[...]) + bglu_ref[...])
    gate = _silu(_dot(h, win_ref[:, BRANCH + XATT:]))
    mbr_ref[...] = branch * gate[:, 0:BRANCH]
    q_ref[...] = _dot(h, win_ref[:, BRANCH:BRANCH + XATT])
    gatt_ref[...] = gate[:, BRANCH:]


def _sample_pre_b(x, win, sre, sim, s5, dvec, wglu, bglu, pre, layer):
    rows = x.shape[0]
    ins = [x, win, sre, sim, *s5, dvec, wglu, bglu]
    shapes = [(rows, BRANCH), (rows, XATT), (rows, XATT), sre.shape, sim.shape]
    return pl.pallas_call(
        _sample_pre_b_kernel,
        grid=(1,),
        in_specs=[_const_spec(a.shape) for a in ins] + [_layer_spec(pre.shape, layer)],
        out_specs=[pl.BlockSpec(s, lambda i: (0, 0)) for s in shapes],
        out_shape=[jax.ShapeDtypeStruct(s, F32) for s in shapes],
        scratch_shapes=[pltpu.VMEM((rows // 2, 2 * S5_HALF), F32),
                        pltpu.VMEM((rows // 2, 2 * S5_HALF), F32),
                        pltpu.VMEM((rows, BRANCH), F32)],
        compiler_params=pltpu.CompilerParams(
            dimension_semantics=("arbitrary",), vmem_limit_bytes=VMEM_LIMIT),
        name="sample_pre_b",
    )(*ins, pre)


def _sample_attn_kernel(q_ref, g_ref, k_ref, v_ref, o_ref):
    bb, rows, _ = q_ref.shape
    cols = k_ref.shape[1]
    row_h = (lax.broadcasted_iota(jnp.int32, (bb * rows, cols), 0) // DEC_SEQ) % N_XHEADS
    col_h = lax.broadcasted_iota(jnp.int32, (bb * rows, cols), 1) % N_XHEADS
    s = jnp.concatenate([_dot_nt(q_ref[j].astype(BF16), k_ref[j].astype(BF16)) for j in range(bb)], axis=0)
    s = jnp.where(row_h == col_h, s * (XHEAD_DIM ** -0.5), -1e30)
    p = jnp.exp(s - jnp.max(s, axis=-1, keepdims=True))
    inv = 1.0 / jnp.sum(p, axis=-1, keepdims=True)
    pb = p.astype(BF16)
    for j in range(bb):
        rs = slice(j * rows, (j + 1) * rows)
        o_ref[j] = _dot(pb[rs], v_ref[j].astype(BF16)) * inv[rs] * g_ref[j]


def _sample_attn(q, gatt, cache_k, cache_v, layer):
    bb = 8
    qs = pl.BlockSpec((bb,) + q.shape[1:], lambda i: (i, 0, 0))
    cs = pl.BlockSpec((None, bb) + cache_k.shape[2:], lambda i: (layer, i, 0, 0))
    return pl.pallas_call(
        _sample_attn_kernel,
        grid=(q.shape[0] // bb,),
        in_specs=[qs, qs, cs, cs],
        out_specs=qs,
        out_shape=jax.ShapeDtypeStruct(q.shape, F32),
        compiler_params=pltpu.CompilerParams(
            dimension_semantics=("arbitrary",), vmem_limit_bytes=VMEM_LIMIT),
        name="sample_attn",
    )(q, gatt, cache_k, cache_v)


def _sample_post_kernel(x_ref, mbr_ref, matt_ref, wout_ref, post_ref, o_ref):
    out = (_dot(mbr_ref[...].astype(BF16), wout_ref[0:BRANCH, :])
           + _dot(matt_ref[...].astype(BF16), wout_ref[BRANCH:, :]))
    o_ref[...] = x_ref[...] + _rms(out, post_ref[...])


def _sample_post(x, mbr, matt, wout, post, layer):
    ins = [x, mbr, matt, wout, post]
    return pl.pallas_call(
        _sample_post_kernel,
        grid=(1,),
        in_specs=[_const_spec(a.shape) for a in ins[:3]]
        + [_layer_spec(wout.shape, layer), _layer_spec(post.shape, layer)],
        out_specs=pl.BlockSpec(x.shape, lambda i: (0, 0)),
        out_shape=jax.ShapeDtypeStruct(x.shape, F32),
        compiler_params=pltpu.CompilerParams(
            dimension_semantics=("arbitrary",), vmem_limit_bytes=VMEM_LIMIT),
        name="sample_post",
    )(*ins)


def _to_bt(a):
    return a.reshape(DEC_SEQ, DEC_BATCH, a.shape[-1]).transpose(1, 0, 2)


def _to_tb(a):
    return a.transpose(1, 0, 2).reshape(DEC_SEQ * DEC_BATCH, a.shape[-1])


def _to_bht(a):
    a = a.reshape(DEC_SEQ, DEC_BATCH, N_XHEADS, XHEAD_DIM).transpose(1, 2, 0, 3)
    return a.reshape(DEC_BATCH, N_XHEADS * DEC_SEQ, XHEAD_DIM)


def _from_bht(a):
    a = a.reshape(DEC_BATCH, N_XHEADS, DEC_SEQ, XHEAD_DIM).transpose(2, 0, 1, 3)
    return a.reshape(DEC_SEQ * DEC_BATCH, XATT)


def kernel(x_prompt, x_sample, cache_mem_k, cache_mem_v, state_ssm_re, state_ssm_im, mem_prompt,
           w_in_a, ln_v_g, ln_v_b, w_spatial, b_spatial,
           w_in_b, ssm_lambda_re, ssm_lambda_im, ssm_log_dt, ssm_b_re, ssm_b_im, ssm_c_re, ssm_c_im,
           ssm_d, w_glu, b_glu,
           mem_norm_g, w_mem_k, w_mem_v, w_out, pre_norm_g, post_norm_g):
    depth = w_out.shape[0]
    win_a = w_in_a[0].astype(BF16)
    win_b = w_in_b[0].astype(BF16)
    wglu = w_glu[0].astype(BF16)
    wout = w_out.astype(BF16)
    wk = w_mem_k.astype(BF16)
    wv = w_mem_v.astype(BF16)
    pre = pre_norm_g.reshape(depth, 1, D_MODEL)
    post = post_norm_g.reshape(depth, 1, D_MODEL)
    lng = ln_v_g[0].reshape(1, BRANCH)
    lnb = ln_v_b[0].reshape(1, BRANCH)
    bglu = b_glu[0].reshape(1, BRANCH)
    dvec = ssm_d[0].reshape(1, BRANCH)
    bias = jnp.repeat(b_spatial[0].T, A_GDIM, axis=1)
    ws4 = w_spatial[0][:, :DEC_SEQ, :DEC_SEQ]
    wrow = jnp.repeat(ws4.transpose(1, 2, 0).reshape(DEC_SEQ * DEC_SEQ, A_GROUPS), A_GDIM, axis=1)
    brow = bias[:DEC_SEQ]

    s5 = _s5_prep(ssm_lambda_re[0], ssm_lambda_im[0], ssm_log_dt[0],
                  ssm_b_re[0], ssm_b_im[0], ssm_c_re[0], ssm_c_im[0])

    mem = mem_prompt.reshape(BATCH * N_MEM, D_MODEL)
    mk, mv, mkb, mvb = _mem_kv(mem, mem_norm_g.reshape(depth, 1, D_MODEL), wk, wv)

    yp = _layer_a_prompt(x_prompt, mkb, mvb, win_a, lng, lnb, w_spatial[0], bias, wout, pre, post, 0)
    yp, hp_re, hp_im = _layer_b_prompt(yp, mkb, mvb, win_b, s5, dvec, wglu, bglu, wout, pre, post, 1)

    ck = cache_mem_k.reshape(depth, DEC_BATCH, N_MEM * N_XHEADS, XHEAD_DIM)
    cv = cache_mem_v.reshape(depth, DEC_BATCH, N_MEM * N_XHEADS, XHEAD_DIM)
    xs = _to_tb(x_sample)
    v_rows, mbr, q, gatt = _sample_pre_a(xs, win_a, lng, lnb, wrow, brow, pre, 0)
    matt = _sample_attn(_to_bht(q), _to_bht(gatt), ck, cv, 0)
    xs = _sample_post(xs, mbr, _from_bht(matt), wout, post, 0)
    sre = state_ssm_re[0].reshape(DEC_BATCH, SSM_GROUPS * SSM_STATE)
    sim = state_ssm_im[0].reshape(DEC_BATCH, SSM_GROUPS * SSM_STATE)
    mbr, q, gatt, hs_re, hs_im = _sample_pre_b(xs, win_b, sre, sim, s5, dvec, wglu, bglu, pre, 1)
    matt = _sample_attn(_to_bht(q), _to_bht(gatt), ck, cv, 1)
    xs = _sample_post(xs, mbr, _from_bht(matt), wout, post, 1)

    kv_shape = (depth, BATCH, N_MEM, N_XHEADS, XHEAD_DIM)
    st_p = (1, BATCH, SSM_GROUPS, SSM_STATE)
    st_s = (1, DEC_BATCH, SSM_GROUPS, SSM_STATE)
    return (yp, _to_bt(xs), mk.reshape(kv_shape), mv.reshape(kv_shape),
            hp_re.reshape(st_p), hp_im.reshape(st_p), hs_re.reshape(st_s), hs_im.reshape(st_s),
            _to_bt(v_rows)[None])
```

```python
import functools
import math

import jax
import jax.numpy as jnp
from jax import lax
from jax.experimental import pallas as pl
from jax.experimental.pallas import tpu as pltpu

D_MODEL = 1024
BATCH = 8
SEQ = 2048
DEC_BATCH = 128
DEC_SEQ = 4
BRANCH = 1536
N_XHEADS = 4
XHEAD_DIM = 128
XATT = N_XHEADS * XHEAD_DIM
MIX_WIDTH = BRANCH + XATT
N_MEM = 256
CHUNK = 128
A_GROUPS = 8
A_GDIM = BRANCH // A_GROUPS
SSM_GCH = 16
SSM_GROUPS = 96
SSM_STATE = 64
EPS = 1e-6

LANES = 128
S5_TILE_GROUPS = LANES // SSM_GCH
S5_TILES = SSM_GROUPS // S5_TILE_GROUPS
S5_HALF = S5_TILE_GROUPS * SSM_STATE
VMEM_LIMIT = 60 * 1024 * 1024

F32 = jnp.float32
BF16 = jnp.bfloat16


def _dot(a, b):
    return jnp.dot(a, b, preferred_element_type=F32)


def _rms(x, g):
    return x * lax.rsqrt(jnp.mean(x * x, axis=-1, keepdims=True) + EPS) * g


def _layer_norm(x, g, b):
    mu = jnp.mean(x, axis=-1, keepdims=True)
    xc = x - mu
    var = jnp.mean(xc * xc, axis=-1, keepdims=True)
    return xc * lax.rsqrt(var + EPS) * g + b


def _gelu(x):
    c = math.sqrt(2.0 / math.pi)
    hx = 0.5 * x
    return hx + hx * jnp.tanh(x * (c + (c * 0.044715) * (x * x)))


def _sigmoid(x):
    return 0.5 + 0.5 * jnp.tanh(0.5 * x)


def _silu(x):
    hx = 0.5 * x
    return hx + hx * jnp.tanh(hx)


def _dot_nt(a, b):
    return lax.dot_general(a, b, (((1,), (1,)), ((), ())), preferred_element_type=F32)


def _attend(qs, ks, vs):
    r = qs[0].shape[0]
    s = jnp.concatenate([_dot_nt(q.astype(BF16), k) for q, k in zip(qs, ks)], axis=0)
    s = s * (XHEAD_DIM ** -0.5)
    p = jnp.exp(s - jnp.max(s, axis=-1, keepdims=True))
    inv = 1.0 / jnp.sum(p, axis=-1, keepdims=True)
    pb = p.astype(BF16)
    return [_dot(pb[i * r:(i + 1) * r], v) * inv[i * r:(i + 1) * r] for i, v in enumerate(vs)]


def _head(a, h):
    return a[:, h * XHEAD_DIM:(h + 1) * XHEAD_DIM]


def _const_spec(shape):
    nd = len(shape)
    return pl.BlockSpec(shape, lambda *_: (0,) * nd, pipeline_mode=pl.Buffered(1))


def _layer_spec(shape, layer):
    nd = len(shape)
    return pl.BlockSpec((None,) + tuple(shape[1:]), lambda *_: (layer,) + (0,) * (nd - 1),
                        pipeline_mode=pl.Buffered(1))


def _lam_bar(lr, li, ldt):
    dt = jnp.exp(ldt)
    e = jnp.exp(lr * dt)
    return e * jnp.cos(li * dt), e * jnp.sin(li * dt)


def _s5_prep_kernel(lr_ref, li_ref, ldt_ref, br_ref, bi_ref, cr_ref, ci_ref,
                    a2r_ref, a2i_ref, b2_ref, c2_ref, k2_ref):
    lr = lr_ref[0]
    li = li_ref[0]
    ar, ai = _lam_bar(lr, li, ldt_ref[0])
    a2r = ar * ar - ai * ai
    a2i = 2.0 * ar * ai
    a2r_ref[0] = a2r
    a2i_ref[0] = a2i
    nr = ar - 1.0
    den = lr * lr + li * li
    kr = (nr * lr + ai * li) / den
    ki = (ai * lr - nr * li) / den
    shape = (LANES, S5_HALF)
    own = (lax.broadcasted_iota(jnp.int32, shape, 0) // SSM_GCH
           == lax.broadcasted_iota(jnp.int32, shape, 1) // SSM_STATE)

    def own_blocks(ref):
        return jnp.where(own, jnp.concatenate([ref[0]] * S5_TILE_GROUPS, axis=0), 0.0)

    br, bi, cr, ci = own_blocks(br_ref), own_blocks(bi_ref), own_blocks(cr_ref), own_blocks(ci_ref)
    bbr = kr * br - ki * bi
    bbi = kr * bi + ki * br
    b_tok1 = jnp.concatenate([bbr, bbi], axis=1).astype(BF16)
    b_tok0 = jnp.concatenate([ar * bbr - ai * bbi, ar * bbi + ai * bbr], axis=1).astype(BF16)
    b2_ref[0, 0:LANES, :] = b_tok0
    b2_ref[0, LANES:, :] = b_tok1
    c_tok0 = jnp.concatenate([cr * ar - ci * ai, -(cr * ai + ci * ar)], axis=1)
    c_tok1 = jnp.concatenate([cr * a2r - ci * a2i, -(cr * a2i + ci * a2r)], axis=1)
    c2_ref[0, :, 0:LANES] = c_tok0.T.astype(BF16)
    c2_ref[0, :, LANES:] = c_tok1.T.astype(BF16)
    c_plain = jnp.concatenate([cr, -ci], axis=1).astype(BF16)
    k0 = _dot_nt(b_tok1, c_plain).astype(BF16)
    k1 = _dot_nt(b_tok0, c_plain).astype(BF16)
    k2_ref[0, 0:LANES, 0:LANES] = k0
    k2_ref[0, 0:LANES, LANES:] = k1
    k2_ref[0, LANES:, 0:LANES] = jnp.zeros_like(k0)
    k2_ref[0, LANES:, LANES:] = k0


def _s5_prep(lam_re, lam_im, log_dt, b_re, b_im, c_re, c_im):
    nt, tg = S5_TILES, S5_TILE_GROUPS
    lr = lam_re.reshape(nt, 1, S5_HALF)
    li = lam_im.reshape(nt, 1, S5_HALF)
    ldt = jnp.broadcast_to(log_dt[:, None], (SSM_GROUPS, SSM_STATE)).reshape(nt, 1, S5_HALF)

    def b_cp(b):
        return b.reshape(nt, tg, SSM_STATE, SSM_GCH).transpose(0, 3, 1, 2).reshape(nt, SSM_GCH, S5_HALF)

    def c_cp(c):
        return c.reshape(nt, tg, SSM_GCH, SSM_STATE).transpose(0, 2, 1, 3).reshape(nt, SSM_GCH, S5_HALF)

    def spec(*shape):
        return pl.BlockSpec((1,) + shape, lambda j: (j, 0, 0))

    vec, chan = spec(1, S5_HALF), spec(SSM_GCH, S5_HALF)
    out_shapes = [(1, S5_HALF), (1, S5_HALF), (2 * LANES, 2 * S5_HALF), (2 * S5_HALF, 2 * LANES),
                  (2 * LANES, 2 * LANES)]
    out_dtypes = [F32, F32, BF16, BF16, BF16]
    return pl.pallas_call(
        _s5_prep_kernel,
        grid=(nt,),
        in_specs=[vec, vec, vec, chan, chan, chan, chan],
        out_specs=[spec(*s) for s in out_shapes],
        out_shape=[jax.ShapeDtypeStruct((nt,) + s, d) for s, d in zip(out_shapes, out_dtypes)],
        name="s5_prep",
    )(lr, li, ldt, b_cp(b_re), b_cp(b_im), c_cp(c_re), c_cp(c_im))


def _mem_kv_kernel(mem_ref, g_ref, wk_ref, wv_ref, k_ref, v_ref, kb_ref, vb_ref):
    m = _rms(mem_ref[...], g_ref[...]).astype(BF16)
    k = _dot(m, wk_ref[...])
    v = _dot(m, wv_ref[...])
    kb_ref[...] = k.astype(BF16)
    vb_ref[...] = v.astype(BF16)
    for b in range(k_ref.shape[0]):
        for h in range(N_XHEADS):
            rows = pl.ds(h, N_MEM, stride=N_XHEADS)
            k_ref[b, rows, :] = _head(k[b * N_MEM:(b + 1) * N_MEM], h)
            v_ref[b, rows, :] = _head(v[b * N_MEM:(b + 1) * N_MEM], h)


def _mem_kv(mem, g, wk, wv):
    rows = mem.shape[0]
    depth = g.shape[0]
    nb = 2
    tm = nb * N_MEM
    out = pl.BlockSpec((None, nb, N_MEM * N_XHEADS, XHEAD_DIM), lambda i, r: (i, r, 0, 0))
    outb = pl.BlockSpec((None, tm, XATT), lambda i, r: (i, r, 0))
    wsp = pl.BlockSpec((None, D_MODEL, XATT), lambda i, r: (i, 0, 0))
    return pl.pallas_call(
        _mem_kv_kernel,
        grid=(depth, rows // tm),
        in_specs=[pl.BlockSpec((tm, D_MODEL), lambda i, r: (r, 0)),
                  pl.BlockSpec((None, 1, D_MODEL), lambda i, r: (i, 0, 0)),
                  wsp, wsp],
        out_specs=[out, out, outb, outb],
        out_shape=[jax.ShapeDtypeStruct((depth, rows // N_MEM, N_MEM * N_XHEADS, XHEAD_DIM), F32)] * 2
        + [jax.ShapeDtypeStruct((depth, rows, XATT), BF16)] * 2,
        name="mem_kv",
    )(mem, g, wk, wv)


def _masked_spatial(ws_ref):
    row = lax.broadcasted_iota(jnp.int32, (CHUNK, CHUNK), 0)
    col = lax.broadcasted_iota(jnp.int32, (CHUNK, CHUNK), 1)
    return [jnp.where(row >= col, ws_ref[g], 0.0).astype(BF16) for g in range(A_GROUPS)]


def _gate_chunk(wm, vc, pair):
    base = pair * 2 * A_GDIM
    even = _dot(wm[2 * pair], vc[:, base:base + 2 * LANES])
    odd = _dot(wm[2 * pair + 1], vc[:, base + LANES:base + 3 * LANES])
    lane = lax.broadcasted_iota(jnp.int32, (CHUNK, LANES), 1)
    mid = jnp.where(lane < A_GDIM - LANES, even[:, LANES:], odd[:, :LANES])
    return jnp.concatenate([even[:, :LANES], mid, odd[:, LANES:]], axis=1)


def _s5_tile(u, hr, hi, a2r, a2i, b2, c2, k2, bu_ref, hs_ref, steps, nb):
    pairs = steps // 2
    u3 = u.reshape(pairs, 2 * nb, LANES)
    u2 = jnp.concatenate([u3[:, 0:nb, :].reshape(pairs * nb, LANES),
                          u3[:, nb:, :].reshape(pairs * nb, LANES)], axis=1).astype(BF16)
    bu_ref[...] = _dot(u2, b2)
    a2r = jnp.broadcast_to(a2r, (nb, S5_HALF))
    a2i = jnp.broadcast_to(a2i, (nb, S5_HALF))
    for k in range(pairs):
        rows = slice(k * nb, (k + 1) * nb)
        hs_ref[rows, 0:S5_HALF] = hr
        hs_ref[rows, S5_HALF:] = hi
        hr, hi = (a2r * hr - a2i * hi + bu_ref[rows, 0:S5_HALF],
                  a2r * hi + a2i * hr + bu_ref[rows, S5_HALF:])
    y2 = _dot(hs_ref[...].astype(BF16), c2) + _dot(u2, k2)
    return y2, hr, hi


def _layer_a_prompt_kernel(x_ref, k_ref, v_ref, win_ref, lng_ref, lnb_ref, ws_ref, bias_ref,
                           wout_ref, pre_ref, post_ref, o_ref, m_ref):
    tm = x_ref.shape[0]
    x = x_ref[...]
    h = _rms(x, pre_ref[...]).astype(BF16)
    gate = _silu(_dot(h, win_ref[:, 2 * BRANCH + XATT:]))
    u = _gelu(_dot(h, win_ref[:, 0:BRANCH]))
    v = _layer_norm(_gelu(_dot(h, win_ref[:, BRANCH:2 * BRANCH])), lng_ref[...], lnb_ref[...])
    vb = v.astype(BF16)
    wm = _masked_spatial(ws_ref)
    for c in range(tm // CHUNK):
        rs = slice(c * CHUNK, (c + 1) * CHUNK)
        vc = vb[rs]
        for pair in range(A_GROUPS // 2):
            cs = slice(pair * 2 * A_GDIM, (pair + 1) * 2 * A_GDIM)
            mixed = _gate_chunk(wm, vc, pair) + bias_ref[:, cs]
            m_ref[rs, cs] = (u[rs, cs] * mixed * gate[rs, cs]).astype(BF16)
    q = _dot(h, win_ref[:, 2 * BRANCH:2 * BRANCH + XATT])
    heads = range(N_XHEADS)
    k = k_ref[...]
    v = v_ref[...]
    att = _attend([_head(q, h) for h in heads], [_head(k, h) for h in heads], [_head(v, h) for h in heads])
    for h in heads:
        cs = slice(BRANCH + h * XHEAD_DIM, BRANCH + (h + 1) * XHEAD_DIM)
        m_ref[:, cs] = (att[h] * gate[:, cs]).astype(BF16)
    out = _dot(m_ref[...], wout_ref[...])
    o_ref[...] = x + _rms(out, post_ref[...])


def _layer_a_prompt(x, kb, vb, win, lng, lnb, ws, bias, wout, pre, post, layer):
    tm = 512
    nb, seq, _ = x.shape
    xs = pl.BlockSpec((None, tm, D_MODEL), lambda b, t: (b, t, 0))
    kv = pl.BlockSpec((None, N_MEM, XATT), lambda b, t: (layer, b, 0))
    return pl.pallas_call(
        _layer_a_prompt_kernel,
        grid=(nb, seq // tm),
        in_specs=[xs, kv, kv, _const_spec(win.shape), _const_spec(lng.shape), _const_spec(lnb.shape),
                  _const_spec(ws.shape), _const_spec(bias.shape), _layer_spec(wout.shape, layer),
                  _layer_spec(pre.shape, layer), _layer_spec(post.shape, layer)],
        out_specs=xs,
        out_shape=jax.ShapeDtypeStruct(x.shape, F32),
        scratch_shapes=[pltpu.VMEM((tm, MIX_WIDTH), BF16)],
        compiler_params=pltpu.CompilerParams(
            dimension_semantics=("arbitrary", "arbitrary"), vmem_limit_bytes=VMEM_LIMIT),
        name="layer_a_prompt",
    )(x, kb, vb, win, lng, lnb, ws, bias, wout, pre, post)


def _layer_b_prompt_kernel(x_ref, k_ref, v_ref, win_ref, a2r_ref, a2i_ref, b2_ref, c2_ref, k2_ref, d_ref,
                           wglu_ref, bglu_ref, wout_ref, pre_ref, post_ref,
                           o_ref, hre_ref, him_ref,
                           ut_ref, bu_ref, hs_ref, yt_ref, y_ref, m_ref):
    nb, tc, _ = x_ref.shape
    tm = nb * tc
    half = tc // 2

    @pl.when(pl.program_id(0) == 0)
    def _():
        hre_ref[...] = jnp.zeros_like(hre_ref)
        him_ref[...] = jnp.zeros_like(him_ref)

    x = x_ref[...].reshape(tm, D_MODEL)
    h = _rms(x, pre_ref[...]).astype(BF16)
    u = _dot(h, win_ref[:, 0:BRANCH])
    for j in range(S5_TILES):
        for b in range(nb):
            ut_ref[j, pl.ds(b, tc, stride=nb), :] = u[b * tc:(b + 1) * tc, j * LANES:(j + 1) * LANES]
    for j in range(S5_TILES):
        ss = slice(j * S5_HALF, (j + 1) * S5_HALF)
        par = j % 2
        y2, hr, hi = _s5_tile(ut_ref[j], hre_ref[:, ss], him_ref[:, ss], a2r_ref[j], a2i_ref[j],
                              b2_ref[j], c2_ref[j], k2_ref[j], bu_ref.at[par], hs_ref.at[par], tc, nb)
        hre_ref[:, ss] = hr
        him_ref[:, ss] = hi
        for tp in range(2):
            yt_ref[par, tp] = y2[:, tp * LANES:(tp + 1) * LANES]
            for b in range(nb):
                y_ref[j, pl.ds(b * tc + tp, half, stride=2), :] = yt_ref[par, tp, pl.ds(b, half, stride=nb), :]
    y = _gelu(jnp.concatenate([y_ref[j] for j in range(S5_TILES)], axis=1) + d_ref[...] * u)
    branch = y * _sigmoid(_dot(y.astype(BF16), wglu_ref[...]) + bglu_ref[...])
    gate = _silu(_dot(h, win_ref[:, BRANCH + XATT:]))
    m_ref[:, 0:BRANCH] = (branch * gate[:, 0:BRANCH]).astype(BF16)
    q = _dot(h, win_ref[:, BRANCH:BRANCH + XATT])
    pieces = [(b, h) for b in range(nb) for h in range(N_XHEADS)]
    att = _attend([_head(q[b * tc:(b + 1) * tc], h) for b, h in pieces],
                  [_head(k_ref[b * N_MEM:(b + 1) * N_MEM, :], h) for b, h in pieces],
                  [_head(v_ref[b * N_MEM:(b + 1) * N_MEM, :], h) for b, h in pieces])
    for (b, h), a in zip(pieces, att):
        rs = slice(b * tc, (b + 1) * tc)
        cs = slice(BRANCH + h * XHEAD_DIM, BRANCH + (h + 1) * XHEAD_DIM)
        m_ref[rs, cs] = (a * gate[rs, cs]).astype(BF16)
    out = _dot(m_ref[...], wout_ref[...])
    o_ref[...] = (x + _rms(out, post_ref[...])).reshape(nb, tc, D_MODEL)


def _layer_b_prompt(x, kb, vb, win, s5, dvec, wglu, bglu, wout, pre, post, layer):
    tc = 64
    nb, seq, _ = x.shape
    tm = nb * tc
    xs = pl.BlockSpec((nb, tc, D_MODEL), lambda t: (0, t, 0))
    hs = pl.BlockSpec((nb, SSM_GROUPS * SSM_STATE), lambda t: (0, 0))
    ins = [kb, vb, win, *s5, dvec, wglu, bglu, wout, pre, post]
    stacked = [True, True] + [False] * (len(s5) + 4) + [True, True, True]
    return pl.pallas_call(
        _layer_b_prompt_kernel,
        grid=(seq // tc,),
        in_specs=[xs] + [_layer_spec(a.shape, layer) if st else _const_spec(a.shape)
                         for a, st in zip(ins, stacked)],
        out_specs=[xs, hs, hs],
        out_shape=[jax.ShapeDtypeStruct(x.shape, F32),
                   jax.ShapeDtypeStruct((nb, SSM_GROUPS * SSM_STATE), F32),
                   jax.ShapeDtypeStruct((nb, SSM_GROUPS * SSM_STATE), F32)],
        scratch_shapes=[pltpu.VMEM((S5_TILES, tm, LANES), F32),
                        pltpu.VMEM((2, tm // 2, 2 * S5_HALF), F32),
                        pltpu.VMEM((2, tm // 2, 2 * S5_HALF), F32),
                        pltpu.VMEM((2, 2, tm // 2, LANES), F32),
                        pltpu.VMEM((S5_TILES, tm, LANES), F32),
                        pltpu.VMEM((tm, MIX_WIDTH), BF16)],
        compiler_params=pltpu.CompilerParams(
            dimension_semantics=("arbitrary",), vmem_limit_bytes=VMEM_LIMIT),
        name="layer_b_prompt",
    )(x, *ins)


def _sample_pre_a_kernel(x_ref, win_ref, lng_ref, lnb_ref, wrow_ref, brow_ref, pre_ref,
                         v_ref, mbr_ref, q_ref, gatt_ref):
    nb = DEC_BATCH
    h = _rms(x_ref[...], pre_ref[...]).astype(BF16)
    gate = _silu(_dot(h, win_ref[:, 2 * BRANCH + XATT:]))
    u = _gelu(_dot(h, win_ref[:, 0:BRANCH]))
    v = _layer_norm(_gelu(_dot(h, win_ref[:, BRANCH:2 * BRANCH])), lng_ref[...], lnb_ref[...])
    v_ref[...] = v
    for t in range(DEC_SEQ):
        rs = slice(t * nb, (t + 1) * nb)
        mixed = brow_ref[t:t + 1, :]
        for s in range(t + 1):
            mixed = mixed + wrow_ref[t * DEC_SEQ + s:t * DEC_SEQ + s + 1, :] * v[s * nb:(s + 1) * nb]
        mbr_ref[rs, :] = u[rs] * mixed * gate[rs, 0:BRANCH]
    q_ref[...] = _dot(h, win_ref[:, 2 * BRANCH:2 * BRANCH + XATT])
    gatt_ref[...] = gate[:, BRANCH:]


def _sample_pre_a(x, win, lng, lnb, wrow, brow, pre, layer):
    rows = x.shape[0]
    ins = [x, win, lng, lnb, wrow, brow, pre]
    shapes = [(rows, BRANCH), (rows, BRANCH), (rows, XATT), (rows, XATT)]
    return pl.pallas_call(
        _sample_pre_a_kernel,
        grid=(1,),
        in_specs=[_const_spec(a.shape) for a in ins[:-1]] + [_layer_spec(pre.shape, layer)],
        out_specs=[pl.BlockSpec(s, lambda i: (0, 0)) for s in shapes],
        out_shape=[jax.ShapeDtypeStruct(s, F32) for s in shapes],
        compiler_params=pltpu.CompilerParams(
            dimension_semantics=("arbitrary",), vmem_limit_bytes=VMEM_LIMIT),
        name="sample_pre_a",
    )(*ins)


def _sample_pre_b_kernel(x_ref, win_ref, sre_ref, sim_ref, a2r_ref, a2i_ref, b2_ref, c2_ref, k2_ref, d_ref,
                         wglu_ref, bglu_ref, pre_ref,
                         mbr_ref, q_ref, gatt_ref, hre_ref, him_ref,
                         bu_ref, hs_ref, y_ref):
    nb = DEC_BATCH
    h = _rms(x_ref[...], pre_ref[...]).astype(BF16)
    u = _dot(h, win_ref[:, 0:BRANCH])
    for j in range(S5_TILES):
        ss = slice(j * S5_HALF, (j + 1) * S5_HALF)
        ls = slice(j * LANES, (j + 1) * LANES)
        y2, hr, hi = _s5_tile(u[:, ls], sre_ref[:, ss], sim_ref[:, ss], a2r_ref[j], a2i_ref[j],
                              b2_ref[j], c2_ref[j], k2_ref[j], bu_ref, hs_ref, DEC_SEQ, nb)
        hre_ref[:, ss] = hr
        him_ref[:, ss] = hi
        for pair in range(DEC_SEQ // 2):
            for tp in range(2):
                t = 2 * pair + tp
                y_ref[t * nb:(t + 1) * nb, ls] = y2[pair * nb:(pair + 1) * nb, tp * LANES:(tp + 1) * LANES]
    y = _gelu(y_ref[...] + d_ref[...] * u)
    branch = y * _sigmoid(_dot(y.astype(BF16), wglu_ref[...]) + bglu_ref[...])
    gate = _silu(_dot(h, win_ref[:, BRANCH + XATT:]))
    mbr_ref[...] = branch * gate[:, 0:BRANCH]
    q_ref[...] = _dot(h, win_ref[:, BRANCH:BRANCH + XATT])
    gatt_ref[...] = gate[:, BRANCH:]


def _sample_pre_b(x, win, sre, sim, s5, dvec, wglu, bglu, pre, layer):
    rows = x.shape[0]
    ins = [x, win, sre, sim, *s5, dvec, wglu, bglu]
    shapes = [(rows, BRANCH), (rows, XATT), (rows, XATT), sre.shape, sim.shape]
    return pl.pallas_call(
        _sample_pre_b_kernel,
        grid=(1,),
        in_specs=[_const_spec(a.shape) for a in ins] + [_layer_spec(pre.shape, layer)],
        out_specs=[pl.BlockSpec(s, lambda i: (0, 0)) for s in shapes],
        out_shape=[jax.ShapeDtypeStruct(s, F32) for s in shapes],
        scratch_shapes=[pltpu.VMEM((rows // 2, 2 * S5_HALF), F32),
                        pltpu.VMEM((rows // 2, 2 * S5_HALF), F32),
                        pltpu.VMEM((rows, BRANCH), F32)],
        compiler_params=pltpu.CompilerParams(
            dimension_semantics=("arbitrary",), vmem_limit_bytes=VMEM_LIMIT),
        name="sample_pre_b",
    )(*ins, pre)


def _sample_attn_kernel(q_ref, g_ref, k_ref, v_ref, o_ref):
    bb, rows, _ = q_ref.shape
    cols = k_ref.shape[1]
    row_h = (lax.broadcasted_iota(jnp.int32, (bb * rows, cols), 0) // DEC_SEQ) % N_XHEADS
    col_h = lax.broadcasted_iota(jnp.int32, (bb * rows, cols), 1) % N_XHEADS
    s = jnp.concatenate([_dot_nt(q_ref[j].astype(BF16), k_ref[j].astype(BF16)) for j in range(bb)], axis=0)
    s = jnp.where(row_h == col_h, s * (XHEAD_DIM ** -0.5), -1e30)
    p = jnp.exp(s - jnp.max(s, axis=-1, keepdims=True))
    inv = 1.0 / jnp.sum(p, axis=-1, keepdims=True)
    pb = p.astype(BF16)
    for j in range(bb):
        rs = slice(j * rows, (j + 1) * rows)
        o_ref[j] = _dot(pb[rs], v_ref[j].astype(BF16)) * inv[rs] * g_ref[j]


def _sample_attn(q, gatt, cache_k, cache_v, layer):
    bb = 8
    qs = pl.BlockSpec((bb,) + q.shape[1:], lambda i: (i, 0, 0))
    cs = pl.BlockSpec((None, bb) + cache_k.shape[2:], lambda i: (layer, i, 0, 0))
    return pl.pallas_call(
        _sample_attn_kernel,
        grid=(q.shape[0] // bb,),
        in_specs=[qs, qs, cs, cs],
        out_specs=qs,
        out_shape=jax.ShapeDtypeStruct(q.shape, F32),
        compiler_params=pltpu.CompilerParams(
            dimension_semantics=("arbitrary",), vmem_limit_bytes=VMEM_LIMIT),
        name="sample_attn",
    )(q, gatt, cache_k, cache_v)


def _sample_post_kernel(x_ref, mbr_ref, matt_ref, wout_ref, post_ref, o_ref):
    out = (_dot(mbr_ref[...].astype(BF16), wout_ref[0:BRANCH, :])
           + _dot(matt_ref[...].astype(BF16), wout_ref[BRANCH:, :]))
    o_ref[...] = x_ref[...] + _rms(out, post_ref[...])


def _sample_post(x, mbr, matt, wout, post, layer):
    ins = [x, mbr, matt, wout, post]
    return pl.pallas_call(
        _sample_post_kernel,
        grid=(1,),
        in_specs=[_const_spec(a.shape) for a in ins[:3]]
        + [_layer_spec(wout.shape, layer), _layer_spec(post.shape, layer)],
        out_specs=pl.BlockSpec(x.shape, lambda i: (0, 0)),
        out_shape=jax.ShapeDtypeStruct(x.shape, F32),
        compiler_params=pltpu.CompilerParams(
            dimension_semantics=("arbitrary",), vmem_limit_bytes=VMEM_LIMIT),
        name="sample_post",
    )(*ins)


def _to_bt(a):
    return a.reshape(DEC_SEQ, DEC_BATCH, a.shape[-1]).transpose(1, 0, 2)


def _to_tb(a):
    return a.transpose(1, 0, 2).reshape(DEC_SEQ * DEC_BATCH, a.shape[-1])


def _to_bht(a):
    a = a.reshape(DEC_SEQ, DEC_BATCH, N_XHEADS, XHEAD_DIM).transpose(1, 2, 0, 3)
    return a.reshape(DEC_BATCH, N_XHEADS * DEC_SEQ, XHEAD_DIM)


def _from_bht(a):
    a = a.reshape(DEC_BATCH, N_XHEADS, DEC_SEQ, XHEAD_DIM).transpose(2, 0, 1, 3)
    return a.reshape(DEC_SEQ * DEC_BATCH, XATT)


def kernel(x_prompt, x_sample, cache_mem_k, cache_mem_v, state_ssm_re, state_ssm_im, mem_prompt,
           w_in_a, ln_v_g, ln_v_b, w_spatial, b_spatial,
           w_in_b, ssm_lambda_re, ssm_lambda_im, ssm_log_dt, ssm_b_re, ssm_b_im, ssm_c_re, ssm_c_im,
           ssm_d, w_glu, b_glu,
           mem_norm_g, w_mem_k, w_mem_v, w_out, pre_norm_g, post_norm_g):
    depth = w_out.shape[0]
    win_a = w_in_a[0].astype(BF16)
    win_b = w_in_b[0].astype(BF16)
    wglu = w_glu[0].astype(BF16)
    wout = w_out.astype(BF16)
    wk = w_mem_k.astype(BF16)
    wv = w_mem_v.astype(BF16)
    pre = pre_norm_g.reshape(depth, 1, D_MODEL)
    post = post_norm_g.reshape(depth, 1, D_MODEL)
    lng = ln_v_g[0].reshape(1, BRANCH)
    lnb = ln_v_b[0].reshape(1, BRANCH)
    bglu = b_glu[0].reshape(1, BRANCH)
    dvec = ssm_d[0].reshape(1, BRANCH)
    bias = jnp.repeat(b_spatial[0].T, A_GDIM, axis=1)
    ws4 = w_spatial[0][:, :DEC_SEQ, :DEC_SEQ]
    wrow = jnp.repeat(ws4.transpose(1, 2, 0).reshape(DEC_SEQ * DEC_SEQ, A_GROUPS), A_GDIM, axis=1)
    brow = bias[:DEC_SEQ]

    s5 = _s5_prep(ssm_lambda_re[0], ssm_lambda_im[0], ssm_log_dt[0],
                  ssm_b_re[0], ssm_b_im[0], ssm_c_re[0], ssm_c_im[0])

    mem = mem_prompt.reshape(BATCH * N_MEM, D_MODEL)
    mk, mv, mkb, mvb = _mem_kv(mem, mem_norm_g.reshape(depth, 1, D_MODEL), wk, wv)

    yp = _layer_a_prompt(x_prompt, mkb, mvb, win_a, lng, lnb, w_spatial[0], bias, wout, pre, post, 0)
    yp, hp_re, hp_im = _layer_b_prompt(yp, mkb, mvb, win_b, s5, dvec, wglu, bglu, wout, pre, post, 1)

    ck = cache_mem_k.reshape(depth, DEC_BATCH, N_MEM * N_XHEADS, XHEAD_DIM)
    cv = cache_mem_v.reshape(depth, DEC_BATCH, N_MEM * N_XHEADS, XHEAD_DIM)
    xs = _to_tb(x_sample)
    v_rows, mbr, q, gatt = _sample_pre_a(xs, win_a, lng, lnb, wrow, brow, pre, 0)
    matt = _sample_attn(_to_bht(q), _to_bht(gatt), ck, cv, 0)
    xs = _sample_post(xs, mbr, _from_bht(matt), wout, post, 0)
    sre = state_ssm_re[0].reshape(DEC_BATCH, SSM_GROUPS * SSM_STATE)
    sim = state_ssm_im[0].reshape(DEC_BATCH, SSM_GROUPS * SSM_STATE)
    mbr, q, gatt, hs_re, hs_im = _sample_pre_b(xs, win_b, sre, sim, s5, dvec, wglu, bglu, pre, 1)
    matt = _sample_attn(_to_bht(q), _to_bht(gatt), ck, cv, 1)
    xs = _sample_post(xs, mbr, _from_bht(matt), wout, post, 1)

    kv_shape = (depth, BATCH, N_MEM, N_XHEADS, XHEAD_DIM)
    st_p = (1, BATCH, SSM_GROUPS, SSM_STATE)
    st_s = (1, DEC_BATCH, SSM_GROUPS, SSM_STATE)
    return (yp, _to_bt(xs), mk.reshape(kv_shape), mv.reshape(kv_shape),
            hp_re.reshape(st_p), hp_im.reshape(st_p), hs_re.reshape(st_s), hs_im.reshape(st_s),
            _to_bt(v_rows)[None])
```

```python
import functools
import math

import jax
import jax.numpy as jnp
from jax import lax
from jax.experimental import pallas as pl
from jax.experimental.pallas import tpu as pltpu

D_MODEL = 1024
BATCH = 8
SEQ = 2048
DEC_BATCH = 128
DEC_SEQ = 4
BRANCH = 1536
N_XHEADS = 4
XHEAD_DIM = 128
XATT = N_XHEADS * XHEAD_DIM
MIX_WIDTH = BRANCH + XATT
N_MEM = 256
CHUNK = 128
A_GROUPS = 8
A_GDIM = BRANCH // A_GROUPS
SSM_GCH = 16
SSM_GROUPS = 96
SSM_STATE = 64
EPS = 1e-6

LANES = 128
S5_TILE_GROUPS = LANES // SSM_GCH
S5_TILES = SSM_GROUPS // S5_TILE_GROUPS
S5_HALF = S5_TILE_GROUPS * SSM_STATE
VMEM_LIMIT = 60 * 1024 * 1024

F32 = jnp.float32
BF16 = jnp.bfloat16


def _dot(a, b):
    return jnp.dot(a, b, preferred_element_type=F32)


def _rms(x, g):
    return x * lax.rsqrt(jnp.mean(x * x, axis=-1, keepdims=True) + EPS) * g


def _layer_norm(x, g, b):
    mu = jnp.mean(x, axis=-1, keepdims=True)
    xc = x - mu
    var = jnp.mean(xc * xc, axis=-1, keepdims=True)
    return xc * lax.rsqrt(var + EPS) * g + b


def _gelu(x):
    c = math.sqrt(2.0 / math.pi)
    hx = 0.5 * x
    return hx + hx * jnp.tanh(x * (c + (c * 0.044715) * (x * x)))


def _sigmoid(x):
    return 0.5 + 0.5 * jnp.tanh(0.5 * x)


def _silu(x):
    hx = 0.5 * x
    return hx + hx * jnp.tanh(hx)


def _dot_nt(a, b):
    return lax.dot_general(a, b, (((1,), (1,)), ((), ())), preferred_element_type=F32)


def _attend(qs, kts, vs):
    r = qs[0].shape[0]
    s = jnp.concatenate([_dot(q.astype(BF16), kt) for q, kt in zip(qs, kts)], axis=0)
    s = s * (XHEAD_DIM ** -0.5)
    p = jnp.exp(s - jnp.max(s, axis=-1, keepdims=True))
    inv = 1.0 / jnp.sum(p, axis=-1, keepdims=True)
    pb = p.astype(BF16)
    return [_dot(pb[i * r:(i + 1) * r], v) * inv[i * r:(i + 1) * r] for i, v in enumerate(vs)]


def _head(a, h):
    return a[:, h * XHEAD_DIM:(h + 1) * XHEAD_DIM]


def _const_spec(shape):
    nd = len(shape)
    return pl.BlockSpec(shape, lambda *_: (0,) * nd, pipeline_mode=pl.Buffered(1))


def _layer_spec(shape, layer):
    nd = len(shape)
    return pl.BlockSpec((None,) + tuple(shape[1:]), lambda *_: (layer,) + (0,) * (nd - 1),
                        pipeline_mode=pl.Buffered(1))


def _lam_bar(lr, li, ldt):
    dt = jnp.exp(ldt)
    e = jnp.exp(lr * dt)
    return e * jnp.cos(li * dt), e * jnp.sin(li * dt)


def _s5_prep_kernel(*refs):
    for j in range(refs[0].shape[0]):
        _s5_prep_tile(*[r.at[pl.ds(j, 1)] for r in refs])


def _s5_prep_tile(lr_ref, li_ref, ldt_ref, br_ref, bi_ref, cr_ref, ci_ref,
                  a2r_ref, a2i_ref, b2_ref, c2_ref, k2_ref):
    lr = lr_ref[0]
    li = li_ref[0]
    ar, ai = _lam_bar(lr, li, ldt_ref[0])
    a2r = ar * ar - ai * ai
    a2i = 2.0 * ar * ai
    a2r_ref[0] = a2r
    a2i_ref[0] = a2i
    nr = ar - 1.0
    den = lr * lr + li * li
    kr = (nr * lr + ai * li) / den
    ki = (ai * lr - nr * li) / den
    shape = (LANES, S5_HALF)
    own = (lax.broadcasted_iota(jnp.int32, shape, 0) // SSM_GCH
           == lax.broadcasted_iota(jnp.int32, shape, 1) // SSM_STATE)

    def own_blocks(ref):
        return jnp.where(own, jnp.concatenate([ref[0]] * S5_TILE_GROUPS, axis=0), 0.0)

    br, bi, cr, ci = own_blocks(br_ref), own_blocks(bi_ref), own_blocks(cr_ref), own_blocks(ci_ref)
    bbr = kr * br - ki * bi
    bbi = kr * bi + ki * br
    b_tok1 = jnp.concatenate([bbr, bbi], axis=1).astype(BF16)
    b_tok0 = jnp.concatenate([ar * bbr - ai * bbi, ar * bbi + ai * bbr], axis=1).astype(BF16)
    b2_ref[0, 0:LANES, :] = b_tok0
    b2_ref[0, LANES:, :] = b_tok1
    c_tok0 = jnp.concatenate([cr * ar - ci * ai, -(cr * ai + ci * ar)], axis=1)
    c_tok1 = jnp.concatenate([cr * a2r - ci * a2i, -(cr * a2i + ci * a2r)], axis=1)
    c2_ref[0, :, 0:LANES] = c_tok0.T.astype(BF16)
    c2_ref[0, :, LANES:] = c_tok1.T.astype(BF16)
    c_plain = jnp.concatenate([cr, -ci], axis=1).astype(BF16)
    k0 = _dot_nt(b_tok1, c_plain).astype(BF16)
    k1 = _dot_nt(b_tok0, c_plain).astype(BF16)
    k2_ref[0, 0:LANES, 0:LANES] = k0
    k2_ref[0, 0:LANES, LANES:] = k1
    k2_ref[0, LANES:, 0:LANES] = jnp.zeros_like(k0)
    k2_ref[0, LANES:, LANES:] = k0


def _s5_prep(lam_re, lam_im, log_dt, b_re, b_im, c_re, c_im):
    nt, tg = S5_TILES, S5_TILE_GROUPS
    lr = lam_re.reshape(nt, 1, S5_HALF)
    li = lam_im.reshape(nt, 1, S5_HALF)
    ldt = jnp.broadcast_to(log_dt[:, None], (SSM_GROUPS, SSM_STATE)).reshape(nt, 1, S5_HALF)

    def b_cp(b):
        return b.reshape(nt, tg, SSM_STATE, SSM_GCH).transpose(0, 3, 1, 2).reshape(nt, SSM_GCH, S5_HALF)

    def c_cp(c):
        return c.reshape(nt, tg, SSM_GCH, SSM_STATE).transpose(0, 2, 1, 3).reshape(nt, SSM_GCH, S5_HALF)

    per_step = 3

    def spec(*shape):
        return pl.BlockSpec((per_step,) + shape, lambda j: (j, 0, 0))

    vec, chan = spec(1, S5_HALF), spec(SSM_GCH, S5_HALF)
    out_shapes = [(1, S5_HALF), (1, S5_HALF), (2 * LANES, 2 * S5_HALF), (2 * S5_HALF, 2 * LANES),
                  (2 * LANES, 2 * LANES)]
    out_dtypes = [F32, F32, BF16, BF16, BF16]
    return pl.pallas_call(
        _s5_prep_kernel,
        grid=(nt // per_step,),
        in_specs=[vec, vec, vec, chan, chan, chan, chan],
        out_specs=[spec(*s) for s in out_shapes],
        out_shape=[jax.ShapeDtypeStruct((nt,) + s, d) for s, d in zip(out_shapes, out_dtypes)],
        name="s5_prep",
    )(lr, li, ldt, b_cp(b_re), b_cp(b_im), c_cp(c_re), c_cp(c_im))


def _mem_kv_kernel(mem_ref, g_ref, wk_ref, wv_ref, k_ref, v_ref, kt_ref, vb_ref):
    mem = mem_ref[...]
    scaled = mem * lax.rsqrt(jnp.mean(mem * mem, axis=-1, keepdims=True) + EPS)
    for i in range(g_ref.shape[0]):
        m = (scaled * g_ref[i]).astype(BF16)
        k = _dot(m, wk_ref[i])
        v = _dot(m, wv_ref[i])
        vb_ref[i] = v.astype(BF16)
        for b in range(k_ref.shape[1]):
            kb = k[b * N_MEM:(b + 1) * N_MEM]
            vb = v[b * N_MEM:(b + 1) * N_MEM]
            kt_ref[i, b] = kb.T.astype(BF16)
            for h in range(N_XHEADS):
                rows = pl.ds(h, N_MEM, stride=N_XHEADS)
                k_ref[i, b, rows, :] = _head(kb, h)
                v_ref[i, b, rows, :] = _head(vb, h)


def _mem_kv(mem, g, wk, wv):
    rows = mem.shape[0]
    depth = g.shape[0]
    nb = 2
    tm = nb * N_MEM
    return pl.pallas_call(
        _mem_kv_kernel,
        grid=(rows // tm,),
        in_specs=[pl.BlockSpec((tm, D_MODEL), lambda r: (r, 0)),
                  _const_spec(g.shape), _const_spec(wk.shape), _const_spec(wv.shape)],
        out_specs=[pl.BlockSpec((depth, nb, N_MEM * N_XHEADS, XHEAD_DIM), lambda r: (0, r, 0, 0))] * 2
        + [pl.BlockSpec((depth, nb, XATT, N_MEM), lambda r: (0, r, 0, 0)),
           pl.BlockSpec((depth, tm, XATT), lambda r: (0, r, 0))],
        out_shape=[jax.ShapeDtypeStruct((depth, rows // N_MEM, N_MEM * N_XHEADS, XHEAD_DIM), F32)] * 2
        + [jax.ShapeDtypeStruct((depth, rows // N_MEM, XATT, N_MEM), BF16),
           jax.ShapeDtypeStruct((depth, rows, XATT), BF16)],
        compiler_params=pltpu.CompilerParams(
            dimension_semantics=("arbitrary",), vmem_limit_bytes=VMEM_LIMIT),
        name="mem_kv",
    )(mem, g, wk, wv)


def _masked_spatial(ws_ref):
    row = lax.broadcasted_iota(jnp.int32, (CHUNK, CHUNK), 0)
    col = lax.broadcasted_iota(jnp.int32, (CHUNK, CHUNK), 1)
    return [jnp.where(row >= col, ws_ref[g], 0.0).astype(BF16) for g in range(A_GROUPS)]


def _gate_chunk(wm, vc, pair):
    base = pair * 2 * A_GDIM
    even = _dot(wm[2 * pair], vc[:, base:base + 2 * LANES])
    odd = _dot(wm[2 * pair + 1], vc[:, base + LANES:base + 3 * LANES])
    lane = lax.broadcasted_iota(jnp.int32, (CHUNK, LANES), 1)
    mid = jnp.where(lane < A_GDIM - LANES, even[:, LANES:], odd[:, :LANES])
    return jnp.concatenate([even[:, :LANES], mid, odd[:, LANES:]], axis=1)


def _s5_tile(u, hr, hi, a2r, a2i, b2, c2, k2, bu_ref, hs_ref, steps, nb):
    pairs = steps // 2
    u3 = u.reshape(pairs, 2 * nb, LANES)
    u2 = jnp.concatenate([u3[:, 0:nb, :].reshape(pairs * nb, LANES),
                          u3[:, nb:, :].reshape(pairs * nb, LANES)], axis=1).astype(BF16)
    bu_ref[...] = _dot(u2, b2)
    a2r = jnp.broadcast_to(a2r, (nb, S5_HALF))
    a2i = jnp.broadcast_to(a2i, (nb, S5_HALF))
    for k in range(pairs):
        rows = slice(k * nb, (k + 1) * nb)
        hs_ref[rows, 0:S5_HALF] = hr
        hs_ref[rows, S5_HALF:] = hi
        hr, hi = (a2r * hr - a2i * hi + bu_ref[rows, 0:S5_HALF],
                  a2r * hi + a2i * hr + bu_ref[rows, S5_HALF:])
    y2 = _dot(hs_ref[...].astype(BF16), c2) + _dot(u2, k2)
    return y2, hr, hi


def _layer_a_prompt_kernel(x_ref, kt_ref, v_ref, win_ref, lng_ref, lnb_ref, ws_ref, bias_ref,
                           wout_ref, pre_ref, post_ref, o_ref, m_ref):
    tm = x_ref.shape[0]
    x = x_ref[...]
    h = _rms(x, pre_ref[...]).astype(BF16)
    gate = _silu(_dot(h, win_ref[:, 2 * BRANCH + XATT:]))
    u = _gelu(_dot(h, win_ref[:, 0:BRANCH]))
    v = _layer_norm(_gelu(_dot(h, win_ref[:, BRANCH:2 * BRANCH])), lng_ref[...], lnb_ref[...])
    vb = v.astype(BF16)
    wm = _masked_spatial(ws_ref)
    for c in range(tm // CHUNK):
        rs = slice(c * CHUNK, (c + 1) * CHUNK)
        vc = vb[rs]
        for pair in range(A_GROUPS // 2):
            cs = slice(pair * 2 * A_GDIM, (pair + 1) * 2 * A_GDIM)
            mixed = _gate_chunk(wm, vc, pair) + bias_ref[:, cs]
            m_ref[rs, cs] = (u[rs, cs] * mixed * gate[rs, cs]).astype(BF16)
    q = _dot(h, win_ref[:, 2 * BRANCH:2 * BRANCH + XATT])
    heads = range(N_XHEADS)
    v = v_ref[...]
    att = _attend([_head(q, h) for h in heads],
                  [kt_ref[h * XHEAD_DIM:(h + 1) * XHEAD_DIM, :] for h in heads],
                  [_head(v, h) for h in heads])
    for h in heads:
        cs = slice(BRANCH + h * XHEAD_DIM, BRANCH + (h + 1) * XHEAD_DIM)
        m_ref[:, cs] = (att[h] * gate[:, cs]).astype(BF16)
    out = _dot(m_ref[...], wout_ref[...])
    o_ref[...] = x + _rms(out, post_ref[...])


def _layer_a_prompt(x, kb, vb, win, lng, lnb, ws, bias, wout, pre, post, layer):
    tm = 512
    nb, seq, _ = x.shape
    xs = pl.BlockSpec((None, tm, D_MODEL), lambda b, t: (b, t, 0))
    kts = pl.BlockSpec((None, None, XATT, N_MEM), lambda b, t: (layer, b, 0, 0))
    vs = pl.BlockSpec((None, N_MEM, XATT), lambda b, t: (layer, b, 0))
    return pl.pallas_call(
        _layer_a_prompt_kernel,
        grid=(nb, seq // tm),
        in_specs=[xs, kts, vs, _const_spec(win.shape), _const_spec(lng.shape), _const_spec(lnb.shape),
                  _const_spec(ws.shape), _const_spec(bias.shape), _layer_spec(wout.shape, layer),
                  _layer_spec(pre.shape, layer), _layer_spec(post.shape, layer)],
        out_specs=xs,
        out_shape=jax.ShapeDtypeStruct(x.shape, F32),
        scratch_shapes=[pltpu.VMEM((tm, MIX_WIDTH), BF16)],
        compiler_params=pltpu.CompilerParams(
            dimension_semantics=("arbitrary", "arbitrary"), vmem_limit_bytes=VMEM_LIMIT),
        name="layer_a_prompt",
    )(x, kb, vb, win, lng, lnb, ws, bias, wout, pre, post)


def _layer_b_prompt_kernel(x_ref, kt_ref, v_ref, win_ref, a2r_ref, a2i_ref, b2_ref, c2_ref, k2_ref, d_ref,
                           wglu_ref, bglu_ref, wout_ref, pre_ref, post_ref,
                           o_ref, hre_ref, him_ref,
                           ut_ref, bu_ref, hs_ref, yt_ref, y_ref, m_ref):
    nb, tc, _ = x_ref.shape
    tm = nb * tc
    half = tc // 2

    @pl.when(pl.program_id(0) == 0)
    def _():
        hre_ref[...] = jnp.zeros_like(hre_ref)
        him_ref[...] = jnp.zeros_like(him_ref)

    x = x_ref[...].reshape(tm, D_MODEL)
    h = _rms(x, pre_ref[...]).astype(BF16)
    u = _dot(h, win_ref[:, 0:BRANCH])
    for j in range(S5_TILES):
        for b in range(nb):
            ut_ref[j, pl.ds(b, tc, stride=nb), :] = u[b * tc:(b + 1) * tc, j * LANES:(j + 1) * LANES]
    for j in range(S5_TILES):
        ss = slice(j * S5_HALF, (j + 1) * S5_HALF)
        par = j % 2
        y2, hr, hi = _s5_tile(ut_ref[j], hre_ref[:, ss], him_ref[:, ss], a2r_ref[j], a2i_ref[j],
                              b2_ref[j], c2_ref[j], k2_ref[j], bu_ref.at[par], hs_ref.at[par], tc, nb)
        hre_ref[:, ss] = hr
        him_ref[:, ss] = hi
        for tp in range(2):
            yt_ref[par, tp] = y2[:, tp * LANES:(tp + 1) * LANES]
            for b in range(nb):
                y_ref[j, pl.ds(b * tc + tp, half, stride=2), :] = yt_ref[par, tp, pl.ds(b, half, stride=nb), :]
    y = _gelu(jnp.concatenate([y_ref[j] for j in range(S5_TILES)], axis=1) + d_ref[...] * u)
    branch = y * _sigmoid(_dot(y.astype(BF16), wglu_ref[...]) + bglu_ref[...])
    gate = _silu(_dot(h, win_ref[:, BRANCH + XATT:]))
    m_ref[:, 0:BRANCH] = (branch * gate[:, 0:BRANCH]).astype(BF16)
    q = _dot(h, win_ref[:, BRANCH:BRANCH + XATT])
    pieces = [(b, h) for b in range(nb) for h in range(N_XHEADS)]
    att = _attend([_head(q[b * tc:(b + 1) * tc], h) for b, h in pieces],
                  [kt_ref[b, h * XHEAD_DIM:(h + 1) * XHEAD_DIM, :] for b, h in pieces],
                  [_head(v_ref[b * N_MEM:(b + 1) * N_MEM, :], h) for b, h in pieces])
    for (b, h), a in zip(pieces, att):
        rs = slice(b * tc, (b + 1) * tc)
        cs = slice(BRANCH + h * XHEAD_DIM, BRANCH + (h + 1) * XHEAD_DIM)
        m_ref[rs, cs] = (a * gate[rs, cs]).astype(BF16)
    out = _dot(m_ref[...], wout_ref[...])
    o_ref[...] = (x + _rms(out, post_ref[...])).reshape(nb, tc, D_MODEL)


def _layer_b_prompt(x, kb, vb, win, s5, dvec, wglu, bglu, wout, pre, post, layer):
    tc = 64
    nb, seq, _ = x.shape
    tm = nb * tc
    xs = pl.BlockSpec((nb, tc, D_MODEL), lambda t: (0, t, 0))
    hs = pl.BlockSpec((nb, SSM_GROUPS * SSM_STATE), lambda t: (0, 0))
    ins = [kb, vb, win, *s5, dvec, wglu, bglu, wout, pre, post]
    stacked = [True, True] + [False] * (len(s5) + 4) + [True, True, True]
    return pl.pallas_call(
        _layer_b_prompt_kernel,
        grid=(seq // tc,),
        in_specs=[xs] + [_layer_spec(a.shape, layer) if st else _const_spec(a.shape)
                         for a, st in zip(ins, stacked)],
        out_specs=[xs, hs, hs],
        out_shape=[jax.ShapeDtypeStruct(x.shape, F32),
                   jax.ShapeDtypeStruct((nb, SSM_GROUPS * SSM_STATE), F32),
                   jax.ShapeDtypeStruct((nb, SSM_GROUPS * SSM_STATE), F32)],
        scratch_shapes=[pltpu.VMEM((S5_TILES, tm, LANES), F32),
                        pltpu.VMEM((2, tm // 2, 2 * S5_HALF), F32),
                        pltpu.VMEM((2, tm // 2, 2 * S5_HALF), F32),
                        pltpu.VMEM((2, 2, tm // 2, LANES), F32),
                        pltpu.VMEM((S5_TILES, tm, LANES), F32),
                        pltpu.VMEM((tm, MIX_WIDTH), BF16)],
        compiler_params=pltpu.CompilerParams(
            dimension_semantics=("arbitrary",), vmem_limit_bytes=VMEM_LIMIT),
        name="layer_b_prompt",
    )(x, *ins)


Q_ROWS = N_XHEADS * DEC_SEQ


def _store_q_by_sequence(q_ref, q):
    for h in range(N_XHEADS):
        for t in range(DEC_SEQ):
            q_ref[pl.ds(h * DEC_SEQ + t, DEC_BATCH, stride=Q_ROWS), :] = _head(q[t * DEC_BATCH:(t + 1) * DEC_BATCH], h)


def _sample_pre_a_kernel(x_ref, win_ref, lng_ref, lnb_ref, wrow_ref, brow_ref, pre_ref,
                         v_ref, mbr_ref, q_ref, gatt_ref):
    nb = DEC_BATCH
    h = _rms(x_ref[...], pre_ref[...]).astype(BF16)
    gate = _silu(_dot(h, win_ref[:, 2 * BRANCH + XATT:]))
    u = _gelu(_dot(h, win_ref[:, 0:BRANCH]))
    v = _layer_norm(_gelu(_dot(h, win_ref[:, BRANCH:2 * BRANCH])), lng_ref[...], lnb_ref[...])
    v_ref[...] = v
    for t in range(DEC_SEQ):
        rs = slice(t * nb, (t + 1) * nb)
        mixed = brow_ref[t:t + 1, :]
        for s in range(t + 1):
            mixed = mixed + wrow_ref[t * DEC_SEQ + s:t * DEC_SEQ + s + 1, :] * v[s * nb:(s + 1) * nb]
        mbr_ref[rs, :] = u[rs] * mixed * gate[rs, 0:BRANCH]
    _store_q_by_sequence(q_ref, _dot(h, win_ref[:, 2 * BRANCH:2 * BRANCH + XATT]))
    gatt_ref[...] = gate[:, BRANCH:]


def _sample_pre_a(x, win, lng, lnb, wrow, brow, pre, layer):
    rows = x.shape[0]
    ins = [x, win, lng, lnb, wrow, brow, pre]
    shapes = [(rows, BRANCH), (rows, BRANCH), (DEC_BATCH * Q_ROWS, XHEAD_DIM), (rows, XATT)]
    return pl.pallas_call(
        _sample_pre_a_kernel,
        grid=(1,),
        in_specs=[_const_spec(a.shape) for a in ins[:-1]] + [_layer_spec(pre.shape, layer)],
        out_specs=[pl.BlockSpec(s, lambda i: (0, 0)) for s in shapes],
        out_shape=[jax.ShapeDtypeStruct(s, F32) for s in shapes],
        compiler_params=pltpu.CompilerParams(
            dimension_semantics=("arbitrary",), vmem_limit_bytes=VMEM_LIMIT),
        name="sample_pre_a",
    )(*ins)


def _sample_pre_b_kernel(x_ref, win_ref, sre_ref, sim_ref, a2r_ref, a2i_ref, b2_ref, c2_ref, k2_ref, d_ref,
                         wglu_ref, bglu_ref, pre_ref,
                         mbr_ref, q_ref, gatt_ref, hre_ref, him_ref,
                         bu_ref, hs_ref, y_ref):
    nb = DEC_BATCH
    h = _rms(x_ref[...], pre_ref[...]).astype(BF16)
    u = _dot(h, win_ref[:, 0:BRANCH])
    for j in range(S5_TILES):
        ss = slice(j * S5_HALF, (j + 1) * S5_HALF)
        ls = slice(j * LANES, (j + 1) * LANES)
        y2, hr, hi = _s5_tile(u[:, ls], sre_ref[:, ss], sim_ref[:, ss], a2r_ref[j], a2i_ref[j],
                              b2_ref[j], c2_ref[j], k2_ref[j], bu_ref, hs_ref, DEC_SEQ, nb)
        hre_ref[:, ss] = hr
        him_ref[:, ss] = hi
        for pair in range(DEC_SEQ // 2):
            for tp in range(2):
                t = 2 * pair + tp
                y_ref[t * nb:(t + 1) * nb, ls] = y2[pair * nb:(pair + 1) * nb, tp * LANES:(tp + 1) * LANES]
    y = _gelu(y_ref[...] + d_ref[...] * u)
    branch = y * _sigmoid(_dot(y.astype(BF16), wglu_ref[...]) + bglu_ref[...])
    gate = _silu(_dot(h, win_ref[:, BRANCH + XATT:]))
    mbr_ref[...] = branch * gate[:, 0:BRANCH]
    _store_q_by_sequence(q_ref, _dot(h, win_ref[:, BRANCH:BRANCH + XATT]))
    gatt_ref[...] = gate[:, BRANCH:]


def _sample_pre_b(x, win, sre, sim, s5, dvec, wglu, bglu, pre, layer):
    rows = x.shape[0]
    ins = [x, win, sre, sim, *s5, dvec, wglu, bglu]
    shapes = [(rows, BRANCH), (DEC_BATCH * Q_ROWS, XHEAD_DIM), (rows, XATT), sre.shape, sim.shape]
    return pl.pallas_call(
        _sample_pre_b_kernel,
        grid=(1,),
        in_specs=[_const_spec(a.shape) for a in ins] + [_layer_spec(pre.shape, layer)],
        out_specs=[pl.BlockSpec(s, lambda i: (0, 0)) for s in shapes],
        out_shape=[jax.ShapeDtypeStruct(s, F32) for s in shapes],
        scratch_shapes=[pltpu.VMEM((rows // 2, 2 * S5_HALF), F32),
                        pltpu.VMEM((rows // 2, 2 * S5_HALF), F32),
                        pltpu.VMEM((rows, BRANCH), F32)],
        compiler_params=pltpu.CompilerParams(
            dimension_semantics=("arbitrary",), vmem_limit_bytes=VMEM_LIMIT),
        name="sample_pre_b",
    )(*ins, pre)


def _sample_attn_kernel(q_ref, k_ref, v_ref, o_ref):
    bb, rows, _ = q_ref.shape
    cols = k_ref.shape[1]
    row_h = (lax.broadcasted_iota(jnp.int32, (bb * rows, cols), 0) // DEC_SEQ) % N_XHEADS
    col_h = lax.broadcasted_iota(jnp.int32, (bb * rows, cols), 1) % N_XHEADS
    s = jnp.concatenate([_dot_nt(q_ref[j].astype(BF16), k_ref[j].astype(BF16)) for j in range(bb)], axis=0)
    s = jnp.where(row_h == col_h, s * (XHEAD_DIM ** -0.5), -1e30)
    p = jnp.exp(s - jnp.max(s, axis=-1, keepdims=True))
    inv = 1.0 / jnp.sum(p, axis=-1, keepdims=True)
    pb = p.astype(BF16)
    for j in range(bb):
        rs = slice(j * rows, (j + 1) * rows)
        o_ref[j] = _dot(pb[rs], v_ref[j].astype(BF16)) * inv[rs]


def _sample_attn(q, cache_k, cache_v, layer):
    bb = 8
    qs = pl.BlockSpec((bb,) + q.shape[1:], lambda i: (i, 0, 0))
    cs = pl.BlockSpec((None, bb) + cache_k.shape[2:], lambda i: (layer, i, 0, 0))
    return pl.pallas_call(
        _sample_attn_kernel,
        grid=(q.shape[0] // bb,),
        in_specs=[qs, cs, cs],
        out_specs=qs,
        out_shape=jax.ShapeDtypeStruct(q.shape, F32),
        compiler_params=pltpu.CompilerParams(
            dimension_semantics=("arbitrary",), vmem_limit_bytes=VMEM_LIMIT),
        name="sample_attn",
    )(q, cache_k, cache_v)


def _sample_post_kernel(x_ref, mbr_ref, att_ref, gatt_ref, wout_ref, post_ref, o_ref, m_ref):
    nb = DEC_BATCH
    for h in range(N_XHEADS):
        cs = slice(h * XHEAD_DIM, (h + 1) * XHEAD_DIM)
        for t in range(DEC_SEQ):
            rs = slice(t * nb, (t + 1) * nb)
            att = att_ref[pl.ds(h * DEC_SEQ + t, nb, stride=Q_ROWS), :]
            m_ref[rs, cs] = (att * gatt_ref[rs, cs]).astype(BF16)
    out = (_dot(mbr_ref[...].astype(BF16), wout_ref[0:BRANCH, :])
           + _dot(m_ref[...], wout_ref[BRANCH:, :]))
    o_ref[...] = x_ref[...] + _rms(out, post_ref[...])


def _sample_post(x, mbr, att, gatt, wout, post, layer):
    ins = [x, mbr, att, gatt, wout, post]
    return pl.pallas_call(
        _sample_post_kernel,
        grid=(1,),
        in_specs=[_const_spec(a.shape) for a in ins[:4]]
        + [_layer_spec(wout.shape, layer), _layer_spec(post.shape, layer)],
        out_specs=pl.BlockSpec(x.shape, lambda i: (0, 0)),
        out_shape=jax.ShapeDtypeStruct(x.shape, F32),
        scratch_shapes=[pltpu.VMEM((x.shape[0], XATT), BF16)],
        compiler_params=pltpu.CompilerParams(
            dimension_semantics=("arbitrary",), vmem_limit_bytes=VMEM_LIMIT),
        name="sample_post",
    )(*ins)


def _to_bt(a):
    return a.reshape(DEC_SEQ, DEC_BATCH, a.shape[-1]).transpose(1, 0, 2)


def _to_tb(a):
    return a.transpose(1, 0, 2).reshape(DEC_SEQ * DEC_BATCH, a.shape[-1])


def kernel(x_prompt, x_sample, cache_mem_k, cache_mem_v, state_ssm_re, state_ssm_im, mem_prompt,
           w_in_a, ln_v_g, ln_v_b, w_spatial, b_spatial,
           w_in_b, ssm_lambda_re, ssm_lambda_im, ssm_log_dt, ssm_b_re, ssm_b_im, ssm_c_re, ssm_c_im,
           ssm_d, w_glu, b_glu,
           mem_norm_g, w_mem_k, w_mem_v, w_out, pre_norm_g, post_norm_g):
    depth = w_out.shape[0]
    win_a = w_in_a[0].astype(BF16)
    win_b = w_in_b[0].astype(BF16)
    wglu = w_glu[0].astype(BF16)
    wout = w_out.astype(BF16)
    wk = w_mem_k.astype(BF16)
    wv = w_mem_v.astype(BF16)
    pre = pre_norm_g.reshape(depth, 1, D_MODEL)
    post = post_norm_g.reshape(depth, 1, D_MODEL)
    lng = ln_v_g[0].reshape(1, BRANCH)
    lnb = ln_v_b[0].reshape(1, BRANCH)
    bglu = b_glu[0].reshape(1, BRANCH)
    dvec = ssm_d[0].reshape(1, BRANCH)
    bias = jnp.repeat(b_spatial[0].T, A_GDIM, axis=1)
    ws4 = w_spatial[0][:, :DEC_SEQ, :DEC_SEQ]
    wrow = jnp.repeat(ws4.transpose(1, 2, 0).reshape(DEC_SEQ * DEC_SEQ, A_GROUPS), A_GDIM, axis=1)
    brow = bias[:DEC_SEQ]

    s5 = _s5_prep(ssm_lambda_re[0], ssm_lambda_im[0], ssm_log_dt[0],
                  ssm_b_re[0], ssm_b_im[0], ssm_c_re[0], ssm_c_im[0])

    mem = mem_prompt.reshape(BATCH * N_MEM, D_MODEL)
    mk, mv, mkt, mvb = _mem_kv(mem, mem_norm_g.reshape(depth, 1, D_MODEL), wk, wv)

    yp = _layer_a_prompt(x_prompt, mkt, mvb, win_a, lng, lnb, w_spatial[0], bias, wout, pre, post, 0)
    yp, hp_re, hp_im = _layer_b_prompt(yp, mkt, mvb, win_b, s5, dvec, wglu, bglu, wout, pre, post, 1)

    ck = cache_mem_k.reshape(depth, DEC_BATCH, N_MEM * N_XHEADS, XHEAD_DIM)
    cv = cache_mem_v.reshape(depth, DEC_BATCH, N_MEM * N_XHEADS, XHEAD_DIM)
    q_shape = (DEC_BATCH, Q_ROWS, XHEAD_DIM)
    xs = _to_tb(x_sample)
    v_rows, mbr, q, gatt = _sample_pre_a(xs, win_a, lng, lnb, wrow, brow, pre, 0)
    att = _sample_attn(q.reshape(q_shape), ck, cv, 0)
    xs = _sample_post(xs, mbr, att.reshape(q.shape), gatt, wout, post, 0)
    sre = state_ssm_re[0].reshape(DEC_BATCH, SSM_GROUPS * SSM_STATE)
    sim = state_ssm_im[0].reshape(DEC_BATCH, SSM_GROUPS * SSM_STATE)
    mbr, q, gatt, hs_re, hs_im = _sample_pre_b(xs, win_b, sre, sim, s5, dvec, wglu, bglu, pre, 1)
    att = _sample_attn(q.reshape(q_shape), ck, cv, 1)
    xs = _sample_post(xs, mbr, att.reshape(q.shape), gatt, wout, post, 1)

    kv_shape = (depth, BATCH, N_MEM, N_XHEADS, XHEAD_DIM)
    st_p = (1, BATCH, SSM_GROUPS, SSM_STATE)
    st_s = (1, DEC_BATCH, SSM_GROUPS, SSM_STATE)
    return (yp, _to_bt(xs), mk.reshape(kv_shape), mv.reshape(kv_shape),
            hp_re.reshape(st_p), hp_im.reshape(st_p), hs_re.reshape(st_s), hs_im.reshape(st_s),
            _to_bt(v_rows)[None])
```

```python
import functools
import math

import jax
import jax.numpy as jnp
from jax import lax
from jax.experimental import pallas as pl
from jax.experimental.pallas import tpu as pltpu

D_MODEL = 1024
BATCH = 8
SEQ = 2048
DEC_BATCH = 128
DEC_SEQ = 4
BRANCH = 1536
N_XHEADS = 4
XHEAD_DIM = 128
XATT = N_XHEADS * XHEAD_DIM
MIX_WIDTH = BRANCH + XATT
N_MEM = 256
CHUNK = 128
A_GROUPS = 8
A_GDIM = BRANCH // A_GROUPS
SSM_GCH = 16
SSM_GROUPS = 96
SSM_STATE = 64
EPS = 1e-6

LANES = 128
S5_TILE_GROUPS = LANES // SSM_GCH
S5_TILES = SSM_GROUPS // S5_TILE_GROUPS
S5_HALF = S5_TILE_GROUPS * SSM_STATE
VMEM_LIMIT = 60 * 1024 * 1024
A_STREAM_ROWS = 256

F32 = jnp.float32
BF16 = jnp.bfloat16


def _dot(a, b):
    return jnp.dot(a, b, preferred_element_type=F32)


def _rms(x, g):
    return x * lax.rsqrt(jnp.mean(x * x, axis=-1, keepdims=True) + EPS) * g


def _layer_norm(x, g, b):
    mu = jnp.mean(x, axis=-1, keepdims=True)
    xc = x - mu
    var = jnp.mean(xc * xc, axis=-1, keepdims=True)
    return xc * lax.rsqrt(var + EPS) * g + b


def _gelu(x):
    c = math.sqrt(2.0 / math.pi)
    hx = 0.5 * x
    return hx + hx * jnp.tanh(x * (c + (c * 0.044715) * (x * x)))


def _sigmoid(x):
    return 0.5 + 0.5 * jnp.tanh(0.5 * x)


def _silu(x):
    hx = 0.5 * x
    return hx + hx * jnp.tanh(hx)


def _dot_nt(a, b):
    return lax.dot_general(a, b, (((1,), (1,)), ((), ())), preferred_element_type=F32)


def _attend(qs, kts, vs):
    r = qs[0].shape[0]
    s = jnp.concatenate([_dot(q.astype(BF16), kt) for q, kt in zip(qs, kts)], axis=0)
    s = s * (XHEAD_DIM ** -0.5)
    p = jnp.exp(s - jnp.max(s, axis=-1, keepdims=True))
    inv = 1.0 / jnp.sum(p, axis=-1, keepdims=True)
    pb = p.astype(BF16)
    return [_dot(pb[i * r:(i + 1) * r], v) * inv[i * r:(i + 1) * r] for i, v in enumerate(vs)]


def _head(a, h):
    return a[:, h * XHEAD_DIM:(h + 1) * XHEAD_DIM]


def _const_spec(shape):
    nd = len(shape)
    return pl.BlockSpec(shape, lambda *_: (0,) * nd, pipeline_mode=pl.Buffered(1))


def _layer_spec(shape, layer):
    nd = len(shape)
    return pl.BlockSpec((None,) + tuple(shape[1:]), lambda *_: (layer,) + (0,) * (nd - 1),
                        pipeline_mode=pl.Buffered(1))


def _lam_bar(lr, li, ldt):
    dt = jnp.exp(ldt)
    e = jnp.exp(lr * dt)
    return e * jnp.cos(li * dt), e * jnp.sin(li * dt)


def _s5_prep_kernel(*refs):
    for j in range(refs[0].shape[0]):
        _s5_prep_tile(*[r.at[pl.ds(j, 1)] for r in refs])


def _s5_prep_tile(lr_ref, li_ref, ldt_ref, br_ref, bi_ref, cr_ref, ci_ref,
                  a2r_ref, a2i_ref, b2_ref, c2_ref, k2_ref):
    lr = lr_ref[0]
    li = li_ref[0]
    ar, ai = _lam_bar(lr, li, ldt_ref[0])
    a2r = ar * ar - ai * ai
    a2i = 2.0 * ar * ai
    a2r_ref[0] = a2r
    a2i_ref[0] = a2i
    nr = ar - 1.0
    den = lr * lr + li * li
    kr = (nr * lr + ai * li) / den
    ki = (ai * lr - nr * li) / den
    shape = (LANES, S5_HALF)
    own = (lax.broadcasted_iota(jnp.int32, shape, 0) // SSM_GCH
           == lax.broadcasted_iota(jnp.int32, shape, 1) // SSM_STATE)

    def own_blocks(ref):
        return jnp.where(own, jnp.concatenate([ref[0]] * S5_TILE_GROUPS, axis=0), 0.0)

    br, bi, cr, ci = own_blocks(br_ref), own_blocks(bi_ref), own_blocks(cr_ref), own_blocks(ci_ref)
    bbr = kr * br - ki * bi
    bbi = kr * bi + ki * br
    b_tok1 = jnp.concatenate([bbr, bbi], axis=1).astype(BF16)
    b_tok0 = jnp.concatenate([ar * bbr - ai * bbi, ar * bbi + ai * bbr], axis=1).astype(BF16)
    b2_ref[0, 0:LANES, :] = b_tok0
    b2_ref[0, LANES:, :] = b_tok1
    c_tok0 = jnp.concatenate([cr * ar - ci * ai, -(cr * ai + ci * ar)], axis=1)
    c_tok1 = jnp.concatenate([cr * a2r - ci * a2i, -(cr * a2i + ci * a2r)], axis=1)
    c2_ref[0, :, 0:LANES] = c_tok0.T.astype(BF16)
    c2_ref[0, :, LANES:] = c_tok1.T.astype(BF16)
    c_plain = jnp.concatenate([cr, -ci], axis=1).astype(BF16)
    k0 = _dot_nt(b_tok1, c_plain).astype(BF16)
    k1 = _dot_nt(b_tok0, c_plain).astype(BF16)
    k2_ref[0, 0:LANES, 0:LANES] = k0
    k2_ref[0, 0:LANES, LANES:] = k1
    k2_ref[0, LANES:, 0:LANES] = jnp.zeros_like(k0)
    k2_ref[0, LANES:, LANES:] = k0


def _s5_prep(lam_re, lam_im, log_dt, b_re, b_im, c_re, c_im):
    nt, tg = S5_TILES, S5_TILE_GROUPS
    lr = lam_re.reshape(nt, 1, S5_HALF)
    li = lam_im.reshape(nt, 1, S5_HALF)
    ldt = jnp.broadcast_to(log_dt[:, None], (SSM_GROUPS, SSM_STATE)).reshape(nt, 1, S5_HALF)

    def b_cp(b):
        return b.reshape(nt, tg, SSM_STATE, SSM_GCH).transpose(0, 3, 1, 2).reshape(nt, SSM_GCH, S5_HALF)

    def c_cp(c):
        return c.reshape(nt, tg, SSM_GCH, SSM_STATE).transpose(0, 2, 1, 3).reshape(nt, SSM_GCH, S5_HALF)

    per_step = 3

    def spec(*shape):
        return pl.BlockSpec((per_step,) + shape, lambda j: (j, 0, 0))

    vec, chan = spec(1, S5_HALF), spec(SSM_GCH, S5_HALF)
    out_shapes = [(1, S5_HALF), (1, S5_HALF), (2 * LANES, 2 * S5_HALF), (2 * S5_HALF, 2 * LANES),
                  (2 * LANES, 2 * LANES)]
    out_dtypes = [F32, F32, BF16, BF16, BF16]
    return pl.pallas_call(
        _s5_prep_kernel,
        grid=(nt // per_step,),
        in_specs=[vec, vec, vec, chan, chan, chan, chan],
        out_specs=[spec(*s) for s in out_shapes],
        out_shape=[jax.ShapeDtypeStruct((nt,) + s, d) for s, d in zip(out_shapes, out_dtypes)],
        name="s5_prep",
    )(lr, li, ldt, b_cp(b_re), b_cp(b_im), c_cp(c_re), c_cp(c_im))


def _mem_kv_kernel(mem_ref, g_ref, wk_ref, wv_ref, k_ref, v_ref, kt_ref, vb_ref):
    mem = mem_ref[...]
    scaled = mem * lax.rsqrt(jnp.mean(mem * mem, axis=-1, keepdims=True) + EPS)
    for i in range(g_ref.shape[0]):
        m = (scaled * g_ref[i]).astype(BF16)
        k = _dot(m, wk_ref[i])
        v = _dot(m, wv_ref[i])
        vb_ref[i] = v.astype(BF16)
        for b in range(k_ref.shape[1]):
            kb = k[b * N_MEM:(b + 1) * N_MEM]
            vb = v[b * N_MEM:(b + 1) * N_MEM]
            kt_ref[i, b] = kb.T.astype(BF16)
            for h in range(N_XHEADS):
                rows = pl.ds(h, N_MEM, stride=N_XHEADS)
                k_ref[i, b, rows, :] = _head(kb, h)
                v_ref[i, b, rows, :] = _head(vb, h)


def _mem_kv(mem, g, wk, wv):
    rows = mem.shape[0]
    depth = g.shape[0]
    nb = 2
    tm = nb * N_MEM
    return pl.pallas_call(
        _mem_kv_kernel,
        grid=(rows // tm,),
        in_specs=[pl.BlockSpec((tm, D_MODEL), lambda r: (r, 0)),
                  _const_spec(g.shape), _const_spec(wk.shape), _const_spec(wv.shape)],
        out_specs=[pl.BlockSpec((depth, nb, N_MEM * N_XHEADS, XHEAD_DIM), lambda r: (0, r, 0, 0))] * 2
        + [pl.BlockSpec((depth, nb, XATT, N_MEM), lambda r: (0, r, 0, 0)),
           pl.BlockSpec((depth, tm, XATT), lambda r: (0, r, 0))],
        out_shape=[jax.ShapeDtypeStruct((depth, rows // N_MEM, N_MEM * N_XHEADS, XHEAD_DIM), F32)] * 2
        + [jax.ShapeDtypeStruct((depth, rows // N_MEM, XATT, N_MEM), BF16),
           jax.ShapeDtypeStruct((depth, rows, XATT), BF16)],
        compiler_params=pltpu.CompilerParams(
            dimension_semantics=("arbitrary",), vmem_limit_bytes=VMEM_LIMIT),
        name="mem_kv",
    )(mem, g, wk, wv)


def _masked_spatial(ws_ref):
    row = lax.broadcasted_iota(jnp.int32, (CHUNK, CHUNK), 0)
    col = lax.broadcasted_iota(jnp.int32, (CHUNK, CHUNK), 1)
    return [jnp.where(row >= col, ws_ref[g], 0.0).astype(BF16) for g in range(A_GROUPS)]


def _gate_chunk(wm, vc, pair):
    base = pair * 2 * A_GDIM
    even = _dot(wm[2 * pair], vc[:, base:base + 2 * LANES])
    odd = _dot(wm[2 * pair + 1], vc[:, base + LANES:base + 3 * LANES])
    lane = lax.broadcasted_iota(jnp.int32, (CHUNK, LANES), 1)
    mid = jnp.where(lane < A_GDIM - LANES, even[:, LANES:], odd[:, :LANES])
    return jnp.concatenate([even[:, :LANES], mid, odd[:, LANES:]], axis=1)


def _s5_tile(u, hr, hi, a2r, a2i, b2, c2, k2, bu_ref, hs_ref, steps, nb):
    pairs = steps // 2
    u3 = u.reshape(pairs, 2 * nb, LANES)
    u2 = jnp.concatenate([u3[:, 0:nb, :].reshape(pairs * nb, LANES),
                          u3[:, nb:, :].reshape(pairs * nb, LANES)], axis=1).astype(BF16)
    bu_ref[...] = _dot(u2, b2)
    a2r = jnp.broadcast_to(a2r, (nb, S5_HALF))
    a2i = jnp.broadcast_to(a2i, (nb, S5_HALF))
    for k in range(pairs):
        rows = slice(k * nb, (k + 1) * nb)
        hs_ref[rows, 0:S5_HALF] = hr
        hs_ref[rows, S5_HALF:] = hi
        hr, hi = (a2r * hr - a2i * hi + bu_ref[rows, 0:S5_HALF],
                  a2r * hi + a2i * hr + bu_ref[rows, S5_HALF:])
    y2 = _dot(hs_ref[...].astype(BF16), c2) + _dot(u2, k2)
    return y2, hr, hi


def _layer_a_prompt_kernel(x_ref, kt_ref, v_ref, win_ref, lng_ref, lnb_ref, ws_ref, bias_ref,
                           wout_ref, pre_ref, post_ref, o_ref, m_ref, g_ref):
    wm = _masked_spatial(ws_ref)
    heads = range(N_XHEADS)
    n_streams = x_ref.shape[0] // A_STREAM_ROWS

    def project(s):
        x = x_ref[pl.ds(s * A_STREAM_ROWS, A_STREAM_ROWS), :]
        h = _rms(x, pre_ref[...]).astype(BF16)
        v = _layer_norm(_gelu(_dot(h, win_ref[:, BRANCH:2 * BRANCH])), lng_ref[...], lnb_ref[...])
        u = _gelu(_dot(h, win_ref[:, 0:BRANCH]))
        gate = _silu(_dot(h, win_ref[:, 2 * BRANCH + XATT:]))
        g_ref[pl.ds(s * A_STREAM_ROWS, A_STREAM_ROWS), 0:BRANCH] = u * gate[:, 0:BRANCH]
        g_ref[pl.ds(s * A_STREAM_ROWS, A_STREAM_ROWS), BRANCH:] = gate[:, BRANCH:]
        q = _dot(h, win_ref[:, 2 * BRANCH:2 * BRANCH + XATT])
        return x, v.astype(BF16), q

    def mix_and_emit(s, x, vb, q):
        m = m_ref.at[pl.ds(s * A_STREAM_ROWS, A_STREAM_ROWS)]
        g = g_ref.at[pl.ds(s * A_STREAM_ROWS, A_STREAM_ROWS)]
        for c in range(A_STREAM_ROWS // CHUNK):
            rs = slice(c * CHUNK, (c + 1) * CHUNK)
            vc = vb[rs]
            for pair in range(A_GROUPS // 2):
                cs = slice(pair * 2 * A_GDIM, (pair + 1) * 2 * A_GDIM)
                mixed = _gate_chunk(wm, vc, pair) + bias_ref[:, cs]
                m[rs, cs] = (g[rs, cs] * mixed).astype(BF16)
        kv = v_ref[...]
        att = _attend([_head(q, h) for h in heads],
                      [kt_ref[h * XHEAD_DIM:(h + 1) * XHEAD_DIM, :] for h in heads],
                      [_head(kv, h) for h in heads])
        for h in heads:
            cs = slice(BRANCH + h * XHEAD_DIM, BRANCH + (h + 1) * XHEAD_DIM)
            m[:, cs] = (att[h] * g[:, cs]).astype(BF16)
        out = _dot(m[...], wout_ref[...])
        o_ref[pl.ds(s * A_STREAM_ROWS, A_STREAM_ROWS), :] = x + _rms(out, post_ref[...])

    pending = project(0)
    for s in range(n_streams):
        following = project(s + 1) if s + 1 < n_streams else None
        mix_and_emit(s, *pending)
        pending = following


def _layer_a_prompt(x, kb, vb, win, lng, lnb, ws, bias, wout, pre, post, layer):
    tm = 512
    nb, seq, _ = x.shape
    xs = pl.BlockSpec((None, tm, D_MODEL), lambda b, t: (b, t, 0))
    kts = pl.BlockSpec((None, None, XATT, N_MEM), lambda b, t: (layer, b, 0, 0))
    vs = pl.BlockSpec((None, N_MEM, XATT), lambda b, t: (layer, b, 0))
    return pl.pallas_call(
        _layer_a_prompt_kernel,
        grid=(nb, seq // tm),
        in_specs=[xs, kts, vs, _const_spec(win.shape), _const_spec(lng.shape), _const_spec(lnb.shape),
                  _const_spec(ws.shape), _const_spec(bias.shape), _layer_spec(wout.shape, layer),
                  _layer_spec(pre.shape, layer), _layer_spec(post.shape, layer)],
        out_specs=xs,
        out_shape=jax.ShapeDtypeStruct(x.shape, F32),
        scratch_shapes=[pltpu.VMEM((tm, MIX_WIDTH), BF16), pltpu.VMEM((tm, MIX_WIDTH), F32)],
        compiler_params=pltpu.CompilerParams(
            dimension_semantics=("arbitrary", "arbitrary"), vmem_limit_bytes=VMEM_LIMIT),
        name="layer_a_prompt",
    )(x, kb, vb, win, lng, lnb, ws, bias, wout, pre, post)


def _layer_b_prompt_kernel(x_ref, kt_ref, v_ref, win_ref, a2r_ref, a2i_ref, b2_ref, c2_ref, k2_ref, d_ref,
                           wglu_ref, bglu_ref, wout_ref, pre_ref, post_ref,
                           o_ref, hre_ref, him_ref,
                           ut_ref, bu_ref, hs_ref, yt_ref, y_ref, m_ref):
    nb, tc, _ = x_ref.shape
    tm = nb * tc
    half = tc // 2

    @pl.when(pl.program_id(0) == 0)
    def _():
        hre_ref[...] = jnp.zeros_like(hre_ref)
        him_ref[...] = jnp.zeros_like(him_ref)

    x = x_ref[...].reshape(tm, D_MODEL)
    h = _rms(x, pre_ref[...]).astype(BF16)
    u = _dot(h, win_ref[:, 0:BRANCH])
    for j in range(S5_TILES):
        for b in range(nb):
            ut_ref[j, pl.ds(b, tc, stride=nb), :] = u[b * tc:(b + 1) * tc, j * LANES:(j + 1) * LANES]
    for j in range(S5_TILES):
        ss = slice(j * S5_HALF, (j + 1) * S5_HALF)
        par = j % 2
        y2, hr, hi = _s5_tile(ut_ref[j], hre_ref[:, ss], him_ref[:, ss], a2r_ref[j], a2i_ref[j],
                              b2_ref[j], c2_ref[j], k2_ref[j], bu_ref.at[par], hs_ref.at[par], tc, nb)
        hre_ref[:, ss] = hr
        him_ref[:, ss] = hi
        for tp in range(2):
            yt_ref[par, tp] = y2[:, tp * LANES:(tp + 1) * LANES]
            for b in range(nb):
                y_ref[j, pl.ds(b * tc + tp, half, stride=2), :] = yt_ref[par, tp, pl.ds(b, half, stride=nb), :]
    y = _gelu(jnp.concatenate([y_ref[j] for j in range(S5_TILES)], axis=1) + d_ref[...] * u)
    branch = y * _sigmoid(_dot(y.astype(BF16), wglu_ref[...]) + bglu_ref[...])
    gate = _silu(_dot(h, win_ref[:, BRANCH + XATT:]))
    m_ref[:, 0:BRANCH] = (branch * gate[:, 0:BRANCH]).astype(BF16)
    q = _dot(h, win_ref[:, BRANCH:BRANCH + XATT])
    pieces = [(b, h) for b in range(nb) for h in range(N_XHEADS)]
    att = _attend([_head(q[b * tc:(b + 1) * tc], h) for b, h in pieces],
                  [kt_ref[b, h * XHEAD_DIM:(h + 1) * XHEAD_DIM, :] for b, h in pieces],
                  [_head(v_ref[b * N_MEM:(b + 1) * N_MEM, :], h) for b, h in pieces])
    for (b, h), a in zip(pieces, att):
        rs = slice(b * tc, (b + 1) * tc)
        cs = slice(BRANCH + h * XHEAD_DIM, BRANCH + (h + 1) * XHEAD_DIM)
        m_ref[rs, cs] = (a * gate[rs, cs]).astype(BF16)
    out = _dot(m_ref[...], wout_ref[...])
    o_ref[...] = (x + _rms(out, post_ref[...])).reshape(nb, tc, D_MODEL)


def _layer_b_prompt(x, kb, vb, win, s5, dvec, wglu, bglu, wout, pre, post, layer):
    tc = 64
    nb, seq, _ = x.shape
    tm = nb * tc
    xs = pl.BlockSpec((nb, tc, D_MODEL), lambda t: (0, t, 0))
    hs = pl.BlockSpec((nb, SSM_GROUPS * SSM_STATE), lambda t: (0, 0))
    ins = [kb, vb, win, *s5, dvec, wglu, bglu, wout, pre, post]
    stacked = [True, True] + [False] * (len(s5) + 4) + [True, True, True]
    return pl.pallas_call(
        _layer_b_prompt_kernel,
        grid=(seq // tc,),
        in_specs=[xs] + [_layer_spec(a.shape, layer) if st else _const_spec(a.shape)
                         for a, st in zip(ins, stacked)],
        out_specs=[xs, hs, hs],
        out_shape=[jax.ShapeDtypeStruct(x.shape, F32),
                   jax.ShapeDtypeStruct((nb, SSM_GROUPS * SSM_STATE), F32),
                   jax.ShapeDtypeStruct((nb, SSM_GROUPS * SSM_STATE), F32)],
        scratch_shapes=[pltpu.VMEM((S5_TILES, tm, LANES), F32),
                        pltpu.VMEM((2, tm // 2, 2 * S5_HALF), F32),
                        pltpu.VMEM((2, tm // 2, 2 * S5_HALF), F32),
                        pltpu.VMEM((2, 2, tm // 2, LANES), F32),
                        pltpu.VMEM((S5_TILES, tm, LANES), F32),
                        pltpu.VMEM((tm, MIX_WIDTH), BF16)],
        compiler_params=pltpu.CompilerParams(
            dimension_semantics=("arbitrary",), vmem_limit_bytes=VMEM_LIMIT),
        name="layer_b_prompt",
    )(x, *ins)


Q_ROWS = N_XHEADS * DEC_SEQ


def _store_q_by_sequence(q_ref, q):
    for h in range(N_XHEADS):
        for t in range(DEC_SEQ):
            q_ref[pl.ds(h * DEC_SEQ + t, DEC_BATCH, stride=Q_ROWS), :] = _head(q[t * DEC_BATCH:(t + 1) * DEC_BATCH], h)


def _sample_pre_a_kernel(x_ref, win_ref, lng_ref, lnb_ref, wrow_ref, brow_ref, pre_ref,
                         v_ref, mbr_ref, q_ref, gatt_ref):
    nb = DEC_BATCH
    h = _rms(x_ref[...], pre_ref[...]).astype(BF16)
    gate = _silu(_dot(h, win_ref[:, 2 * BRANCH + XATT:]))
    u = _gelu(_dot(h, win_ref[:, 0:BRANCH]))
    v = _layer_norm(_gelu(_dot(h, win_ref[:, BRANCH:2 * BRANCH])), lng_ref[...], lnb_ref[...])
    v_ref[...] = v
    for t in range(DEC_SEQ):
        rs = slice(t * nb, (t + 1) * nb)
        mixed = brow_ref[t:t + 1, :]
        for s in range(t + 1):
            mixed = mixed + wrow_ref[t * DEC_SEQ + s:t * DEC_SEQ + s + 1, :] * v[s * nb:(s + 1) * nb]
        mbr_ref[rs, :] = u[rs] * mixed * gate[rs, 0:BRANCH]
    _store_q_by_sequence(q_ref, _dot(h, win_ref[:, 2 * BRANCH:2 * BRANCH + XATT]))
    gatt_ref[...] = gate[:, BRANCH:]


def _sample_pre_a(x, win, lng, lnb, wrow, brow, pre, layer):
    rows = x.shape[0]
    ins = [x, win, lng, lnb, wrow, brow, pre]
    shapes = [(rows, BRANCH), (rows, BRANCH), (DEC_BATCH * Q_ROWS, XHEAD_DIM), (rows, XATT)]
    return pl.pallas_call(
        _sample_pre_a_kernel,
        grid=(1,),
        in_specs=[_const_spec(a.shape) for a in ins[:-1]] + [_layer_spec(pre.shape, layer)],
        out_specs=[pl.BlockSpec(s, lambda i: (0, 0)) for s in shapes],
        out_shape=[jax.ShapeDtypeStruct(s, F32) for s in shapes],
        compiler_params=pltpu.CompilerParams(
            dimension_semantics=("arbitrary",), vmem_limit_bytes=VMEM_LIMIT),
        name="sample_pre_a",
    )(*ins)


def _sample_pre_b_kernel(x_ref, win_ref, sre_ref, sim_ref, a2r_ref, a2i_ref, b2_ref, c2_ref, k2_ref, d_ref,
                         wglu_ref, bglu_ref, pre_ref,
                         mbr_ref, q_ref, gatt_ref, hre_ref, him_ref,
                         bu_ref, hs_ref, y_ref):
    nb = DEC_BATCH
    h = _rms(x_ref[...], pre_ref[...]).astype(BF16)
    u = _dot(h, win_ref[:, 0:BRANCH])
    for j in range(S5_TILES):
        ss = slice(j * S5_HALF, (j + 1) * S5_HALF)
        ls = slice(j * LANES, (j + 1) * LANES)
        y2, hr, hi = _s5_tile(u[:, ls], sre_ref[:, ss], sim_ref[:, ss], a2r_ref[j], a2i_ref[j],
                              b2_ref[j], c2_ref[j], k2_ref[j], bu_ref, hs_ref, DEC_SEQ, nb)
        hre_ref[:, ss] = hr
        him_ref[:, ss] = hi
        for pair in range(DEC_SEQ // 2):
            for tp in range(2):
                t = 2 * pair + tp
                y_ref[t * nb:(t + 1) * nb, ls] = y2[pair * nb:(pair + 1) * nb, tp * LANES:(tp + 1) * LANES]
    y = _gelu(y_ref[...] + d_ref[...] * u)
    branch = y * _sigmoid(_dot(y.astype(BF16), wglu_ref[...]) + bglu_ref[...])
    gate = _silu(_dot(h, win_ref[:, BRANCH + XATT:]))
    mbr_ref[...] = branch * gate[:, 0:BRANCH]
    _store_q_by_sequence(q_ref, _dot(h, win_ref[:, BRANCH:BRANCH + XATT]))
    gatt_ref[...] = gate[:, BRANCH:]


def _sample_pre_b(x, win, sre, sim, s5, dvec, wglu, bglu, pre, layer):
    rows = x.shape[0]
    ins = [x, win, sre, sim, *s5, dvec, wglu, bglu]
    shapes = [(rows, BRANCH), (DEC_BATCH * Q_ROWS, XHEAD_DIM), (rows, XATT), sre.shape, sim.shape]
    return pl.pallas_call(
        _sample_pre_b_kernel,
        grid=(1,),
        in_specs=[_const_spec(a.shape) for a in ins] + [_layer_spec(pre.shape, layer)],
        out_specs=[pl.BlockSpec(s, lambda i: (0, 0)) for s in shapes],
        out_shape=[jax.ShapeDtypeStruct(s, F32) for s in shapes],
        scratch_shapes=[pltpu.VMEM((rows // 2, 2 * S5_HALF), F32),
                        pltpu.VMEM((rows // 2, 2 * S5_HALF), F32),
                        pltpu.VMEM((rows, BRANCH), F32)],
        compiler_params=pltpu.CompilerParams(
            dimension_semantics=("arbitrary",), vmem_limit_bytes=VMEM_LIMIT),
        name="sample_pre_b",
    )(*ins, pre)


def _sample_attn_kernel(q_ref, k_ref, v_ref, o_ref):
    bb, rows, _ = q_ref.shape
    cols = k_ref.shape[1]
    row_h = (lax.broadcasted_iota(jnp.int32, (bb * rows, cols), 0) // DEC_SEQ) % N_XHEADS
    col_h = lax.broadcasted_iota(jnp.int32, (bb * rows, cols), 1) % N_XHEADS
    s = jnp.concatenate([_dot_nt(q_ref[j].astype(BF16), k_ref[j].astype(BF16)) for j in range(bb)], axis=0)
    s = jnp.where(row_h == col_h, s * (XHEAD_DIM ** -0.5), -1e30)
    p = jnp.exp(s - jnp.max(s, axis=-1, keepdims=True))
    inv = 1.0 / jnp.sum(p, axis=-1, keepdims=True)
    pb = p.astype(BF16)
    for j in range(bb):
        rs = slice(j * rows, (j + 1) * rows)
        o_ref[j] = _dot(pb[rs], v_ref[j].astype(BF16)) * inv[rs]


def _sample_attn(q, cache_k, cache_v, layer):
    bb = 8
    qs = pl.BlockSpec((bb,) + q.shape[1:], lambda i: (i, 0, 0))
    cs = pl.BlockSpec((None, bb) + cache_k.shape[2:], lambda i: (layer, i, 0, 0))
    return pl.pallas_call(
        _sample_attn_kernel,
        grid=(q.shape[0] // bb,),
        in_specs=[qs, cs, cs],
        out_specs=qs,
        out_shape=jax.ShapeDtypeStruct(q.shape, F32),
        compiler_params=pltpu.CompilerParams(
            dimension_semantics=("arbitrary",), vmem_limit_bytes=VMEM_LIMIT),
        name="sample_attn",
    )(q, cache_k, cache_v)


def _sample_post_kernel(x_ref, mbr_ref, att_ref, gatt_ref, wout_ref, post_ref, o_ref, m_ref):
    nb = DEC_BATCH
    for h in range(N_XHEADS):
        cs = slice(h * XHEAD_DIM, (h + 1) * XHEAD_DIM)
        for t in range(DEC_SEQ):
            rs = slice(t * nb, (t + 1) * nb)
            att = att_ref[pl.ds(h * DEC_SEQ + t, nb, stride=Q_ROWS), :]
            m_ref[rs, cs] = (att * gatt_ref[rs, cs]).astype(BF16)
    out = (_dot(mbr_ref[...].astype(BF16), wout_ref[0:BRANCH, :])
           + _dot(m_ref[...], wout_ref[BRANCH:, :]))
    o_ref[...] = x_ref[...] + _rms(out, post_ref[...])


def _sample_post(x, mbr, att, gatt, wout, post, layer):
    ins = [x, mbr, att, gatt, wout, post]
    return pl.pallas_call(
        _sample_post_kernel,
        grid=(1,),
        in_specs=[_const_spec(a.shape) for a in ins[:4]]
        + [_layer_spec(wout.shape, layer), _layer_spec(post.shape, layer)],
        out_specs=pl.BlockSpec(x.shape, lambda i: (0, 0)),
        out_shape=jax.ShapeDtypeStruct(x.shape, F32),
        scratch_shapes=[pltpu.VMEM((x.shape[0], XATT), BF16)],
        compiler_params=pltpu.CompilerParams(
            dimension_semantics=("arbitrary",), vmem_limit_bytes=VMEM_LIMIT),
        name="sample_post",
    )(*ins)


def _to_bt(a):
    return a.reshape(DEC_SEQ, DEC_BATCH, a.shape[-1]).transpose(1, 0, 2)


def _to_tb(a):
    return a.transpose(1, 0, 2).reshape(DEC_SEQ * DEC_BATCH, a.shape[-1])


def kernel(x_prompt, x_sample, cache_mem_k, cache_mem_v, state_ssm_re, state_ssm_im, mem_prompt,
           w_in_a, ln_v_g, ln_v_b, w_spatial, b_spatial,
           w_in_b, ssm_lambda_re, ssm_lambda_im, ssm_log_dt, ssm_b_re, ssm_b_im, ssm_c_re, ssm_c_im,
           ssm_d, w_glu, b_glu,
           mem_norm_g, w_mem_k, w_mem_v, w_out, pre_norm_g, post_norm_g):
    depth = w_out.shape[0]
    win_a = w_in_a[0].astype(BF16)
    win_b = w_in_b[0].astype(BF16)
    wglu = w_glu[0].astype(BF16)
    wout = w_out.astype(BF16)
    wk = w_mem_k.astype(BF16)
    wv = w_mem_v.astype(BF16)
    pre = pre_norm_g.reshape(depth, 1, D_MODEL)
    post = post_norm_g.reshape(depth, 1, D_MODEL)
    lng = ln_v_g[0].reshape(1, BRANCH)
    lnb = ln_v_b[0].reshape(1, BRANCH)
    bglu = b_glu[0].reshape(1, BRANCH)
    dvec = ssm_d[0].reshape(1, BRANCH)
    bias = jnp.repeat(b_spatial[0].T, A_GDIM, axis=1)
    ws4 = w_spatial[0][:, :DEC_SEQ, :DEC_SEQ]
    wrow = jnp.repeat(ws4.transpose(1, 2, 0).reshape(DEC_SEQ * DEC_SEQ, A_GROUPS), A_GDIM, axis=1)
    brow = bias[:DEC_SEQ]

    s5 = _s5_prep(ssm_lambda_re[0], ssm_lambda_im[0], ssm_log_dt[0],
                  ssm_b_re[0], ssm_b_im[0], ssm_c_re[0], ssm_c_im[0])

    mem = mem_prompt.reshape(BATCH * N_MEM, D_MODEL)
    mk, mv, mkt, mvb = _mem_kv(mem, mem_norm_g.reshape(depth, 1, D_MODEL), wk, wv)

    yp = _layer_a_prompt(x_prompt, mkt, mvb, win_a, lng, lnb, w_spatial[0], bias, wout, pre, post, 0)
    yp, hp_re, hp_im = _layer_b_prompt(yp, mkt, mvb, win_b, s5, dvec, wglu, bglu, wout, pre, post, 1)

    ck = cache_mem_k.reshape(depth, DEC_BATCH, N_MEM * N_XHEADS, XHEAD_DIM)
    cv = cache_mem_v.reshape(depth, DEC_BATCH, N_MEM * N_XHEADS, XHEAD_DIM)
    q_shape = (DEC_BATCH, Q_ROWS, XHEAD_DIM)
    xs = _to_tb(x_sample)
    v_rows, mbr, q, gatt = _sample_pre_a(xs, win_a, lng, lnb, wrow, brow, pre, 0)
    att = _sample_attn(q.reshape(q_shape), ck, cv, 0)
    xs = _sample_post(xs, mbr, att.reshape(q.shape), gatt, wout, post, 0)
    sre = state_ssm_re[0].reshape(DEC_BATCH, SSM_GROUPS * SSM_STATE)
    sim = state_ssm_im[0].reshape(DEC_BATCH, SSM_GROUPS * SSM_STATE)
    mbr, q, gatt, hs_re, hs_im = _sample_pre_b(xs, win_b, sre, sim, s5, dvec, wglu, bglu, pre, 1)
    att = _sample_attn(q.reshape(q_shape), ck, cv, 1)
    xs = _sample_post(xs, mbr, att.reshape(q.shape), gatt, wout, post, 1)

    kv_shape = (depth, BATCH, N_MEM, N_XHEADS, XHEAD_DIM)
    st_p = (1, BATCH, SSM_GROUPS, SSM_STATE)
    st_s = (1, DEC_BATCH, SSM_GROUPS, SSM_STATE)
    return (yp, _to_bt(xs), mk.reshape(kv_shape), mv.reshape(kv_shape),
            hp_re.reshape(st_p), hp_im.reshape(st_p), hs_re.reshape(st_s), hs_im.reshape(st_s),
            _to_bt(v_rows)[None])
```

```python
import functools
import math

import jax
import jax.numpy as jnp
from jax import lax
from jax.experimental import pallas as pl
from jax.experimental.pallas import tpu as pltpu

D_MODEL = 1024
BATCH = 8
SEQ = 2048
DEC_BATCH = 128
DEC_SEQ = 4
BRANCH = 1536
N_XHEADS = 4
XHEAD_DIM = 128
XATT = N_XHEADS * XHEAD_DIM
MIX_WIDTH = BRANCH + XATT
N_MEM = 256
CHUNK = 128
A_GROUPS = 8
A_GDIM = BRANCH // A_GROUPS
SSM_GCH = 16
SSM_GROUPS = 96
SSM_STATE = 64
EPS = 1e-6

LANES = 128
S5_TILE_GROUPS = LANES // SSM_GCH
S5_TILES = SSM_GROUPS // S5_TILE_GROUPS
S5_HALF = S5_TILE_GROUPS * SSM_STATE
VMEM_LIMIT = 60 * 1024 * 1024
A_STREAM_ROWS = 256

F32 = jnp.float32
BF16 = jnp.bfloat16


def _dot(a, b):
    return jnp.dot(a, b, preferred_element_type=F32)


def _rms(x, g):
    return x * lax.rsqrt(jnp.mean(x * x, axis=-1, keepdims=True) + EPS) * g


def _layer_norm(x, g, b):
    mu = jnp.mean(x, axis=-1, keepdims=True)
    xc = x - mu
    var = jnp.mean(xc * xc, axis=-1, keepdims=True)
    return xc * lax.rsqrt(var + EPS) * g + b


def _gelu(x):
    c = math.sqrt(2.0 / math.pi)
    hx = 0.5 * x
    return hx + hx * jnp.tanh(x * (c + (c * 0.044715) * (x * x)))


def _sigmoid(x):
    return 0.5 + 0.5 * jnp.tanh(0.5 * x)


def _silu(x):
    hx = 0.5 * x
    return hx + hx * jnp.tanh(hx)


def _dot_nt(a, b):
    return lax.dot_general(a, b, (((1,), (1,)), ((), ())), preferred_element_type=F32)


def _attend(qs, kts, vs):
    r = qs[0].shape[0]
    s = jnp.concatenate([_dot(q.astype(BF16), kt) for q, kt in zip(qs, kts)], axis=0)
    s = s * (XHEAD_DIM ** -0.5)
    p = jnp.exp(s - jnp.max(s, axis=-1, keepdims=True))
    inv = 1.0 / jnp.sum(p, axis=-1, keepdims=True)
    pb = p.astype(BF16)
    return [_dot(pb[i * r:(i + 1) * r], v) * inv[i * r:(i + 1) * r] for i, v in enumerate(vs)]


def _head(a, h):
    return a[:, h * XHEAD_DIM:(h + 1) * XHEAD_DIM]


def _const_spec(shape):
    nd = len(shape)
    return pl.BlockSpec(shape, lambda *_: (0,) * nd, pipeline_mode=pl.Buffered(1))


def _layer_spec(shape, layer):
    nd = len(shape)
    return pl.BlockSpec((None,) + tuple(shape[1:]), lambda *_: (layer,) + (0,) * (nd - 1),
                        pipeline_mode=pl.Buffered(1))


def _lam_bar(lr, li, ldt):
    dt = jnp.exp(ldt)
    e = jnp.exp(lr * dt)
    return e * jnp.cos(li * dt), e * jnp.sin(li * dt)


def _s5_prep_kernel(*refs):
    for j in range(refs[0].shape[0]):
        _s5_prep_tile(*[r.at[pl.ds(j, 1)] for r in refs])


def _s5_prep_tile(lr_ref, li_ref, ldt_ref, br_ref, bi_ref, cr_ref, ci_ref,
                  a2r_ref, a2i_ref, b2_ref, c2_ref, k2_ref):
    lr = lr_ref[0]
    li = li_ref[0]
    ar, ai = _lam_bar(lr, li, ldt_ref[0])
    a2r = ar * ar - ai * ai
    a2i = 2.0 * ar * ai
    a2r_ref[0] = a2r
    a2i_ref[0] = a2i
    nr = ar - 1.0
    den = lr * lr + li * li
    kr = (nr * lr + ai * li) / den
    ki = (ai * lr - nr * li) / den
    shape = (LANES, S5_HALF)
    own = (lax.broadcasted_iota(jnp.int32, shape, 0) // SSM_GCH
           == lax.broadcasted_iota(jnp.int32, shape, 1) // SSM_STATE)

    def own_blocks(ref):
        return jnp.where(own, jnp.concatenate([ref[0]] * S5_TILE_GROUPS, axis=0), 0.0)

    br, bi, cr, ci = own_blocks(br_ref), own_blocks(bi_ref), own_blocks(cr_ref), own_blocks(ci_ref)
    bbr = kr * br - ki * bi
    bbi = kr * bi + ki * br
    b_tok1 = jnp.concatenate([bbr, bbi], axis=1).astype(BF16)
    b_tok0 = jnp.concatenate([ar * bbr - ai * bbi, ar * bbi + ai * bbr], axis=1).astype(BF16)
    b2_ref[0, 0:LANES, :] = b_tok0
    b2_ref[0, LANES:, :] = b_tok1
    c_tok0 = jnp.concatenate([cr * ar - ci * ai, -(cr * ai + ci * ar)], axis=1)
    c_tok1 = jnp.concatenate([cr * a2r - ci * a2i, -(cr * a2i + ci * a2r)], axis=1)
    c2_ref[0, :, 0:LANES] = c_tok0.T.astype(BF16)
    c2_ref[0, :, LANES:] = c_tok1.T.astype(BF16)
    c_plain = jnp.concatenate([cr, -ci], axis=1).astype(BF16)
    k0 = _dot_nt(b_tok1, c_plain).astype(BF16)
    k1 = _dot_nt(b_tok0, c_plain).astype(BF16)
    k2_ref[0, 0:LANES, 0:LANES] = k0
    k2_ref[0, 0:LANES, LANES:] = k1
    k2_ref[0, LANES:, 0:LANES] = jnp.zeros_like(k0)
    k2_ref[0, LANES:, LANES:] = k0


def _s5_prep(lam_re, lam_im, log_dt, b_re, b_im, c_re, c_im):
    nt, tg = S5_TILES, S5_TILE_GROUPS
    lr = lam_re.reshape(nt, 1, S5_HALF)
    li = lam_im.reshape(nt, 1, S5_HALF)
    ldt = jnp.broadcast_to(log_dt[:, None], (SSM_GROUPS, SSM_STATE)).reshape(nt, 1, S5_HALF)

    def b_cp(b):
        return b.reshape(nt, tg, SSM_STATE, SSM_GCH).transpose(0, 3, 1, 2).reshape(nt, SSM_GCH, S5_HALF)

    def c_cp(c):
        return c.reshape(nt, tg, SSM_GCH, SSM_STATE).transpose(0, 2, 1, 3).reshape(nt, SSM_GCH, S5_HALF)

    per_step = 3

    def spec(*shape):
        return pl.BlockSpec((per_step,) + shape, lambda j: (j, 0, 0))

    vec, chan = spec(1, S5_HALF), spec(SSM_GCH, S5_HALF)
    out_shapes = [(1, S5_HALF), (1, S5_HALF), (2 * LANES, 2 * S5_HALF), (2 * S5_HALF, 2 * LANES),
                  (2 * LANES, 2 * LANES)]
    out_dtypes = [F32, F32, BF16, BF16, BF16]
    return pl.pallas_call(
        _s5_prep_kernel,
        grid=(nt // per_step,),
        in_specs=[vec, vec, vec, chan, chan, chan, chan],
        out_specs=[spec(*s) for s in out_shapes],
        out_shape=[jax.ShapeDtypeStruct((nt,) + s, d) for s, d in zip(out_shapes, out_dtypes)],
        name="s5_prep",
    )(lr, li, ldt, b_cp(b_re), b_cp(b_im), c_cp(c_re), c_cp(c_im))


def _mem_kv_kernel(mem_ref, g_ref, wk_ref, wv_ref, k_ref, v_ref, kt_ref, vb_ref):
    mem = mem_ref[...]
    scaled = mem * lax.rsqrt(jnp.mean(mem * mem, axis=-1, keepdims=True) + EPS)
    for i in range(g_ref.shape[0]):
        m = (scaled * g_ref[i]).astype(BF16)
        k = _dot(m, wk_ref[i])
        v = _dot(m, wv_ref[i])
        vb_ref[i] = v.astype(BF16)
        for b in range(k_ref.shape[1]):
            kb = k[b * N_MEM:(b + 1) * N_MEM]
            vb = v[b * N_MEM:(b + 1) * N_MEM]
            kt_ref[i, b] = kb.T.astype(BF16)
            for h in range(N_XHEADS):
                rows = pl.ds(h, N_MEM, stride=N_XHEADS)
                k_ref[i, b, rows, :] = _head(kb, h)
                v_ref[i, b, rows, :] = _head(vb, h)


def _mem_kv(mem, g, wk, wv):
    rows = mem.shape[0]
    depth = g.shape[0]
    nb = 2
    tm = nb * N_MEM
    return pl.pallas_call(
        _mem_kv_kernel,
        grid=(rows // tm,),
        in_specs=[pl.BlockSpec((tm, D_MODEL), lambda r: (r, 0)),
                  _const_spec(g.shape), _const_spec(wk.shape), _const_spec(wv.shape)],
        out_specs=[pl.BlockSpec((depth, nb, N_MEM * N_XHEADS, XHEAD_DIM), lambda r: (0, r, 0, 0))] * 2
        + [pl.BlockSpec((depth, nb, XATT, N_MEM), lambda r: (0, r, 0, 0)),
           pl.BlockSpec((depth, tm, XATT), lambda r: (0, r, 0))],
        out_shape=[jax.ShapeDtypeStruct((depth, rows // N_MEM, N_MEM * N_XHEADS, XHEAD_DIM), F32)] * 2
        + [jax.ShapeDtypeStruct((depth, rows // N_MEM, XATT, N_MEM), BF16),
           jax.ShapeDtypeStruct((depth, rows, XATT), BF16)],
        compiler_params=pltpu.CompilerParams(
            dimension_semantics=("arbitrary",), vmem_limit_bytes=VMEM_LIMIT),
        name="mem_kv",
    )(mem, g, wk, wv)


def _masked_spatial(ws_ref):
    row = lax.broadcasted_iota(jnp.int32, (CHUNK, CHUNK), 0)
    col = lax.broadcasted_iota(jnp.int32, (CHUNK, CHUNK), 1)
    return [jnp.where(row >= col, ws_ref[g], 0.0).astype(BF16) for g in range(A_GROUPS)]


def _gate_chunk(wm, vc, pair):
    base = pair * 2 * A_GDIM
    even = _dot(wm[2 * pair], vc[:, base:base + 2 * LANES])
    odd = _dot(wm[2 * pair + 1], vc[:, base + LANES:base + 3 * LANES])
    lane = lax.broadcasted_iota(jnp.int32, (CHUNK, LANES), 1)
    mid = jnp.where(lane < A_GDIM - LANES, even[:, LANES:], odd[:, :LANES])
    return jnp.concatenate([even[:, :LANES], mid, odd[:, LANES:]], axis=1)


def _s5_tile(u, hr, hi, a2r, a2i, b2, c2, k2, bu_ref, hs_ref, steps, nb):
    pairs = steps // 2
    u3 = u.reshape(pairs, 2 * nb, LANES)
    u2 = jnp.concatenate([u3[:, 0:nb, :].reshape(pairs * nb, LANES),
                          u3[:, nb:, :].reshape(pairs * nb, LANES)], axis=1).astype(BF16)
    bu_ref[...] = _dot(u2, b2)
    a2r = jnp.broadcast_to(a2r, (nb, S5_HALF))
    a2i = jnp.broadcast_to(a2i, (nb, S5_HALF))
    for k in range(pairs):
        rows = slice(k * nb, (k + 1) * nb)
        hs_ref[rows, 0:S5_HALF] = hr
        hs_ref[rows, S5_HALF:] = hi
        hr, hi = (a2r * hr - a2i * hi + bu_ref[rows, 0:S5_HALF],
                  a2r * hi + a2i * hr + bu_ref[rows, S5_HALF:])
    y2 = _dot(hs_ref[...].astype(BF16), c2) + _dot(u2, k2)
    return y2, hr, hi


def _layer_a_prompt_kernel(x_ref, kt_ref, v_ref, win_ref, lng_ref, lnb_ref, ws_ref, bias_ref,
                           wout_ref, pre_ref, post_ref, sq_ref, sk_ref, sv_ref,
                           o_ref, satt_ref, m_ref, g_ref):
    _sample_attn_kernel(sq_ref, sk_ref, sv_ref, satt_ref)
    wm = _masked_spatial(ws_ref)
    heads = range(N_XHEADS)
    n_streams = x_ref.shape[0] // A_STREAM_ROWS

    def project(s):
        x = x_ref[pl.ds(s * A_STREAM_ROWS, A_STREAM_ROWS), :]
        h = _rms(x, pre_ref[...]).astype(BF16)
        v = _layer_norm(_gelu(_dot(h, win_ref[:, BRANCH:2 * BRANCH])), lng_ref[...], lnb_ref[...])
        u = _gelu(_dot(h, win_ref[:, 0:BRANCH]))
        gate = _silu(_dot(h, win_ref[:, 2 * BRANCH + XATT:]))
        g_ref[pl.ds(s * A_STREAM_ROWS, A_STREAM_ROWS), 0:BRANCH] = u * gate[:, 0:BRANCH]
        g_ref[pl.ds(s * A_STREAM_ROWS, A_STREAM_ROWS), BRANCH:] = gate[:, BRANCH:]
        q = _dot(h, win_ref[:, 2 * BRANCH:2 * BRANCH + XATT])
        return x, v.astype(BF16), q

    def mix_and_emit(s, x, vb, q):
        m = m_ref.at[pl.ds(s * A_STREAM_ROWS, A_STREAM_ROWS)]
        g = g_ref.at[pl.ds(s * A_STREAM_ROWS, A_STREAM_ROWS)]
        for c in range(A_STREAM_ROWS // CHUNK):
            rs = slice(c * CHUNK, (c + 1) * CHUNK)
            vc = vb[rs]
            for pair in range(A_GROUPS // 2):
                cs = slice(pair * 2 * A_GDIM, (pair + 1) * 2 * A_GDIM)
                mixed = _gate_chunk(wm, vc, pair) + bias_ref[:, cs]
                m[rs, cs] = (g[rs, cs] * mixed).astype(BF16)
        kv = v_ref[...]
        att = _attend([_head(q, h) for h in heads],
                      [kt_ref[h * XHEAD_DIM:(h + 1) * XHEAD_DIM, :] for h in heads],
                      [_head(kv, h) for h in heads])
        for h in heads:
            cs = slice(BRANCH + h * XHEAD_DIM, BRANCH + (h + 1) * XHEAD_DIM)
            m[:, cs] = (att[h] * g[:, cs]).astype(BF16)
        out = _dot(m[...], wout_ref[...])
        o_ref[pl.ds(s * A_STREAM_ROWS, A_STREAM_ROWS), :] = x + _rms(out, post_ref[...])

    pending = project(0)
    for s in range(n_streams):
        following = project(s + 1) if s + 1 < n_streams else None
        mix_and_emit(s, *pending)
        pending = following


def _layer_a_prompt(x, b0, nb, kt, vb, win, lng, lnb, ws, bias, wout, pre, post, layer,
                    sq, cache_k, cache_v, sample_layer):
    tm = 512
    seq = x.shape[1]
    nt = seq // tm
    bb = sq.shape[0] // (nb * nt)
    xin = pl.BlockSpec((None, tm, D_MODEL), lambda b, t: (b + b0, t, 0))
    xout = pl.BlockSpec((None, tm, D_MODEL), lambda b, t: (b, t, 0))
    kts = pl.BlockSpec((None, None, XATT, N_MEM), lambda b, t: (layer, b + b0, 0, 0))
    vs = pl.BlockSpec((None, N_MEM, XATT), lambda b, t: (layer, b + b0, 0))
    sqs = pl.BlockSpec((bb,) + sq.shape[1:], lambda b, t: (b * nt + t, 0, 0))
    scs = pl.BlockSpec((None, bb) + cache_k.shape[2:], lambda b, t: (sample_layer, b * nt + t, 0, 0))
    return pl.pallas_call(
        _layer_a_prompt_kernel,
        grid=(nb, nt),
        in_specs=[xin, kts, vs, _const_spec(win.shape), _const_spec(lng.shape), _const_spec(lnb.shape),
                  _const_spec(ws.shape), _const_spec(bias.shape), _layer_spec(wout.shape, layer),
                  _layer_spec(pre.shape, layer), _layer_spec(post.shape, layer), sqs, scs, scs],
        out_specs=[xout, sqs],
        out_shape=[jax.ShapeDtypeStruct((nb, seq, D_MODEL), F32), jax.ShapeDtypeStruct(sq.shape, F32)],
        scratch_shapes=[pltpu.VMEM((tm, MIX_WIDTH), BF16), pltpu.VMEM((tm, MIX_WIDTH), F32)],
        compiler_params=pltpu.CompilerParams(
            dimension_semantics=("arbitrary", "arbitrary"), vmem_limit_bytes=VMEM_LIMIT),
        name="layer_a_prompt",
    )(x, kt, vb, win, lng, lnb, ws, bias, wout, pre, post, sq, cache_k, cache_v)


def _layer_b_prompt_kernel(xa_ref, xb_ref, kt_ref, v_ref, win_ref, a2r_ref, a2i_ref, b2_ref, c2_ref, k2_ref,
                           d_ref, wglu_ref, bglu_ref, wout_ref, pre_ref, post_ref,
                           o_ref, hre_ref, him_ref,
                           ut_ref, bu_ref, hs_ref, yt_ref, y_ref, m_ref):
    nb, tc, _ = o_ref.shape
    tm = nb * tc
    half = tc // 2

    @pl.when(pl.program_id(0) == 0)
    def _():
        hre_ref[...] = jnp.zeros_like(hre_ref)
        him_ref[...] = jnp.zeros_like(him_ref)

    x = jnp.concatenate([xa_ref[...], xb_ref[...]], axis=0).reshape(tm, D_MODEL)
    h = _rms(x, pre_ref[...]).astype(BF16)
    u = _dot(h, win_ref[:, 0:BRANCH])
    for j in range(S5_TILES):
        for b in range(nb):
            ut_ref[j, pl.ds(b, tc, stride=nb), :] = u[b * tc:(b + 1) * tc, j * LANES:(j + 1) * LANES]
    for j in range(S5_TILES):
        ss = slice(j * S5_HALF, (j + 1) * S5_HALF)
        par = j % 2
        y2, hr, hi = _s5_tile(ut_ref[j], hre_ref[:, ss], him_ref[:, ss], a2r_ref[j], a2i_ref[j],
                              b2_ref[j], c2_ref[j], k2_ref[j], bu_ref.at[par], hs_ref.at[par], tc, nb)
        hre_ref[:, ss] = hr
        him_ref[:, ss] = hi
        for tp in range(2):
            yt_ref[par, tp] = y2[:, tp * LANES:(tp + 1) * LANES]
            for b in range(nb):
                y_ref[j, pl.ds(b * tc + tp, half, stride=2), :] = yt_ref[par, tp, pl.ds(b, half, stride=nb), :]
    y = _gelu(jnp.concatenate([y_ref[j] for j in range(S5_TILES)], axis=1) + d_ref[...] * u)
    branch = y * _sigmoid(_dot(y.astype(BF16), wglu_ref[...]) + bglu_ref[...])
    gate = _silu(_dot(h, win_ref[:, BRANCH + XATT:]))
    m_ref[:, 0:BRANCH] = (branch * gate[:, 0:BRANCH]).astype(BF16)
    q = _dot(h, win_ref[:, BRANCH:BRANCH + XATT])
    pieces = [(b, h) for b in range(nb) for h in range(N_XHEADS)]
    att = _attend([_head(q[b * tc:(b + 1) * tc], h) for b, h in pieces],
                  [kt_ref[b, h * XHEAD_DIM:(h + 1) * XHEAD_DIM, :] for b, h in pieces],
                  [_head(v_ref[b * N_MEM:(b + 1) * N_MEM, :], h) for b, h in pieces])
    for (b, h), a in zip(pieces, att):
        rs = slice(b * tc, (b + 1) * tc)
        cs = slice(BRANCH + h * XHEAD_DIM, BRANCH + (h + 1) * XHEAD_DIM)
        m_ref[rs, cs] = (a * gate[rs, cs]).astype(BF16)
    out = _dot(m_ref[...], wout_ref[...])
    o_ref[...] = (x + _rms(out, post_ref[...])).reshape(nb, tc, D_MODEL)


def _layer_b_prompt(xa, xb, kb, vb, win, s5, dvec, wglu, bglu, wout, pre, post, layer):
    tc = 64
    nb, seq = xa.shape[0] + xb.shape[0], xa.shape[1]
    tm = nb * tc
    xin = pl.BlockSpec((xa.shape[0], tc, D_MODEL), lambda t: (0, t, 0))
    xs = pl.BlockSpec((nb, tc, D_MODEL), lambda t: (0, t, 0))
    hs = pl.BlockSpec((nb, SSM_GROUPS * SSM_STATE), lambda t: (0, 0))
    ins = [kb, vb, win, *s5, dvec, wglu, bglu, wout, pre, post]
    stacked = [True, True] + [False] * (len(s5) + 4) + [True, True, True]
    return pl.pallas_call(
        _layer_b_prompt_kernel,
        grid=(seq // tc,),
        in_specs=[xin, xin] + [_layer_spec(a.shape, layer) if st else _const_spec(a.shape)
                               for a, st in zip(ins, stacked)],
        out_specs=[xs, hs, hs],
        out_shape=[jax.ShapeDtypeStruct((nb, seq, D_MODEL), F32),
                   jax.ShapeDtypeStruct((nb, SSM_GROUPS * SSM_STATE), F32),
                   jax.ShapeDtypeStruct((nb, SSM_GROUPS * SSM_STATE), F32)],
        scratch_shapes=[pltpu.VMEM((S5_TILES, tm, LANES), F32),
                        pltpu.VMEM((2, tm // 2, 2 * S5_HALF), F32),
                        pltpu.VMEM((2, tm // 2, 2 * S5_HALF), F32),
                        pltpu.VMEM((2, 2, tm // 2, LANES), F32),
                        pltpu.VMEM((S5_TILES, tm, LANES), F32),
                        pltpu.VMEM((tm, MIX_WIDTH), BF16)],
        compiler_params=pltpu.CompilerParams(
            dimension_semantics=("arbitrary",), vmem_limit_bytes=VMEM_LIMIT),
        name="layer_b_prompt",
    )(xa, xb, *ins)


Q_ROWS = N_XHEADS * DEC_SEQ


def _store_q_by_sequence(q_ref, q):
    for h in range(N_XHEADS):
        for t in range(DEC_SEQ):
            q_ref[pl.ds(h * DEC_SEQ + t, DEC_BATCH, stride=Q_ROWS), :] = _head(q[t * DEC_BATCH:(t + 1) * DEC_BATCH], h)


def _sample_pre_a_kernel(x_ref, win_ref, lng_ref, lnb_ref, wrow_ref, brow_ref, pre_ref,
                         v_ref, mbr_ref, q_ref, gatt_ref):
    nb = DEC_BATCH
    h = _rms(x_ref[...], pre_ref[...]).astype(BF16)
    gate = _silu(_dot(h, win_ref[:, 2 * BRANCH + XATT:]))
    u = _gelu(_dot(h, win_ref[:, 0:BRANCH]))
    v = _layer_norm(_gelu(_dot(h, win_ref[:, BRANCH:2 * BRANCH])), lng_ref[...], lnb_ref[...])
    v_ref[...] = v
    for t in range(DEC_SEQ):
        rs = slice(t * nb, (t + 1) * nb)
        mixed = brow_ref[t:t + 1, :]
        for s in range(t + 1):
            mixed = mixed + wrow_ref[t * DEC_SEQ + s:t * DEC_SEQ + s + 1, :] * v[s * nb:(s + 1) * nb]
        mbr_ref[rs, :] = u[rs] * mixed * gate[rs, 0:BRANCH]
    _store_q_by_sequence(q_ref, _dot(h, win_ref[:, 2 * BRANCH:2 * BRANCH + XATT]))
    gatt_ref[...] = gate[:, BRANCH:]


def _sample_pre_a(x, win, lng, lnb, wrow, brow, pre, layer):
    rows = x.shape[0]
    ins = [x, win, lng, lnb, wrow, brow, pre]
    shapes = [(rows, BRANCH), (rows, BRANCH), (DEC_BATCH * Q_ROWS, XHEAD_DIM), (rows, XATT)]
    return pl.pallas_call(
        _sample_pre_a_kernel,
        grid=(1,),
        in_specs=[_const_spec(a.shape) for a in ins[:-1]] + [_layer_spec(pre.shape, layer)],
        out_specs=[pl.BlockSpec(s, lambda i: (0, 0)) for s in shapes],
        out_shape=[jax.ShapeDtypeStruct(s, F32) for s in shapes],
        compiler_params=pltpu.CompilerParams(
            dimension_semantics=("arbitrary",), vmem_limit_bytes=VMEM_LIMIT),
        name="sample_pre_a",
    )(*ins)


def _sample_pre_b_kernel(x_ref, win_ref, sre_ref, sim_ref, a2r_ref, a2i_ref, b2_ref, c2_ref, k2_ref, d_ref,
                         wglu_ref, bglu_ref, pre_ref,
                         mbr_ref, q_ref, gatt_ref, hre_ref, him_ref,
                         bu_ref, hs_ref, y_ref):
    nb = DEC_BATCH
    h = _rms(x_ref[...], pre_ref[...]).astype(BF16)
    u = _dot(h, win_ref[:, 0:BRANCH])
    for j in range(S5_TILES):
        ss = slice(j * S5_HALF, (j + 1) * S5_HALF)
        ls = slice(j * LANES, (j + 1) * LANES)
        y2, hr, hi = _s5_tile(u[:, ls], sre_ref[:, ss], sim_ref[:, ss], a2r_ref[j], a2i_ref[j],
                              b2_ref[j], c2_ref[j], k2_ref[j], bu_ref, hs_ref, DEC_SEQ, nb)
        hre_ref[:, ss] = hr
        him_ref[:, ss] = hi
        for pair in range(DEC_SEQ // 2):
            for tp in range(2):
                t = 2 * pair + tp
                y_ref[t * nb:(t + 1) * nb, ls] = y2[pair * nb:(pair + 1) * nb, tp * LANES:(tp + 1) * LANES]
    y = _gelu(y_ref[...] + d_ref[...] * u)
    branch = y * _sigmoid(_dot(y.astype(BF16), wglu_ref[...]) + bglu_ref[...])
    gate = _silu(_dot(h, win_ref[:, BRANCH + XATT:]))
    mbr_ref[...] = branch * gate[:, 0:BRANCH]
    _store_q_by_sequence(q_ref, _dot(h, win_ref[:, BRANCH:BRANCH + XATT]))
    gatt_ref[...] = gate[:, BRANCH:]


def _sample_pre_b(x, win, sre, sim, s5, dvec, wglu, bglu, pre, layer):
    rows = x.shape[0]
    ins = [x, win, sre, sim, *s5, dvec, wglu, bglu]
    shapes = [(rows, BRANCH), (DEC_BATCH * Q_ROWS, XHEAD_DIM), (rows, XATT), sre.shape, sim.shape]
    return pl.pallas_call(
        _sample_pre_b_kernel,
        grid=(1,),
        in_specs=[_const_spec(a.shape) for a in ins] + [_layer_spec(pre.shape, layer)],
        out_specs=[pl.BlockSpec(s, lambda i: (0, 0)) for s in shapes],
        out_shape=[jax.ShapeDtypeStruct(s, F32) for s in shapes],
        scratch_shapes=[pltpu.VMEM((rows // 2, 2 * S5_HALF), F32),
                        pltpu.VMEM((rows // 2, 2 * S5_HALF), F32),
                        pltpu.VMEM((rows, BRANCH), F32)],
        compiler_params=pltpu.CompilerParams(
            dimension_semantics=("arbitrary",), vmem_limit_bytes=VMEM_LIMIT),
        name="sample_pre_b",
    )(*ins, pre)


def _sample_attn_kernel(q_ref, k_ref, v_ref, o_ref):
    bb, rows, _ = q_ref.shape
    cols = k_ref.shape[1]
    row_h = (lax.broadcasted_iota(jnp.int32, (bb * rows, cols), 0) // DEC_SEQ) % N_XHEADS
    col_h = lax.broadcasted_iota(jnp.int32, (bb * rows, cols), 1) % N_XHEADS
    s = jnp.concatenate([_dot_nt(q_ref[j].astype(BF16), k_ref[j].astype(BF16)) for j in range(bb)], axis=0)
    s = jnp.where(row_h == col_h, s * (XHEAD_DIM ** -0.5), -1e30)
    p = jnp.exp(s - jnp.max(s, axis=-1, keepdims=True))
    inv = 1.0 / jnp.sum(p, axis=-1, keepdims=True)
    pb = p.astype(BF16)
    for j in range(bb):
        rs = slice(j * rows, (j + 1) * rows)
        o_ref[j] = _dot(pb[rs], v_ref[j].astype(BF16)) * inv[rs]


def _sample_post_kernel(x_ref, mbr_ref, att_ref, gatt_ref, wout_ref, post_ref, o_ref, m_ref):
    nb = DEC_BATCH
    for h in range(N_XHEADS):
        cs = slice(h * XHEAD_DIM, (h + 1) * XHEAD_DIM)
        for t in range(DEC_SEQ):
            rs = slice(t * nb, (t + 1) * nb)
            att = att_ref[pl.ds(h * DEC_SEQ + t, nb, stride=Q_ROWS), :]
            m_ref[rs, cs] = (att * gatt_ref[rs, cs]).astype(BF16)
    out = (_dot(mbr_ref[...].astype(BF16), wout_ref[0:BRANCH, :])
           + _dot(m_ref[...], wout_ref[BRANCH:, :]))
    o_ref[...] = x_ref[...] + _rms(out, post_ref[...])


def _sample_post(x, mbr, att, gatt, wout, post, layer):
    ins = [x, mbr, att, gatt, wout, post]
    return pl.pallas_call(
        _sample_post_kernel,
        grid=(1,),
        in_specs=[_const_spec(a.shape) for a in ins[:4]]
        + [_layer_spec(wout.shape, layer), _layer_spec(post.shape, layer)],
        out_specs=pl.BlockSpec(x.shape, lambda i: (0, 0)),
        out_shape=jax.ShapeDtypeStruct(x.shape, F32),
        scratch_shapes=[pltpu.VMEM((x.shape[0], XATT), BF16)],
        compiler_params=pltpu.CompilerParams(
            dimension_semantics=("arbitrary",), vmem_limit_bytes=VMEM_LIMIT),
        name="sample_post",
    )(*ins)


def _to_bt(a):
    return a.reshape(DEC_SEQ, DEC_BATCH, a.shape[-1]).transpose(1, 0, 2)


def _to_tb(a):
    return a.transpose(1, 0, 2).reshape(DEC_SEQ * DEC_BATCH, a.shape[-1])


def kernel(x_prompt, x_sample, cache_mem_k, cache_mem_v, state_ssm_re, state_ssm_im, mem_prompt,
           w_in_a, ln_v_g, ln_v_b, w_spatial, b_spatial,
           w_in_b, ssm_lambda_re, ssm_lambda_im, ssm_log_dt, ssm_b_re, ssm_b_im, ssm_c_re, ssm_c_im,
           ssm_d, w_glu, b_glu,
           mem_norm_g, w_mem_k, w_mem_v, w_out, pre_norm_g, post_norm_g):
    depth = w_out.shape[0]
    win_a = w_in_a[0].astype(BF16)
    win_b = w_in_b[0].astype(BF16)
    wglu = w_glu[0].astype(BF16)
    wout = w_out.astype(BF16)
    wk = w_mem_k.astype(BF16)
    wv = w_mem_v.astype(BF16)
    pre = pre_norm_g.reshape(depth, 1, D_MODEL)
    post = post_norm_g.reshape(depth, 1, D_MODEL)
    lng = ln_v_g[0].reshape(1, BRANCH)
    lnb = ln_v_b[0].reshape(1, BRANCH)
    bglu = b_glu[0].reshape(1, BRANCH)
    dvec = ssm_d[0].reshape(1, BRANCH)
    bias = jnp.repeat(b_spatial[0].T, A_GDIM, axis=1)
    ws4 = w_spatial[0][:, :DEC_SEQ, :DEC_SEQ]
    wrow = jnp.repeat(ws4.transpose(1, 2, 0).reshape(DEC_SEQ * DEC_SEQ, A_GROUPS), A_GDIM, axis=1)
    brow = bias[:DEC_SEQ]

    s5 = _s5_prep(ssm_lambda_re[0], ssm_lambda_im[0], ssm_log_dt[0],
                  ssm_b_re[0], ssm_b_im[0], ssm_c_re[0], ssm_c_im[0])

    mem = mem_prompt.reshape(BATCH * N_MEM, D_MODEL)
    mk, mv, mkt, mvb = _mem_kv(mem, mem_norm_g.reshape(depth, 1, D_MODEL), wk, wv)

    ck = cache_mem_k.reshape(depth, DEC_BATCH, N_MEM * N_XHEADS, XHEAD_DIM)
    cv = cache_mem_v.reshape(depth, DEC_BATCH, N_MEM * N_XHEADS, XHEAD_DIM)
    q_shape = (DEC_BATCH, Q_ROWS, XHEAD_DIM)
    half = BATCH // 2
    layer_a = functools.partial(_layer_a_prompt, x_prompt, kt=mkt, vb=mvb, win=win_a, lng=lng, lnb=lnb,
                                ws=w_spatial[0], bias=bias, wout=wout, pre=pre, post=post, layer=0,
                                cache_k=ck, cache_v=cv)
    xs = _to_tb(x_sample)
    v_rows, mbr, q, gatt = _sample_pre_a(xs, win_a, lng, lnb, wrow, brow, pre, 0)
    yp_a, att = layer_a(b0=0, nb=half, sq=q.reshape(q_shape), sample_layer=0)
    xs = _sample_post(xs, mbr, att.reshape(q.shape), gatt, wout, post, 0)
    sre = state_ssm_re[0].reshape(DEC_BATCH, SSM_GROUPS * SSM_STATE)
    sim = state_ssm_im[0].reshape(DEC_BATCH, SSM_GROUPS * SSM_STATE)
    mbr, q, gatt, hs_re, hs_im = _sample_pre_b(xs, win_b, sre, sim, s5, dvec, wglu, bglu, pre, 1)
    yp_b, att = layer_a(b0=half, nb=half, sq=q.reshape(q_shape), sample_layer=1)
    xs = _sample_post(xs, mbr, att.reshape(q.shape), gatt, wout, post, 1)
    yp, hp_re, hp_im = _layer_b_prompt(yp_a, yp_b, mkt, mvb, win_b, s5, dvec, wglu, bglu, wout, pre, post, 1)

    kv_shape = (depth, BATCH, N_MEM, N_XHEADS, XHEAD_DIM)
    st_p = (1, BATCH, SSM_GROUPS, SSM_STATE)
    st_s = (1, DEC_BATCH, SSM_GROUPS, SSM_STATE)
    return (yp, _to_bt(xs), mk.reshape(kv_shape), mv.reshape(kv_shape),
            hp_re.reshape(st_p), hp_im.reshape(st_p), hs_re.reshape(st_s), hs_im.reshape(st_s),
            _to_bt(v_rows)[None])
```

```python
import functools
import math

import jax
import jax.numpy as jnp
from jax import lax
from jax.experimental import pallas as pl
from jax.experimental.pallas import tpu as pltpu

D_MODEL = 1024
BATCH = 8
SEQ = 2048
DEC_BATCH = 128
DEC_SEQ = 4
BRANCH = 1536
N_XHEADS = 4
XHEAD_DIM = 128
XATT = N_XHEADS * XHEAD_DIM
MIX_WIDTH = BRANCH + XATT
N_MEM = 256
CHUNK = 128
A_GROUPS = 8
A_GDIM = BRANCH // A_GROUPS
SSM_GCH = 16
SSM_GROUPS = 96
SSM_STATE = 64
EPS = 1e-6

LANES = 128
S5_TILE_GROUPS = LANES // SSM_GCH
S5_TILES = SSM_GROUPS // S5_TILE_GROUPS
S5_HALF = S5_TILE_GROUPS * SSM_STATE
VMEM_LIMIT = 60 * 1024 * 1024
A_STREAM_ROWS = 256

F32 = jnp.float32
BF16 = jnp.bfloat16


def _dot(a, b):
    return jnp.dot(a, b, preferred_element_type=F32)


def _rms(x, g):
    return x * lax.rsqrt(jnp.mean(x * x, axis=-1, keepdims=True) + EPS) * g


def _layer_norm(x, g, b):
    mu = jnp.mean(x, axis=-1, keepdims=True)
    xc = x - mu
    var = jnp.mean(xc * xc, axis=-1, keepdims=True)
    return xc * lax.rsqrt(var + EPS) * g + b


def _gelu(x):
    c = math.sqrt(2.0 / math.pi)
    hx = 0.5 * x
    return hx + hx * jnp.tanh(x * (c + (c * 0.044715) * (x * x)))


def _sigmoid(x):
    return 0.5 + 0.5 * jnp.tanh(0.5 * x)


def _silu(x):
    hx = 0.5 * x
    return hx + hx * jnp.tanh(hx)


def _dot_nt(a, b):
    return lax.dot_general(a, b, (((1,), (1,)), ((), ())), preferred_element_type=F32)


def _attend(qs, kts, vs):
    r = qs[0].shape[0]
    s = jnp.concatenate([_dot(q.astype(BF16), kt) for q, kt in zip(qs, kts)], axis=0)
    s = s * (XHEAD_DIM ** -0.5)
    p = jnp.exp(s - jnp.max(s, axis=-1, keepdims=True))
    inv = 1.0 / jnp.sum(p, axis=-1, keepdims=True)
    pb = p.astype(BF16)
    return [_dot(pb[i * r:(i + 1) * r], v) * inv[i * r:(i + 1) * r] for i, v in enumerate(vs)]


def _head(a, h):
    return a[:, h * XHEAD_DIM:(h + 1) * XHEAD_DIM]


def _const_spec(shape):
    nd = len(shape)
    return pl.BlockSpec(shape, lambda *_: (0,) * nd, pipeline_mode=pl.Buffered(1))


def _layer_spec(shape, layer):
    nd = len(shape)
    return pl.BlockSpec((None,) + tuple(shape[1:]), lambda *_: (layer,) + (0,) * (nd - 1),
                        pipeline_mode=pl.Buffered(1))


def _lam_bar(lr, li, ldt):
    dt = jnp.exp(ldt)
    e = jnp.exp(lr * dt)
    return e * jnp.cos(li * dt), e * jnp.sin(li * dt)


def _s5_prep_kernel(*refs):
    for j in range(refs[0].shape[0]):
        _s5_prep_tile(*[r.at[pl.ds(j, 1)] for r in refs])


def _s5_prep_tile(lr_ref, li_ref, ldt_ref, br_ref, bi_ref, cr_ref, ci_ref,
                  a2r_ref, a2i_ref, b2_ref, c2_ref, k2_ref):
    lr = lr_ref[0]
    li = li_ref[0]
    ar, ai = _lam_bar(lr, li, ldt_ref[0])
    a2r = ar * ar - ai * ai
    a2i = 2.0 * ar * ai
    a2r_ref[0] = a2r
    a2i_ref[0] = a2i
    nr = ar - 1.0
    den = lr * lr + li * li
    kr = (nr * lr + ai * li) / den
    ki = (ai * lr - nr * li) / den
    shape = (LANES, S5_HALF)
    own = (lax.broadcasted_iota(jnp.int32, shape, 0) // SSM_GCH
           == lax.broadcasted_iota(jnp.int32, shape, 1) // SSM_STATE)

    def own_blocks(ref):
        return jnp.where(own, jnp.concatenate([ref[0]] * S5_TILE_GROUPS, axis=0), 0.0)

    br, bi, cr, ci = own_blocks(br_ref), own_blocks(bi_ref), own_blocks(cr_ref), own_blocks(ci_ref)
    bbr = kr * br - ki * bi
    bbi = kr * bi + ki * br
    b_tok1 = jnp.concatenate([bbr, bbi], axis=1).astype(BF16)
    b_tok0 = jnp.concatenate([ar * bbr - ai * bbi, ar * bbi + ai * bbr], axis=1).astype(BF16)
    b2_ref[0, 0:LANES, :] = b_tok0
    b2_ref[0, LANES:, :] = b_tok1
    c_tok0 = jnp.concatenate([cr * ar - ci * ai, -(cr * ai + ci * ar)], axis=1)
    c_tok1 = jnp.concatenate([cr * a2r - ci * a2i, -(cr * a2i + ci * a2r)], axis=1)
    c2_ref[0, :, 0:LANES] = c_tok0.T.astype(BF16)
    c2_ref[0, :, LANES:] = c_tok1.T.astype(BF16)
    c_plain = jnp.concatenate([cr, -ci], axis=1).astype(BF16)
    k0 = _dot_nt(b_tok1, c_plain).astype(BF16)
    k1 = _dot_nt(b_tok0, c_plain).astype(BF16)
    k2_ref[0, 0:LANES, 0:LANES] = k0
    k2_ref[0, 0:LANES, LANES:] = k1
    k2_ref[0, LANES:, 0:LANES] = jnp.zeros_like(k0)
    k2_ref[0, LANES:, LANES:] = k0


def _s5_prep(lam_re, lam_im, log_dt, b_re, b_im, c_re, c_im):
    nt, tg = S5_TILES, S5_TILE_GROUPS
    lr = lam_re.reshape(nt, 1, S5_HALF)
    li = lam_im.reshape(nt, 1, S5_HALF)
    ldt = jnp.broadcast_to(log_dt[:, None], (SSM_GROUPS, SSM_STATE)).reshape(nt, 1, S5_HALF)

    def b_cp(b):
        return b.reshape(nt, tg, SSM_STATE, SSM_GCH).transpose(0, 3, 1, 2).reshape(nt, SSM_GCH, S5_HALF)

    def c_cp(c):
        return c.reshape(nt, tg, SSM_GCH, SSM_STATE).transpose(0, 2, 1, 3).reshape(nt, SSM_GCH, S5_HALF)

    per_step = 3

    def spec(*shape):
        return pl.BlockSpec((per_step,) + shape, lambda j: (j, 0, 0))

    vec, chan = spec(1, S5_HALF), spec(SSM_GCH, S5_HALF)
    out_shapes = [(1, S5_HALF), (1, S5_HALF), (2 * LANES, 2 * S5_HALF), (2 * S5_HALF, 2 * LANES),
                  (2 * LANES, 2 * LANES)]
    out_dtypes = [F32, F32, BF16, BF16, BF16]
    return pl.pallas_call(
        _s5_prep_kernel,
        grid=(nt // per_step,),
        in_specs=[vec, vec, vec, chan, chan, chan, chan],
        out_specs=[spec(*s) for s in out_shapes],
        out_shape=[jax.ShapeDtypeStruct((nt,) + s, d) for s, d in zip(out_shapes, out_dtypes)],
        name="s5_prep",
    )(lr, li, ldt, b_cp(b_re), b_cp(b_im), c_cp(c_re), c_cp(c_im))


def _mem_kv_kernel(mem_ref, g_ref, wk_ref, wv_ref, k_ref, v_ref, kt_ref, vb_ref):
    mem = mem_ref[...]
    scaled = mem * lax.rsqrt(jnp.mean(mem * mem, axis=-1, keepdims=True) + EPS)
    for i in range(g_ref.shape[0]):
        m = (scaled * g_ref[i]).astype(BF16)
        k = _dot(m, wk_ref[i])
        v = _dot(m, wv_ref[i])
        vb_ref[i] = v.astype(BF16)
        for b in range(k_ref.shape[1]):
            kb = k[b * N_MEM:(b + 1) * N_MEM]
            vb = v[b * N_MEM:(b + 1) * N_MEM]
            kt_ref[i, b] = kb.T.astype(BF16)
            for h in range(N_XHEADS):
                rows = pl.ds(h, N_MEM, stride=N_XHEADS)
                k_ref[i, b, rows, :] = _head(kb, h)
                v_ref[i, b, rows, :] = _head(vb, h)


def _mem_kv(mem, g, wk, wv):
    rows = mem.shape[0]
    depth = g.shape[0]
    nb = 2
    tm = nb * N_MEM
    return pl.pallas_call(
        _mem_kv_kernel,
        grid=(rows // tm,),
        in_specs=[pl.BlockSpec((tm, D_MODEL), lambda r: (r, 0)),
                  _const_spec(g.shape), _const_spec(wk.shape), _const_spec(wv.shape)],
        out_specs=[pl.BlockSpec((depth, nb, N_MEM * N_XHEADS, XHEAD_DIM), lambda r: (0, r, 0, 0))] * 2
        + [pl.BlockSpec((depth, nb, XATT, N_MEM), lambda r: (0, r, 0, 0)),
           pl.BlockSpec((depth, tm, XATT), lambda r: (0, r, 0))],
        out_shape=[jax.ShapeDtypeStruct((depth, rows // N_MEM, N_MEM * N_XHEADS, XHEAD_DIM), F32)] * 2
        + [jax.ShapeDtypeStruct((depth, rows // N_MEM, XATT, N_MEM), BF16),
           jax.ShapeDtypeStruct((depth, rows, XATT), BF16)],
        compiler_params=pltpu.CompilerParams(
            dimension_semantics=("arbitrary",), vmem_limit_bytes=VMEM_LIMIT),
        name="mem_kv",
    )(mem, g, wk, wv)


def _masked_spatial(ws_ref):
    row = lax.broadcasted_iota(jnp.int32, (CHUNK, CHUNK), 0)
    col = lax.broadcasted_iota(jnp.int32, (CHUNK, CHUNK), 1)
    return [jnp.where(row >= col, ws_ref[g], 0.0).astype(BF16) for g in range(A_GROUPS)]


def _gate_chunk(wm, vc, pair):
    base = pair * 2 * A_GDIM
    even = _dot(wm[2 * pair], vc[:, base:base + 2 * LANES])
    odd = _dot(wm[2 * pair + 1], vc[:, base + LANES:base + 3 * LANES])
    lane = lax.broadcasted_iota(jnp.int32, (CHUNK, LANES), 1)
    mid = jnp.where(lane < A_GDIM - LANES, even[:, LANES:], odd[:, :LANES])
    return jnp.concatenate([even[:, :LANES], mid, odd[:, LANES:]], axis=1)


def _s5_tile(u, hr, hi, a2r, a2i, b2, c2, k2, bu_ref, hs_ref, steps, nb):
    pairs = steps // 2
    u3 = u.reshape(pairs, 2 * nb, LANES)
    u2 = jnp.concatenate([u3[:, 0:nb, :].reshape(pairs * nb, LANES),
                          u3[:, nb:, :].reshape(pairs * nb, LANES)], axis=1).astype(BF16)
    bu_ref[...] = _dot(u2, b2)
    a2r = jnp.broadcast_to(a2r, (nb, S5_HALF))
    a2i = jnp.broadcast_to(a2i, (nb, S5_HALF))
    for k in range(pairs):
        rows = slice(k * nb, (k + 1) * nb)
        hs_ref[rows, 0:S5_HALF] = hr
        hs_ref[rows, S5_HALF:] = hi
        hr, hi = (a2r * hr - a2i * hi + bu_ref[rows, 0:S5_HALF],
                  a2r * hi + a2i * hr + bu_ref[rows, S5_HALF:])
    y2 = _dot(hs_ref[...].astype(BF16), c2) + _dot(u2, k2)
    return y2, hr, hi


def _layer_a_prompt_kernel(*refs, n_casts):
    (x_ref, kt_ref, v_ref, win_ref, lng_ref, lnb_ref, ws_ref, bias_ref,
     wout_ref, pre_ref, post_ref, sq_ref, sk_ref, sv_ref) = refs[:14]
    cast_in = refs[14:14 + n_casts]
    o_ref, satt_ref = refs[14 + n_casts:16 + n_casts]
    cast_out = refs[16 + n_casts:16 + 2 * n_casts]
    m_ref, g_ref = refs[16 + 2 * n_casts:]
    _sample_attn_kernel(sq_ref, sk_ref, sv_ref, satt_ref)
    for src, dst in zip(cast_in, cast_out):
        dst[...] = src[...].astype(BF16)
    wm = _masked_spatial(ws_ref)
    heads = range(N_XHEADS)
    n_streams = x_ref.shape[0] // A_STREAM_ROWS

    def project(s):
        x = x_ref[pl.ds(s * A_STREAM_ROWS, A_STREAM_ROWS), :]
        h = _rms(x, pre_ref[...]).astype(BF16)
        v = _layer_norm(_gelu(_dot(h, win_ref[:, BRANCH:2 * BRANCH])), lng_ref[...], lnb_ref[...])
        u = _gelu(_dot(h, win_ref[:, 0:BRANCH]))
        gate = _silu(_dot(h, win_ref[:, 2 * BRANCH + XATT:]))
        g_ref[pl.ds(s * A_STREAM_ROWS, A_STREAM_ROWS), 0:BRANCH] = u * gate[:, 0:BRANCH]
        g_ref[pl.ds(s * A_STREAM_ROWS, A_STREAM_ROWS), BRANCH:] = gate[:, BRANCH:]
        q = _dot(h, win_ref[:, 2 * BRANCH:2 * BRANCH + XATT])
        return x, v.astype(BF16), q

    def mix_and_emit(s, x, vb, q):
        m = m_ref.at[pl.ds(s * A_STREAM_ROWS, A_STREAM_ROWS)]
        g = g_ref.at[pl.ds(s * A_STREAM_ROWS, A_STREAM_ROWS)]
        for c in range(A_STREAM_ROWS // CHUNK):
            rs = slice(c * CHUNK, (c + 1) * CHUNK)
            vc = vb[rs]
            for pair in range(A_GROUPS // 2):
                cs = slice(pair * 2 * A_GDIM, (pair + 1) * 2 * A_GDIM)
                mixed = _gate_chunk(wm, vc, pair) + bias_ref[:, cs]
                m[rs, cs] = (g[rs, cs] * mixed).astype(BF16)
        kv = v_ref[...]
        att = _attend([_head(q, h) for h in heads],
                      [kt_ref[h * XHEAD_DIM:(h + 1) * XHEAD_DIM, :] for h in heads],
                      [_head(kv, h) for h in heads])
        for h in heads:
            cs = slice(BRANCH + h * XHEAD_DIM, BRANCH + (h + 1) * XHEAD_DIM)
            m[:, cs] = (att[h] * g[:, cs]).astype(BF16)
        out = _dot(m[...], wout_ref[...])
        o_ref[pl.ds(s * A_STREAM_ROWS, A_STREAM_ROWS), :] = x + _rms(out, post_ref[...])

    pending = project(0)
    for s in range(n_streams):
        following = project(s + 1) if s + 1 < n_streams else None
        mix_and_emit(s, *pending)
        pending = following


def _layer_a_prompt(x, b0, nb, kt, vb, win, lng, lnb, ws, bias, wout, pre, post, layer,
                    sq, cache_k, cache_v, sample_layer, casts=()):
    tm = 512
    seq = x.shape[1]
    nt = seq // tm
    steps = nb * nt
    bb = sq.shape[0] // steps
    cast_specs = [pl.BlockSpec((None, w.shape[1] // steps, w.shape[2]),
                               functools.partial(lambda b, t, l: (l, b * nt + t, 0), l=l))
                  for w, l in casts]
    cast_out_specs = [pl.BlockSpec((w.shape[1] // steps, w.shape[2]), lambda b, t: (b * nt + t, 0))
                      for w, _ in casts]
    xin = pl.BlockSpec((None, tm, D_MODEL), lambda b, t: (b + b0, t, 0))
    xout = pl.BlockSpec((None, tm, D_MODEL), lambda b, t: (b, t, 0))
    kts = pl.BlockSpec((None, None, XATT, N_MEM), lambda b, t: (layer, b + b0, 0, 0))
    vs = pl.BlockSpec((None, N_MEM, XATT), lambda b, t: (layer, b + b0, 0))
    sqs = pl.BlockSpec((bb,) + sq.shape[1:], lambda b, t: (b * nt + t, 0, 0))
    scs = pl.BlockSpec((None, bb) + cache_k.shape[2:], lambda b, t: (sample_layer, b * nt + t, 0, 0))
    return pl.pallas_call(
        functools.partial(_layer_a_prompt_kernel, n_casts=len(casts)),
        grid=(nb, nt),
        in_specs=[xin, kts, vs, _const_spec(win.shape), _const_spec(lng.shape), _const_spec(lnb.shape),
                  _const_spec(ws.shape), _const_spec(bias.shape), _const_spec(wout.shape),
                  _layer_spec(pre.shape, layer), _layer_spec(post.shape, layer), sqs, scs, scs] + cast_specs,
        out_specs=[xout, sqs] + cast_out_specs,
        out_shape=[jax.ShapeDtypeStruct((nb, seq, D_MODEL), F32), jax.ShapeDtypeStruct(sq.shape, F32)]
        + [jax.ShapeDtypeStruct(w.shape[1:], BF16) for w, _ in casts],
        scratch_shapes=[pltpu.VMEM((tm, MIX_WIDTH), BF16), pltpu.VMEM((tm, MIX_WIDTH), F32)],
        compiler_params=pltpu.CompilerParams(
            dimension_semantics=("arbitrary", "arbitrary"), vmem_limit_bytes=VMEM_LIMIT),
        name="layer_a_prompt",
    )(x, kt, vb, win, lng, lnb, ws, bias, wout, pre, post, sq, cache_k, cache_v, *[w for w, _ in casts])


def _layer_b_prompt_kernel(xa_ref, xb_ref, kt_ref, v_ref, win_ref, a2r_ref, a2i_ref, b2_ref, c2_ref, k2_ref,
                           d_ref, wglu_ref, bglu_ref, wout_ref, pre_ref, post_ref,
                           o_ref, hre_ref, him_ref,
                           ut_ref, bu_ref, hs_ref, yt_ref, y_ref, m_ref):
    nb, tc, _ = o_ref.shape
    tm = nb * tc
    half = tc // 2

    @pl.when(pl.program_id(0) == 0)
    def _():
        hre_ref[...] = jnp.zeros_like(hre_ref)
        him_ref[...] = jnp.zeros_like(him_ref)

    x = jnp.concatenate([xa_ref[...], xb_ref[...]], axis=0).reshape(tm, D_MODEL)
    h = _rms(x, pre_ref[...]).astype(BF16)
    u = _dot(h, win_ref[:, 0:BRANCH])
    for j in range(S5_TILES):
        for b in range(nb):
            ut_ref[j, pl.ds(b, tc, stride=nb), :] = u[b * tc:(b + 1) * tc, j * LANES:(j + 1) * LANES]
    for j in range(S5_TILES):
        ss = slice(j * S5_HALF, (j + 1) * S5_HALF)
        par = j % 2
        y2, hr, hi = _s5_tile(ut_ref[j], hre_ref[:, ss], him_ref[:, ss], a2r_ref[j], a2i_ref[j],
                              b2_ref[j], c2_ref[j], k2_ref[j], bu_ref.at[par], hs_ref.at[par], tc, nb)
        hre_ref[:, ss] = hr
        him_ref[:, ss] = hi
        for tp in range(2):
            yt_ref[par, tp] = y2[:, tp * LANES:(tp + 1) * LANES]
            for b in range(nb):
                y_ref[j, pl.ds(b * tc + tp, half, stride=2), :] = yt_ref[par, tp, pl.ds(b, half, stride=nb), :]
    y = _gelu(jnp.concatenate([y_ref[j] for j in range(S5_TILES)], axis=1) + d_ref[...] * u)
    branch = y * _sigmoid(_dot(y.astype(BF16), wglu_ref[...]) + bglu_ref[...])
    gate = _silu(_dot(h, win_ref[:, BRANCH + XATT:]))
    m_ref[:, 0:BRANCH] = (branch * gate[:, 0:BRANCH]).astype(BF16)
    q = _dot(h, win_ref[:, BRANCH:BRANCH + XATT])
    pieces = [(b, h) for b in range(nb) for h in range(N_XHEADS)]
    att = _attend([_head(q[b * tc:(b + 1) * tc], h) for b, h in pieces],
                  [kt_ref[b, h * XHEAD_DIM:(h + 1) * XHEAD_DIM, :] for b, h in pieces],
                  [_head(v_ref[b * N_MEM:(b + 1) * N_MEM, :], h) for b, h in pieces])
    for (b, h), a in zip(pieces, att):
        rs = slice(b * tc, (b + 1) * tc)
        cs = slice(BRANCH + h * XHEAD_DIM, BRANCH + (h + 1) * XHEAD_DIM)
        m_ref[rs, cs] = (a * gate[rs, cs]).astype(BF16)
    out = _dot(m_ref[...], wout_ref[...])
    o_ref[...] = (x + _rms(out, post_ref[...])).reshape(nb, tc, D_MODEL)


def _layer_b_prompt(xa, xb, kb, vb, win, s5, dvec, wglu, bglu, wout, pre, post, layer):
    tc = 64
    nb, seq = xa.shape[0] + xb.shape[0], xa.shape[1]
    tm = nb * tc
    xin = pl.BlockSpec((xa.shape[0], tc, D_MODEL), lambda t: (0, t, 0))
    xs = pl.BlockSpec((nb, tc, D_MODEL), lambda t: (0, t, 0))
    hs = pl.BlockSpec((nb, SSM_GROUPS * SSM_STATE), lambda t: (0, 0))
    ins = [kb, vb, win, *s5, dvec, wglu, bglu, wout, pre, post]
    stacked = [True, True] + [False] * (len(s5) + 5) + [True, True]
    return pl.pallas_call(
        _layer_b_prompt_kernel,
        grid=(seq // tc,),
        in_specs=[xin, xin] + [_layer_spec(a.shape, layer) if st else _const_spec(a.shape)
                               for a, st in zip(ins, stacked)],
        out_specs=[xs, hs, hs],
        out_shape=[jax.ShapeDtypeStruct((nb, seq, D_MODEL), F32),
                   jax.ShapeDtypeStruct((nb, SSM_GROUPS * SSM_STATE), F32),
                   jax.ShapeDtypeStruct((nb, SSM_GROUPS * SSM_STATE), F32)],
        scratch_shapes=[pltpu.VMEM((S5_TILES, tm, LANES), F32),
                        pltpu.VMEM((2, tm // 2, 2 * S5_HALF), F32),
                        pltpu.VMEM((2, tm // 2, 2 * S5_HALF), F32),
                        pltpu.VMEM((2, 2, tm // 2, LANES), F32),
                        pltpu.VMEM((S5_TILES, tm, LANES), F32),
                        pltpu.VMEM((tm, MIX_WIDTH), BF16)],
        compiler_params=pltpu.CompilerParams(
            dimension_semantics=("arbitrary",), vmem_limit_bytes=VMEM_LIMIT),
        name="layer_b_prompt",
    )(xa, xb, *ins)


Q_ROWS = N_XHEADS * DEC_SEQ


def _store_q_by_sequence(q_ref, q):
    for h in range(N_XHEADS):
        for t in range(DEC_SEQ):
            q_ref[pl.ds(h * DEC_SEQ + t, DEC_BATCH, stride=Q_ROWS), :] = _head(q[t * DEC_BATCH:(t + 1) * DEC_BATCH], h)


def _sample_pre_a_kernel(x_ref, win_ref, lng_ref, lnb_ref, wrow_ref, brow_ref, pre_ref,
                         v_ref, mbr_ref, q_ref, gatt_ref):
    nb = DEC_BATCH
    h = _rms(x_ref[...], pre_ref[...]).astype(BF16)
    gate = _silu(_dot(h, win_ref[:, 2 * BRANCH + XATT:]))
    u = _gelu(_dot(h, win_ref[:, 0:BRANCH]))
    v = _layer_norm(_gelu(_dot(h, win_ref[:, BRANCH:2 * BRANCH])), lng_ref[...], lnb_ref[...])
    v_ref[...] = v
    for t in range(DEC_SEQ):
        rs = slice(t * nb, (t + 1) * nb)
        mixed = brow_ref[t:t + 1, :]
        for s in range(t + 1):
            mixed = mixed + wrow_ref[t * DEC_SEQ + s:t * DEC_SEQ + s + 1, :] * v[s * nb:(s + 1) * nb]
        mbr_ref[rs, :] = u[rs] * mixed * gate[rs, 0:BRANCH]
    _store_q_by_sequence(q_ref, _dot(h, win_ref[:, 2 * BRANCH:2 * BRANCH + XATT]))
    gatt_ref[...] = gate[:, BRANCH:]


def _sample_pre_a(x, win, lng, lnb, wrow, brow, pre, layer):
    rows = x.shape[0]
    ins = [x, win, lng, lnb, wrow, brow, pre]
    shapes = [(rows, BRANCH), (rows, BRANCH), (DEC_BATCH * Q_ROWS, XHEAD_DIM), (rows, XATT)]
    return pl.pallas_call(
        _sample_pre_a_kernel,
        grid=(1,),
        in_specs=[_const_spec(a.shape) for a in ins[:-1]] + [_layer_spec(pre.shape, layer)],
        out_specs=[pl.BlockSpec(s, lambda i: (0, 0)) for s in shapes],
        out_shape=[jax.ShapeDtypeStruct(s, F32) for s in shapes],
        compiler_params=pltpu.CompilerParams(
            dimension_semantics=("arbitrary",), vmem_limit_bytes=VMEM_LIMIT),
        name="sample_pre_a",
    )(*ins)


def _sample_pre_b_kernel(x_ref, win_ref, sre_ref, sim_ref, a2r_ref, a2i_ref, b2_ref, c2_ref, k2_ref, d_ref,
                         wglu_ref, bglu_ref, pre_ref,
                         mbr_ref, q_ref, gatt_ref, hre_ref, him_ref,
                         bu_ref, hs_ref, y_ref):
    nb = DEC_BATCH
    h = _rms(x_ref[...], pre_ref[...]).astype(BF16)
    u = _dot(h, win_ref[:, 0:BRANCH])
    for j in range(S5_TILES):
        ss = slice(j * S5_HALF, (j + 1) * S5_HALF)
        ls = slice(j * LANES, (j + 1) * LANES)
        y2, hr, hi = _s5_tile(u[:, ls], sre_ref[:, ss], sim_ref[:, ss], a2r_ref[j], a2i_ref[j],
                              b2_ref[j], c2_ref[j], k2_ref[j], bu_ref, hs_ref, DEC_SEQ, nb)
        hre_ref[:, ss] = hr
        him_ref[:, ss] = hi
        for pair in range(DEC_SEQ // 2):
            for tp in range(2):
                t = 2 * pair + tp
                y_ref[t * nb:(t + 1) * nb, ls] = y2[pair * nb:(pair + 1) * nb, tp * LANES:(tp + 1) * LANES]
    y = _gelu(y_ref[...] + d_ref[...] * u)
    branch = y * _sigmoid(_dot(y.astype(BF16), wglu_ref[...]) + bglu_ref[...])
    gate = _silu(_dot(h, win_ref[:, BRANCH + XATT:]))
    mbr_ref[...] = branch * gate[:, 0:BRANCH]
    _store_q_by_sequence(q_ref, _dot(h, win_ref[:, BRANCH:BRANCH + XATT]))
    gatt_ref[...] = gate[:, BRANCH:]


def _sample_pre_b(x, win, sre, sim, s5, dvec, wglu, bglu, pre, layer):
    rows = x.shape[0]
    ins = [x, win, sre, sim, *s5, dvec, wglu, bglu]
    shapes = [(rows, BRANCH), (DEC_BATCH * Q_ROWS, XHEAD_DIM), (rows, XATT), sre.shape, sim.shape]
    return pl.pallas_call(
        _sample_pre_b_kernel,
        grid=(1,),
        in_specs=[_const_spec(a.shape) for a in ins] + [_layer_spec(pre.shape, layer)],
        out_specs=[pl.BlockSpec(s, lambda i: (0, 0)) for s in shapes],
        out_shape=[jax.ShapeDtypeStruct(s, F32) for s in shapes],
        scratch_shapes=[pltpu.VMEM((rows // 2, 2 * S5_HALF), F32),
                        pltpu.VMEM((rows // 2, 2 * S5_HALF), F32),
                        pltpu.VMEM((rows, BRANCH), F32)],
        compiler_params=pltpu.CompilerParams(
            dimension_semantics=("arbitrary",), vmem_limit_bytes=VMEM_LIMIT),
        name="sample_pre_b",
    )(*ins, pre)


def _sample_attn_kernel(q_ref, k_ref, v_ref, o_ref):
    bb, rows, _ = q_ref.shape
    cols = k_ref.shape[1]
    row_h = (lax.broadcasted_iota(jnp.int32, (bb * rows, cols), 0) // DEC_SEQ) % N_XHEADS
    col_h = lax.broadcasted_iota(jnp.int32, (bb * rows, cols), 1) % N_XHEADS
    s = jnp.concatenate([_dot_nt(q_ref[j].astype(BF16), k_ref[j].astype(BF16)) for j in range(bb)], axis=0)
    s = jnp.where(row_h == col_h, s * (XHEAD_DIM ** -0.5), -1e30)
    p = jnp.exp(s - jnp.max(s, axis=-1, keepdims=True))
    inv = 1.0 / jnp.sum(p, axis=-1, keepdims=True)
    pb = p.astype(BF16)
    for j in range(bb):
        rs = slice(j * rows, (j + 1) * rows)
        o_ref[j] = _dot(pb[rs], v_ref[j].astype(BF16)) * inv[rs]


def _sample_post_kernel(x_ref, mbr_ref, att_ref, gatt_ref, wout_ref, post_ref, o_ref):
    t = pl.program_id(0)
    m = jnp.concatenate(
        [(att_ref[pl.ds(h * DEC_SEQ + t, DEC_BATCH, stride=Q_ROWS), :] * _head(gatt_ref[...], h)).astype(BF16)
         for h in range(N_XHEADS)], axis=1)
    out = (_dot(mbr_ref[...].astype(BF16), wout_ref[0:BRANCH, :]) + _dot(m, wout_ref[BRANCH:, :]))
    o_ref[...] = x_ref[...] + _rms(out, post_ref[...])


def _sample_post(x, mbr, att, gatt, wout, post, layer):
    def rows_tb(n):
        return pl.BlockSpec((DEC_BATCH, n), lambda t: (t, 0))

    return pl.pallas_call(
        _sample_post_kernel,
        grid=(DEC_SEQ,),
        in_specs=[rows_tb(D_MODEL), rows_tb(BRANCH), _const_spec(att.shape), rows_tb(XATT),
                  _const_spec(wout.shape), _layer_spec(post.shape, layer)],
        out_specs=rows_tb(D_MODEL),
        out_shape=jax.ShapeDtypeStruct(x.shape, F32),
        compiler_params=pltpu.CompilerParams(
            dimension_semantics=("arbitrary",), vmem_limit_bytes=VMEM_LIMIT),
        name="sample_post",
    )(x, mbr, att, gatt, wout, post)


def _to_bt(a):
    return a.reshape(DEC_SEQ, DEC_BATCH, a.shape[-1]).transpose(1, 0, 2)


def _to_tb(a):
    return a.transpose(1, 0, 2).reshape(DEC_SEQ * DEC_BATCH, a.shape[-1])


def kernel(x_prompt, x_sample, cache_mem_k, cache_mem_v, state_ssm_re, state_ssm_im, mem_prompt,
           w_in_a, ln_v_g, ln_v_b, w_spatial, b_spatial,
           w_in_b, ssm_lambda_re, ssm_lambda_im, ssm_log_dt, ssm_b_re, ssm_b_im, ssm_c_re, ssm_c_im,
           ssm_d, w_glu, b_glu,
           mem_norm_g, w_mem_k, w_mem_v, w_out, pre_norm_g, post_norm_g):
    depth = w_out.shape[0]
    win_a = w_in_a[0].astype(BF16)
    wout0 = w_out[0].astype(BF16)
    wglu = w_glu[0].astype(BF16)
    wk = w_mem_k.astype(BF16)
    wv = w_mem_v.astype(BF16)
    pre = pre_norm_g.reshape(depth, 1, D_MODEL)
    post = post_norm_g.reshape(depth, 1, D_MODEL)
    lng = ln_v_g[0].reshape(1, BRANCH)
    lnb = ln_v_b[0].reshape(1, BRANCH)
    bglu = b_glu[0].reshape(1, BRANCH)
    dvec = ssm_d[0].reshape(1, BRANCH)
    bias = jnp.repeat(b_spatial[0].T, A_GDIM, axis=1)
    ws4 = w_spatial[0][:, :DEC_SEQ, :DEC_SEQ]
    wrow = jnp.repeat(ws4.transpose(1, 2, 0).reshape(DEC_SEQ * DEC_SEQ, A_GROUPS), A_GDIM, axis=1)
    brow = bias[:DEC_SEQ]

    s5 = _s5_prep(ssm_lambda_re[0], ssm_lambda_im[0], ssm_log_dt[0],
                  ssm_b_re[0], ssm_b_im[0], ssm_c_re[0], ssm_c_im[0])

    mem = mem_prompt.reshape(BATCH * N_MEM, D_MODEL)
    mk, mv, mkt, mvb = _mem_kv(mem, mem_norm_g.reshape(depth, 1, D_MODEL), wk, wv)

    ck = cache_mem_k.reshape(depth, DEC_BATCH, N_MEM * N_XHEADS, XHEAD_DIM)
    cv = cache_mem_v.reshape(depth, DEC_BATCH, N_MEM * N_XHEADS, XHEAD_DIM)
    q_shape = (DEC_BATCH, Q_ROWS, XHEAD_DIM)
    half = BATCH // 2
    layer_a = functools.partial(_layer_a_prompt, x_prompt, kt=mkt, vb=mvb, win=win_a, lng=lng, lnb=lnb,
                                ws=w_spatial[0], bias=bias, wout=wout0, pre=pre, post=post, layer=0,
                                cache_k=ck, cache_v=cv)
    xs = _to_tb(x_sample)
    v_rows, mbr, q, gatt = _sample_pre_a(xs, win_a, lng, lnb, wrow, brow, pre, 0)
    yp_a, att, win_b = layer_a(b0=0, nb=half, sq=q.reshape(q_shape), sample_layer=0, casts=[(w_in_b, 0)])
    xs = _sample_post(xs, mbr, att.reshape(q.shape), gatt, wout0, post, 0)
    sre = state_ssm_re[0].reshape(DEC_BATCH, SSM_GROUPS * SSM_STATE)
    sim = state_ssm_im[0].reshape(DEC_BATCH, SSM_GROUPS * SSM_STATE)
    mbr, q, gatt, hs_re, hs_im = _sample_pre_b(xs, win_b, sre, sim, s5, dvec, wglu, bglu, pre, 1)
    yp_b, att, wout1 = layer_a(b0=half, nb=half, sq=q.reshape(q_shape), sample_layer=1, casts=[(w_out, 1)])
    xs = _sample_post(xs, mbr, att.reshape(q.shape), gatt, wout1, post, 1)
    yp, hp_re, hp_im = _layer_b_prompt(yp_a, yp_b, mkt, mvb, win_b, s5, dvec, wglu, bglu, wout1, pre, post, 1)

    kv_shape = (depth, BATCH, N_MEM, N_XHEADS, XHEAD_DIM)
    st_p = (1, BATCH, SSM_GROUPS, SSM_STATE)
    st_s = (1, DEC_BATCH, SSM_GROUPS, SSM_STATE)
    return (yp, _to_bt(xs), mk.reshape(kv_shape), mv.reshape(kv_shape),
            hp_re.reshape(st_p), hp_im.reshape(st_p), hs_re.reshape(st_s), hs_im.reshape(st_s),
            _to_bt(v_rows)[None])
```

```python
import functools
import math

import jax
import jax.numpy as jnp
from jax import lax
from jax.experimental import pallas as pl
from jax.experimental.pallas import tpu as pltpu

D_MODEL = 1024
BATCH = 8
SEQ = 2048
DEC_BATCH = 128
DEC_SEQ = 4
BRANCH = 1536
N_XHEADS = 4
XHEAD_DIM = 128
XATT = N_XHEADS * XHEAD_DIM
MIX_WIDTH = BRANCH + XATT
N_MEM = 256
CHUNK = 128
A_GROUPS = 8
A_GDIM = BRANCH // A_GROUPS
SSM_GCH = 16
SSM_GROUPS = 96
SSM_STATE = 64
EPS = 1e-6

LANES = 128
S5_TILE_GROUPS = LANES // SSM_GCH
S5_TILES = SSM_GROUPS // S5_TILE_GROUPS
S5_HALF = S5_TILE_GROUPS * SSM_STATE
VMEM_LIMIT = 60 * 1024 * 1024
A_STREAM_ROWS = 256

F32 = jnp.float32
BF16 = jnp.bfloat16


def _dot(a, b):
    return jnp.dot(a, b, preferred_element_type=F32)


def _rms(x, g):
    return x * lax.rsqrt(jnp.mean(x * x, axis=-1, keepdims=True) + EPS) * g


def _layer_norm(x, g, b):
    mu = jnp.mean(x, axis=-1, keepdims=True)
    xc = x - mu
    var = jnp.mean(xc * xc, axis=-1, keepdims=True)
    return xc * lax.rsqrt(var + EPS) * g + b


def _gelu(x):
    c = math.sqrt(2.0 / math.pi)
    hx = 0.5 * x
    return hx + hx * jnp.tanh(x * (c + (c * 0.044715) * (x * x)))


def _sigmoid(x):
    return 0.5 + 0.5 * jnp.tanh(0.5 * x)


def _silu(x):
    hx = 0.5 * x
    return hx + hx * jnp.tanh(hx)


def _dot_nt(a, b):
    return lax.dot_general(a, b, (((1,), (1,)), ((), ())), preferred_element_type=F32)


def _attend(qs, kts, vs):
    r = qs[0].shape[0]
    s = jnp.concatenate([_dot(q.astype(BF16), kt) for q, kt in zip(qs, kts)], axis=0)
    s = s * (XHEAD_DIM ** -0.5)
    p = jnp.exp(s - jnp.max(s, axis=-1, keepdims=True))
    inv = 1.0 / jnp.sum(p, axis=-1, keepdims=True)
    pb = p.astype(BF16)
    return [_dot(pb[i * r:(i + 1) * r], v) * inv[i * r:(i + 1) * r] for i, v in enumerate(vs)]


def _head(a, h):
    return a[:, h * XHEAD_DIM:(h + 1) * XHEAD_DIM]


def _const_spec(shape):
    nd = len(shape)
    return pl.BlockSpec(shape, lambda *_: (0,) * nd, pipeline_mode=pl.Buffered(1))


def _layer_spec(shape, layer):
    nd = len(shape)
    return pl.BlockSpec((None,) + tuple(shape[1:]), lambda *_: (layer,) + (0,) * (nd - 1),
                        pipeline_mode=pl.Buffered(1))


def _lam_bar(lr, li, ldt):
    dt = jnp.exp(ldt)
    e = jnp.exp(lr * dt)
    return e * jnp.cos(li * dt), e * jnp.sin(li * dt)


def _s5_prep_kernel(*refs):
    for j in range(refs[0].shape[0]):
        _s5_prep_tile(*[r.at[pl.ds(j, 1)] for r in refs])


def _s5_prep_tile(lr_ref, li_ref, ldt_ref, br_ref, bi_ref, cr_ref, ci_ref,
                  a2r_ref, a2i_ref, b2_ref, c2_ref, k2_ref):
    lr = lr_ref[0]
    li = li_ref[0]
    ar, ai = _lam_bar(lr, li, ldt_ref[0])
    a2r = ar * ar - ai * ai
    a2i = 2.0 * ar * ai
    a2r_ref[0] = a2r
    a2i_ref[0] = a2i
    nr = ar - 1.0
    den = lr * lr + li * li
    kr = (nr * lr + ai * li) / den
    ki = (ai * lr - nr * li) / den
    shape = (LANES, S5_HALF)
    own = (lax.broadcasted_iota(jnp.int32, shape, 0) // SSM_GCH
           == lax.broadcasted_iota(jnp.int32, shape, 1) // SSM_STATE)

    def own_blocks(ref):
        return jnp.where(own, jnp.concatenate([ref[0]] * S5_TILE_GROUPS, axis=0), 0.0)

    br, bi, cr, ci = own_blocks(br_ref), own_blocks(bi_ref), own_blocks(cr_ref), own_blocks(ci_ref)
    bbr = kr * br - ki * bi
    bbi = kr * bi + ki * br
    b_tok1 = jnp.concatenate([bbr, bbi], axis=1).astype(BF16)
    b_tok0 = jnp.concatenate([ar * bbr - ai * bbi, ar * bbi + ai * bbr], axis=1).astype(BF16)
    b2_ref[0, 0:LANES, :] = b_tok0
    b2_ref[0, LANES:, :] = b_tok1
    c_tok0 = jnp.concatenate([cr * ar - ci * ai, -(cr * ai + ci * ar)], axis=1)
    c_tok1 = jnp.concatenate([cr * a2r - ci * a2i, -(cr * a2i + ci * a2r)], axis=1)
    c2_ref[0, :, 0:LANES] = c_tok0.T.astype(BF16)
    c2_ref[0, :, LANES:] = c_tok1.T.astype(BF16)
    c_plain = jnp.concatenate([cr, -ci], axis=1).astype(BF16)
    k0 = _dot_nt(b_tok1, c_plain).astype(BF16)
    k1 = _dot_nt(b_tok0, c_plain).astype(BF16)
    k2_ref[0, 0:LANES, 0:LANES] = k0
    k2_ref[0, 0:LANES, LANES:] = k1
    k2_ref[0, LANES:, 0:LANES] = jnp.zeros_like(k0)
    k2_ref[0, LANES:, LANES:] = k0


def _s5_prep(lam_re, lam_im, log_dt, b_re, b_im, c_re, c_im):
    nt, tg = S5_TILES, S5_TILE_GROUPS
    lr = lam_re.reshape(nt, 1, S5_HALF)
    li = lam_im.reshape(nt, 1, S5_HALF)
    ldt = jnp.broadcast_to(log_dt[:, None], (SSM_GROUPS, SSM_STATE)).reshape(nt, 1, S5_HALF)

    def b_cp(b):
        return b.reshape(nt, tg, SSM_STATE, SSM_GCH).transpose(0, 3, 1, 2).reshape(nt, SSM_GCH, S5_HALF)

    def c_cp(c):
        return c.reshape(nt, tg, SSM_GCH, SSM_STATE).transpose(0, 2, 1, 3).reshape(nt, SSM_GCH, S5_HALF)

    per_step = 3

    def spec(*shape):
        return pl.BlockSpec((per_step,) + shape, lambda j: (j, 0, 0))

    vec, chan = spec(1, S5_HALF), spec(SSM_GCH, S5_HALF)
    out_shapes = [(1, S5_HALF), (1, S5_HALF), (2 * LANES, 2 * S5_HALF), (2 * S5_HALF, 2 * LANES),
                  (2 * LANES, 2 * LANES)]
    out_dtypes = [F32, F32, BF16, BF16, BF16]
    return pl.pallas_call(
        _s5_prep_kernel,
        grid=(nt // per_step,),
        in_specs=[vec, vec, vec, chan, chan, chan, chan],
        out_specs=[spec(*s) for s in out_shapes],
        out_shape=[jax.ShapeDtypeStruct((nt,) + s, d) for s, d in zip(out_shapes, out_dtypes)],
        name="s5_prep",
    )(lr, li, ldt, b_cp(b_re), b_cp(b_im), c_cp(c_re), c_cp(c_im))


def _mem_kv_kernel(mem_ref, g_ref, wk_ref, wv_ref, k_ref, v_ref, kt_ref, vb_ref):
    mem = mem_ref[...]
    scaled = mem * lax.rsqrt(jnp.mean(mem * mem, axis=-1, keepdims=True) + EPS)
    for i in range(g_ref.shape[0]):
        m = (scaled * g_ref[i]).astype(BF16)
        k = _dot(m, wk_ref[i])
        v = _dot(m, wv_ref[i])
        vb_ref[i] = v.astype(BF16)
        for b in range(k_ref.shape[1]):
            kb = k[b * N_MEM:(b + 1) * N_MEM]
            vb = v[b * N_MEM:(b + 1) * N_MEM]
            kt_ref[i, b] = kb.T.astype(BF16)
            for h in range(N_XHEADS):
                rows = pl.ds(h, N_MEM, stride=N_XHEADS)
                k_ref[i, b, rows, :] = _head(kb, h)
                v_ref[i, b, rows, :] = _head(vb, h)


def _mem_kv(mem, g, wk, wv):
    rows = mem.shape[0]
    depth = g.shape[0]
    nb = 2
    tm = nb * N_MEM
    return pl.pallas_call(
        _mem_kv_kernel,
        grid=(rows // tm,),
        in_specs=[pl.BlockSpec((tm, D_MODEL), lambda r: (r, 0)),
                  _const_spec(g.shape), _const_spec(wk.shape), _const_spec(wv.shape)],
        out_specs=[pl.BlockSpec((depth, nb, N_MEM * N_XHEADS, XHEAD_DIM), lambda r: (0, r, 0, 0))] * 2
        + [pl.BlockSpec((depth, nb, XATT, N_MEM), lambda r: (0, r, 0, 0)),
           pl.BlockSpec((depth, tm, XATT), lambda r: (0, r, 0))],
        out_shape=[jax.ShapeDtypeStruct((depth, rows // N_MEM, N_MEM * N_XHEADS, XHEAD_DIM), F32)] * 2
        + [jax.ShapeDtypeStruct((depth, rows // N_MEM, XATT, N_MEM), BF16),
           jax.ShapeDtypeStruct((depth, rows, XATT), BF16)],
        compiler_params=pltpu.CompilerParams(
            dimension_semantics=("arbitrary",), vmem_limit_bytes=VMEM_LIMIT),
        name="mem_kv",
    )(mem, g, wk, wv)


def _masked_spatial(ws_ref):
    row = lax.broadcasted_iota(jnp.int32, (CHUNK, CHUNK), 0)
    col = lax.broadcasted_iota(jnp.int32, (CHUNK, CHUNK), 1)
    return [jnp.where(row >= col, ws_ref[g], 0.0).astype(BF16) for g in range(A_GROUPS)]


def _gate_chunk(wm, vc, pair):
    base = pair * 2 * A_GDIM
    even = _dot(wm[2 * pair], vc[:, base:base + 2 * LANES])
    odd = _dot(wm[2 * pair + 1], vc[:, base + LANES:base + 3 * LANES])
    lane = lax.broadcasted_iota(jnp.int32, (CHUNK, LANES), 1)
    mid = jnp.where(lane < A_GDIM - LANES, even[:, LANES:], odd[:, :LANES])
    return jnp.concatenate([even[:, :LANES], mid, odd[:, LANES:]], axis=1)


def _s5_tile(u, hr, hi, a2r, a2i, b2, c2, k2, bu_ref, hs_ref, steps, nb):
    pairs = steps // 2
    u3 = u.reshape(pairs, 2 * nb, LANES)
    u2 = jnp.concatenate([u3[:, 0:nb, :].reshape(pairs * nb, LANES),
                          u3[:, nb:, :].reshape(pairs * nb, LANES)], axis=1).astype(BF16)
    bu_ref[...] = _dot(u2, b2)
    a2r = jnp.broadcast_to(a2r, (nb, S5_HALF))
    a2i = jnp.broadcast_to(a2i, (nb, S5_HALF))
    for k in range(pairs):
        rows = slice(k * nb, (k + 1) * nb)
        hs_ref[rows, 0:S5_HALF] = hr
        hs_ref[rows, S5_HALF:] = hi
        hr, hi = (a2r * hr - a2i * hi + bu_ref[rows, 0:S5_HALF],
                  a2r * hi + a2i * hr + bu_ref[rows, S5_HALF:])
    y2 = _dot(hs_ref[...].astype(BF16), c2) + _dot(u2, k2)
    return y2, hr, hi


def _layer_a_prompt_kernel(*refs, n_casts):
    (x_ref, kt_ref, v_ref, win_ref, lng_ref, lnb_ref, ws_ref, bias_ref,
     wout_ref, pre_ref, post_ref, sq_ref, sk_ref, sv_ref) = refs[:14]
    cast_in = refs[14:14 + n_casts]
    o_ref, satt_ref = refs[14 + n_casts:16 + n_casts]
    cast_out = refs[16 + n_casts:16 + 2 * n_casts]
    m_ref, g_ref = refs[16 + 2 * n_casts:]
    _sample_attn_kernel(sq_ref, sk_ref, sv_ref, satt_ref)
    for src, dst in zip(cast_in, cast_out):
        dst[...] = src[...].astype(BF16)
    wm = _masked_spatial(ws_ref)
    heads = range(N_XHEADS)
    n_streams = x_ref.shape[0] // A_STREAM_ROWS

    def project(s):
        x = x_ref[pl.ds(s * A_STREAM_ROWS, A_STREAM_ROWS), :]
        h = _rms(x, pre_ref[...]).astype(BF16)
        v = _layer_norm(_gelu(_dot(h, win_ref[:, BRANCH:2 * BRANCH])), lng_ref[...], lnb_ref[...])
        u = _gelu(_dot(h, win_ref[:, 0:BRANCH]))
        gate = _silu(_dot(h, win_ref[:, 2 * BRANCH + XATT:]))
        g_ref[pl.ds(s * A_STREAM_ROWS, A_STREAM_ROWS), 0:BRANCH] = u * gate[:, 0:BRANCH]
        g_ref[pl.ds(s * A_STREAM_ROWS, A_STREAM_ROWS), BRANCH:] = gate[:, BRANCH:]
        q = _dot(h, win_ref[:, 2 * BRANCH:2 * BRANCH + XATT])
        return x, v.astype(BF16), q

    def mix_and_emit(s, x, vb, q):
        m = m_ref.at[pl.ds(s * A_STREAM_ROWS, A_STREAM_ROWS)]
        g = g_ref.at[pl.ds(s * A_STREAM_ROWS, A_STREAM_ROWS)]
        for c in range(A_STREAM_ROWS // CHUNK):
            rs = slice(c * CHUNK, (c + 1) * CHUNK)
            vc = vb[rs]
            for pair in range(A_GROUPS // 2):
                cs = slice(pair * 2 * A_GDIM, (pair + 1) * 2 * A_GDIM)
                mixed = _gate_chunk(wm, vc, pair) + bias_ref[:, cs]
                m[rs, cs] = (g[rs, cs] * mixed).astype(BF16)
        kv = v_ref[...]
        att = _attend([_head(q, h) for h in heads],
                      [kt_ref[h * XHEAD_DIM:(h + 1) * XHEAD_DIM, :] for h in heads],
                      [_head(kv, h) for h in heads])
        for h in heads:
            cs = slice(BRANCH + h * XHEAD_DIM, BRANCH + (h + 1) * XHEAD_DIM)
            m[:, cs] = (att[h] * g[:, cs]).astype(BF16)
        out = _dot(m[...], wout_ref[...])
        o_ref[pl.ds(s * A_STREAM_ROWS, A_STREAM_ROWS), :] = x + _rms(out, post_ref[...])

    pending = project(0)
    for s in range(n_streams):
        following = project(s + 1) if s + 1 < n_streams else None
        mix_and_emit(s, *pending)
        pending = following


def _layer_a_prompt(x, b0, nb, kt, vb, win, lng, lnb, ws, bias, wout, pre, post, layer,
                    sq, cache_k, cache_v, sample_layer, casts=()):
    tm = 512
    seq = x.shape[1]
    nt = seq // tm
    steps = nb * nt
    bb = sq.shape[0] // steps
    cast_specs = [pl.BlockSpec((None, w.shape[1] // steps, w.shape[2]),
                               functools.partial(lambda b, t, l: (l, b * nt + t, 0), l=l))
                  for w, l in casts]
    cast_out_specs = [pl.BlockSpec((w.shape[1] // steps, w.shape[2]), lambda b, t: (b * nt + t, 0))
                      for w, _ in casts]
    xin = pl.BlockSpec((None, tm, D_MODEL), lambda b, t: (b + b0, t, 0))
    xout = pl.BlockSpec((None, tm, D_MODEL), lambda b, t: (b, t, 0))
    kts = pl.BlockSpec((None, None, XATT, N_MEM), lambda b, t: (layer, b + b0, 0, 0))
    vs = pl.BlockSpec((None, N_MEM, XATT), lambda b, t: (layer, b + b0, 0))
    sqs = pl.BlockSpec((bb,) + sq.shape[1:], lambda b, t: (b * nt + t, 0, 0))
    scs = pl.BlockSpec((None, bb) + cache_k.shape[2:], lambda b, t: (sample_layer, b * nt + t, 0, 0))
    return pl.pallas_call(
        functools.partial(_layer_a_prompt_kernel, n_casts=len(casts)),
        grid=(nb, nt),
        in_specs=[xin, kts, vs, _const_spec(win.shape), _const_spec(lng.shape), _const_spec(lnb.shape),
                  _const_spec(ws.shape), _const_spec(bias.shape), _const_spec(wout.shape),
                  _layer_spec(pre.shape, layer), _layer_spec(post.shape, layer), sqs, scs, scs] + cast_specs,
        out_specs=[xout, sqs] + cast_out_specs,
        out_shape=[jax.ShapeDtypeStruct((nb, seq, D_MODEL), F32), jax.ShapeDtypeStruct(sq.shape, F32)]
        + [jax.ShapeDtypeStruct(w.shape[1:], BF16) for w, _ in casts],
        scratch_shapes=[pltpu.VMEM((tm, MIX_WIDTH), BF16), pltpu.VMEM((tm, MIX_WIDTH), F32)],
        compiler_params=pltpu.CompilerParams(
            dimension_semantics=("arbitrary", "arbitrary"), vmem_limit_bytes=VMEM_LIMIT),
        name="layer_a_prompt",
    )(x, kt, vb, win, lng, lnb, ws, bias, wout, pre, post, sq, cache_k, cache_v, *[w for w, _ in casts])


def _layer_b_prompt_kernel(xa_ref, xb_ref, kt_ref, v_ref, win_ref, a2r_ref, a2i_ref, b2_ref, c2_ref, k2_ref,
                           d_ref, wglu_ref, bglu_ref, wout_ref, pre_ref, post_ref,
                           o_ref, hre_ref, him_ref,
                           ut_ref, bu_ref, hs_ref, yt_ref, y_ref, m_ref):
    nb, tc, _ = o_ref.shape
    tm = nb * tc
    half = tc // 2

    @pl.when(pl.program_id(0) == 0)
    def _():
        hre_ref[...] = jnp.zeros_like(hre_ref)
        him_ref[...] = jnp.zeros_like(him_ref)

    x = jnp.concatenate([xa_ref[...], xb_ref[...]], axis=0).reshape(tm, D_MODEL)
    h = _rms(x, pre_ref[...]).astype(BF16)
    u = _dot(h, win_ref[:, 0:BRANCH])
    for j in range(S5_TILES):
        for b in range(nb):
            ut_ref[j, pl.ds(b, tc, stride=nb), :] = u[b * tc:(b + 1) * tc, j * LANES:(j + 1) * LANES]
    for j in range(S5_TILES):
        ss = slice(j * S5_HALF, (j + 1) * S5_HALF)
        par = j % 2
        y2, hr, hi = _s5_tile(ut_ref[j], hre_ref[:, ss], him_ref[:, ss], a2r_ref[j], a2i_ref[j],
                              b2_ref[j], c2_ref[j], k2_ref[j], bu_ref.at[par], hs_ref.at[par], tc, nb)
        hre_ref[:, ss] = hr
        him_ref[:, ss] = hi
        for tp in range(2):
            yt_ref[par, tp] = y2[:, tp * LANES:(tp + 1) * LANES]
            for b in range(nb):
                y_ref[j, pl.ds(b * tc + tp, half, stride=2), :] = yt_ref[par, tp, pl.ds(b, half, stride=nb), :]
    y = _gelu(jnp.concatenate([y_ref[j] for j in range(S5_TILES)], axis=1) + d_ref[...] * u)
    branch = y * _sigmoid(_dot(y.astype(BF16), wglu_ref[...]) + bglu_ref[...])
    gate = _silu(_dot(h, win_ref[:, BRANCH + XATT:]))
    m_ref[:, 0:BRANCH] = (branch * gate[:, 0:BRANCH]).astype(BF16)
    q = _dot(h, win_ref[:, BRANCH:BRANCH + XATT])
    pieces = [(b, h) for b in range(nb) for h in range(N_XHEADS)]
    att = _attend([_head(q[b * tc:(b + 1) * tc], h) for b, h in pieces],
                  [kt_ref[b, h * XHEAD_DIM:(h + 1) * XHEAD_DIM, :] for b, h in pieces],
                  [_head(v_ref[b * N_MEM:(b + 1) * N_MEM, :], h) for b, h in pieces])
    for (b, h), a in zip(pieces, att):
        rs = slice(b * tc, (b + 1) * tc)
        cs = slice(BRANCH + h * XHEAD_DIM, BRANCH + (h + 1) * XHEAD_DIM)
        m_ref[rs, cs] = (a * gate[rs, cs]).astype(BF16)
    out = _dot(m_ref[...], wout_ref[...])
    o_ref[...] = (x + _rms(out, post_ref[...])).reshape(nb, tc, D_MODEL)


def _layer_b_prompt(xa, xb, kb, vb, win, s5, dvec, wglu, bglu, wout, pre, post, layer):
    tc = 64
    nb, seq = xa.shape[0] + xb.shape[0], xa.shape[1]
    tm = nb * tc
    xin = pl.BlockSpec((xa.shape[0], tc, D_MODEL), lambda t: (0, t, 0))
    xs = pl.BlockSpec((nb, tc, D_MODEL), lambda t: (0, t, 0))
    hs = pl.BlockSpec((nb, SSM_GROUPS * SSM_STATE), lambda t: (0, 0))
    ins = [kb, vb, win, *s5, dvec, wglu, bglu, wout, pre, post]
    stacked = [True, True] + [False] * (len(s5) + 5) + [True, True]
    return pl.pallas_call(
        _layer_b_prompt_kernel,
        grid=(seq // tc,),
        in_specs=[xin, xin] + [_layer_spec(a.shape, layer) if st else _const_spec(a.shape)
                               for a, st in zip(ins, stacked)],
        out_specs=[xs, hs, hs],
        out_shape=[jax.ShapeDtypeStruct((nb, seq, D_MODEL), F32),
                   jax.ShapeDtypeStruct((nb, SSM_GROUPS * SSM_STATE), F32),
                   jax.ShapeDtypeStruct((nb, SSM_GROUPS * SSM_STATE), F32)],
        scratch_shapes=[pltpu.VMEM((S5_TILES, tm, LANES), F32),
                        pltpu.VMEM((2, tm // 2, 2 * S5_HALF), F32),
                        pltpu.VMEM((2, tm // 2, 2 * S5_HALF), F32),
                        pltpu.VMEM((2, 2, tm // 2, LANES), F32),
                        pltpu.VMEM((S5_TILES, tm, LANES), F32),
                        pltpu.VMEM((tm, MIX_WIDTH), BF16)],
        compiler_params=pltpu.CompilerParams(
            dimension_semantics=("arbitrary",), vmem_limit_bytes=VMEM_LIMIT),
        name="layer_b_prompt",
    )(xa, xb, *ins)


Q_ROWS = N_XHEADS * DEC_SEQ


def _store_q_by_sequence(q_ref, q):
    for h in range(N_XHEADS):
        for t in range(DEC_SEQ):
            q_ref[pl.ds(h * DEC_SEQ + t, DEC_BATCH, stride=Q_ROWS), :] = _head(q[t * DEC_BATCH:(t + 1) * DEC_BATCH], h)


def _sample_pre_a_kernel(x_ref, win_ref, lng_ref, lnb_ref, wrow_ref, brow_ref, pre_ref,
                         v_ref, mbr_ref, q_ref, gatt_ref):
    nb = DEC_BATCH
    h = _rms(x_ref[...], pre_ref[...]).astype(BF16)
    gate = _silu(_dot(h, win_ref[:, 2 * BRANCH + XATT:]))
    u = _gelu(_dot(h, win_ref[:, 0:BRANCH]))
    v = _layer_norm(_gelu(_dot(h, win_ref[:, BRANCH:2 * BRANCH])), lng_ref[...], lnb_ref[...])
    v_ref[...] = v
    for t in range(DEC_SEQ):
        rs = slice(t * nb, (t + 1) * nb)
        mixed = brow_ref[t:t + 1, :]
        for s in range(t + 1):
            mixed = mixed + wrow_ref[t * DEC_SEQ + s:t * DEC_SEQ + s + 1, :] * v[s * nb:(s + 1) * nb]
        mbr_ref[rs, :] = u[rs] * mixed * gate[rs, 0:BRANCH]
    _store_q_by_sequence(q_ref, _dot(h, win_ref[:, 2 * BRANCH:2 * BRANCH + XATT]))
    gatt_ref[...] = gate[:, BRANCH:]


def _sample_pre_a(x, win, lng, lnb, wrow, brow, pre, layer):
    rows = x.shape[0]
    ins = [x, win, lng, lnb, wrow, brow, pre]
    shapes = [(rows, BRANCH), (rows, BRANCH), (DEC_BATCH * Q_ROWS, XHEAD_DIM), (rows, XATT)]
    return pl.pallas_call(
        _sample_pre_a_kernel,
        grid=(1,),
        in_specs=[_const_spec(a.shape) for a in ins[:-1]] + [_layer_spec(pre.shape, layer)],
        out_specs=[pl.BlockSpec(s, lambda i: (0, 0)) for s in shapes],
        out_shape=[jax.ShapeDtypeStruct(s, F32) for s in shapes],
        compiler_params=pltpu.CompilerParams(
            dimension_semantics=("arbitrary",), vmem_limit_bytes=VMEM_LIMIT),
        name="sample_pre_a",
    )(*ins)


def _sample_pre_b_kernel(x_ref, win_ref, sre_ref, sim_ref, a2r_ref, a2i_ref, b2_ref, c2_ref, k2_ref, d_ref,
                         wglu_ref, bglu_ref, pre_ref,
                         mbr_ref, q_ref, gatt_ref, hre_ref, him_ref,
                         bu_ref, hs_ref, u_ref, y_ref):
    nb = DEC_BATCH
    step = pl.program_id(0)
    per_step = a2r_ref.shape[0]

    @pl.when(step == 0)
    def _():
        h = _rms(x_ref[...], pre_ref[...]).astype(BF16)
        u = _dot(h, win_ref[:, 0:BRANCH])
        for j in range(S5_TILES):
            u_ref[j] = u[:, j * LANES:(j + 1) * LANES]
        gate = _silu(_dot(h, win_ref[:, BRANCH + XATT:]))
        mbr_ref[...] = gate[:, 0:BRANCH]
        gatt_ref[...] = gate[:, BRANCH:]
        _store_q_by_sequence(q_ref, _dot(h, win_ref[:, BRANCH:BRANCH + XATT]))

    for jj in range(per_step):
        j = step * per_step + jj
        ss = slice(jj * S5_HALF, (jj + 1) * S5_HALF)
        y2, hr, hi = _s5_tile(u_ref[j], sre_ref[:, ss], sim_ref[:, ss], a2r_ref[jj], a2i_ref[jj],
                              b2_ref[jj], c2_ref[jj], k2_ref[jj], bu_ref, hs_ref, DEC_SEQ, nb)
        hre_ref[:, ss] = hr
        him_ref[:, ss] = hi
        for pair in range(DEC_SEQ // 2):
            for tp in range(2):
                t = 2 * pair + tp
                y_ref[j, t * nb:(t + 1) * nb, :] = y2[pair * nb:(pair + 1) * nb, tp * LANES:(tp + 1) * LANES]

    @pl.when(step == pl.num_programs(0) - 1)
    def _():
        u = jnp.concatenate([u_ref[j] for j in range(S5_TILES)], axis=1)
        y = _gelu(jnp.concatenate([y_ref[j] for j in range(S5_TILES)], axis=1) + d_ref[...] * u)
        branch = y * _sigmoid(_dot(y.astype(BF16), wglu_ref[...]) + bglu_ref[...])
        mbr_ref[...] = branch * mbr_ref[...]


def _sample_pre_b(x, win, sre, sim, s5, dvec, wglu, bglu, pre, layer):
    rows = x.shape[0]
    per_step = 2
    a2r, a2i, b2, c2, k2 = s5

    def tiles(a):
        return pl.BlockSpec((per_step,) + a.shape[1:], lambda s: (s, 0, 0))

    state = pl.BlockSpec((DEC_BATCH, per_step * S5_HALF), lambda s: (0, s))
    shapes = [(rows, BRANCH), (DEC_BATCH * Q_ROWS, XHEAD_DIM), (rows, XATT)]
    return pl.pallas_call(
        _sample_pre_b_kernel,
        grid=(S5_TILES // per_step,),
        in_specs=[_const_spec(x.shape), _const_spec(win.shape), state, state,
                  tiles(a2r), tiles(a2i), tiles(b2), tiles(c2), tiles(k2),
                  _const_spec(dvec.shape), _const_spec(wglu.shape), _const_spec(bglu.shape),
                  _layer_spec(pre.shape, layer)],
        out_specs=[pl.BlockSpec(s, lambda i: (0, 0)) for s in shapes] + [state, state],
        out_shape=[jax.ShapeDtypeStruct(s, F32) for s in shapes]
        + [jax.ShapeDtypeStruct(sre.shape, F32), jax.ShapeDtypeStruct(sim.shape, F32)],
        scratch_shapes=[pltpu.VMEM((rows // 2, 2 * S5_HALF), F32),
                        pltpu.VMEM((rows // 2, 2 * S5_HALF), F32),
                        pltpu.VMEM((S5_TILES, rows, LANES), F32),
                        pltpu.VMEM((S5_TILES, rows, LANES), F32)],
        compiler_params=pltpu.CompilerParams(
            dimension_semantics=("arbitrary",), vmem_limit_bytes=VMEM_LIMIT),
        name="sample_pre_b",
    )(x, win, sre, sim, a2r, a2i, b2, c2, k2, dvec, wglu, bglu, pre)


def _sample_attn_kernel(q_ref, k_ref, v_ref, o_ref):
    bb, rows, _ = q_ref.shape
    cols = k_ref.shape[1]
    row_h = (lax.broadcasted_iota(jnp.int32, (bb * rows, cols), 0) // DEC_SEQ) % N_XHEADS
    col_h = lax.broadcasted_iota(jnp.int32, (bb * rows, cols), 1) % N_XHEADS
    s = jnp.concatenate([_dot_nt(q_ref[j].astype(BF16), k_ref[j].astype(BF16)) for j in range(bb)], axis=0)
    s = jnp.where(row_h == col_h, s * (XHEAD_DIM ** -0.5), -1e30)
    p = jnp.exp(s - jnp.max(s, axis=-1, keepdims=True))
    inv = 1.0 / jnp.sum(p, axis=-1, keepdims=True)
    pb = p.astype(BF16)
    for j in range(bb):
        rs = slice(j * rows, (j + 1) * rows)
        o_ref[j] = _dot(pb[rs], v_ref[j].astype(BF16)) * inv[rs]


def _sample_post_kernel(x_ref, mbr_ref, att_ref, gatt_ref, wout_ref, post_ref, o_ref):
    t = pl.program_id(0)
    m = jnp.concatenate(
        [(att_ref[pl.ds(h * DEC_SEQ + t, DEC_BATCH, stride=Q_ROWS), :] * _head(gatt_ref[...], h)).astype(BF16)
         for h in range(N_XHEADS)], axis=1)
    out = (_dot(mbr_ref[...].astype(BF16), wout_ref[0:BRANCH, :]) + _dot(m, wout_ref[BRANCH:, :]))
    o_ref[...] = x_ref[...] + _rms(out, post_ref[...])


def _sample_post(x, mbr, att, gatt, wout, post, layer):
    def rows_tb(n):
        return pl.BlockSpec((DEC_BATCH, n), lambda t: (t, 0))

    return pl.pallas_call(
        _sample_post_kernel,
        grid=(DEC_SEQ,),
        in_specs=[rows_tb(D_MODEL), rows_tb(BRANCH), _const_spec(att.shape), rows_tb(XATT),
                  _const_spec(wout.shape), _layer_spec(post.shape, layer)],
        out_specs=rows_tb(D_MODEL),
        out_shape=jax.ShapeDtypeStruct(x.shape, F32),
        compiler_params=pltpu.CompilerParams(
            dimension_semantics=("arbitrary",), vmem_limit_bytes=VMEM_LIMIT),
        name="sample_post",
    )(x, mbr, att, gatt, wout, post)


def _to_bt(a):
    return a.reshape(DEC_SEQ, DEC_BATCH, a.shape[-1]).transpose(1, 0, 2)


def _to_tb(a):
    return a.transpose(1, 0, 2).reshape(DEC_SEQ * DEC_BATCH, a.shape[-1])


def kernel(x_prompt, x_sample, cache_mem_k, cache_mem_v, state_ssm_re, state_ssm_im, mem_prompt,
           w_in_a, ln_v_g, ln_v_b, w_spatial, b_spatial,
           w_in_b, ssm_lambda_re, ssm_lambda_im, ssm_log_dt, ssm_b_re, ssm_b_im, ssm_c_re, ssm_c_im,
           ssm_d, w_glu, b_glu,
           mem_norm_g, w_mem_k, w_mem_v, w_out, pre_norm_g, post_norm_g):
    depth = w_out.shape[0]
    win_a = w_in_a[0].astype(BF16)
    wout0 = w_out[0].astype(BF16)
    wglu = w_glu[0].astype(BF16)
    wk = w_mem_k.astype(BF16)
    wv = w_mem_v.astype(BF16)
    pre = pre_norm_g.reshape(depth, 1, D_MODEL)
    post = post_norm_g.reshape(depth, 1, D_MODEL)
    lng = ln_v_g[0].reshape(1, BRANCH)
    lnb = ln_v_b[0].reshape(1, BRANCH)
    bglu = b_glu[0].reshape(1, BRANCH)
    dvec = ssm_d[0].reshape(1, BRANCH)
    bias = jnp.repeat(b_spatial[0].T, A_GDIM, axis=1)
    ws4 = w_spatial[0][:, :DEC_SEQ, :DEC_SEQ]
    wrow = jnp.repeat(ws4.transpose(1, 2, 0).reshape(DEC_SEQ * DEC_SEQ, A_GROUPS), A_GDIM, axis=1)
    brow = bias[:DEC_SEQ]

    s5 = _s5_prep(ssm_lambda_re[0], ssm_lambda_im[0], ssm_log_dt[0],
                  ssm_b_re[0], ssm_b_im[0], ssm_c_re[0], ssm_c_im[0])

    mem = mem_prompt.reshape(BATCH * N_MEM, D_MODEL)
    mk, mv, mkt, mvb = _mem_kv(mem, mem_norm_g.reshape(depth, 1, D_MODEL), wk, wv)

    ck = cache_mem_k.reshape(depth, DEC_BATCH, N_MEM * N_XHEADS, XHEAD_DIM)
    cv = cache_mem_v.reshape(depth, DEC_BATCH, N_MEM * N_XHEADS, XHEAD_DIM)
    q_shape = (DEC_BATCH, Q_ROWS, XHEAD_DIM)
    half = BATCH // 2
    layer_a = functools.partial(_layer_a_prompt, x_prompt, kt=mkt, vb=mvb, win=win_a, lng=lng, lnb=lnb,
                                ws=w_spatial[0], bias=bias, wout=wout0, pre=pre, post=post, layer=0,
                                cache_k=ck, cache_v=cv)
    xs = _to_tb(x_sample)
    v_rows, mbr, q, gatt = _sample_pre_a(xs, win_a, lng, lnb, wrow, brow, pre, 0)
    yp_a, att, win_b = layer_a(b0=0, nb=half, sq=q.reshape(q_shape), sample_layer=0, casts=[(w_in_b, 0)])
    xs = _sample_post(xs, mbr, att.reshape(q.shape), gatt, wout0, post, 0)
    sre = state_ssm_re[0].reshape(DEC_BATCH, SSM_GROUPS * SSM_STATE)
    sim = state_ssm_im[0].reshape(DEC_BATCH, SSM_GROUPS * SSM_STATE)
    mbr, q, gatt, hs_re, hs_im = _sample_pre_b(xs, win_b, sre, sim, s5, dvec, wglu, bglu, pre, 1)
    yp_b, att, wout1 = layer_a(b0=half, nb=half, sq=q.reshape(q_shape), sample_layer=1, casts=[(w_out, 1)])
    xs = _sample_post(xs, mbr, att.reshape(q.shape), gatt, wout1, post, 1)
    yp, hp_re, hp_im = _layer_b_prompt(yp_a, yp_b, mkt, mvb, win_b, s5, dvec, wglu, bglu, wout1, pre, post, 1)

    kv_shape = (depth, BATCH, N_MEM, N_XHEADS, XHEAD_DIM)
    st_p = (1, BATCH, SSM_GROUPS, SSM_STATE)
    st_s = (1, DEC_BATCH, SSM_GROUPS, SSM_STATE)
    return (yp, _to_bt(xs), mk.reshape(kv_shape), mv.reshape(kv_shape),
            hp_re.reshape(st_p), hp_im.reshape(st_p), hs_re.reshape(st_s), hs_im.reshape(st_s),
            _to_bt(v_rows)[None])
```

```python
import functools
import math

import jax
import jax.numpy as jnp
from jax import lax
from jax.experimental import pallas as pl
from jax.experimental.pallas import tpu as pltpu

D_MODEL = 1024
BATCH = 8
SEQ = 2048
DEC_BATCH = 128
DEC_SEQ = 4
BRANCH = 1536
N_XHEADS = 4
XHEAD_DIM = 128
XATT = N_XHEADS * XHEAD_DIM
MIX_WIDTH = BRANCH + XATT
N_MEM = 256
CHUNK = 128
A_GROUPS = 8
A_GDIM = BRANCH // A_GROUPS
SSM_GCH = 16
SSM_GROUPS = 96
SSM_STATE = 64
EPS = 1e-6

LANES = 128
S5_TILE_GROUPS = LANES // SSM_GCH
S5_TILES = SSM_GROUPS // S5_TILE_GROUPS
S5_HALF = S5_TILE_GROUPS * SSM_STATE
VMEM_LIMIT = 60 * 1024 * 1024
A_STREAM_ROWS = 256

F32 = jnp.float32
BF16 = jnp.bfloat16


def _dot(a, b):
    return jnp.dot(a, b, preferred_element_type=F32)


def _rms(x, g):
    return x * lax.rsqrt(jnp.mean(x * x, axis=-1, keepdims=True) + EPS) * g


def _layer_norm(x, g, b):
    mu = jnp.mean(x, axis=-1, keepdims=True)
    xc = x - mu
    var = jnp.mean(xc * xc, axis=-1, keepdims=True)
    return xc * lax.rsqrt(var + EPS) * g + b


def _gelu(x):
    c = math.sqrt(2.0 / math.pi)
    hx = 0.5 * x
    return hx + hx * jnp.tanh(x * (c + (c * 0.044715) * (x * x)))


def _sigmoid(x):
    return 0.5 + 0.5 * jnp.tanh(0.5 * x)


def _silu(x):
    hx = 0.5 * x
    return hx + hx * jnp.tanh(hx)


def _dot_nt(a, b):
    return lax.dot_general(a, b, (((1,), (1,)), ((), ())), preferred_element_type=F32)


def _attend(qs, kts, vs):
    r = qs[0].shape[0]
    s = jnp.concatenate([_dot(q.astype(BF16), kt) for q, kt in zip(qs, kts)], axis=0)
    s = s * (XHEAD_DIM ** -0.5)
    p = jnp.exp(s - jnp.max(s, axis=-1, keepdims=True))
    inv = 1.0 / jnp.sum(p, axis=-1, keepdims=True)
    pb = p.astype(BF16)
    return [_dot(pb[i * r:(i + 1) * r], v) * inv[i * r:(i + 1) * r] for i, v in enumerate(vs)]


def _head(a, h):
    return a[:, h * XHEAD_DIM:(h + 1) * XHEAD_DIM]


def _const_spec(shape):
    nd = len(shape)
    return pl.BlockSpec(shape, lambda *_: (0,) * nd, pipeline_mode=pl.Buffered(1))


def _layer_spec(shape, layer):
    nd = len(shape)
    return pl.BlockSpec((None,) + tuple(shape[1:]), lambda *_: (layer,) + (0,) * (nd - 1),
                        pipeline_mode=pl.Buffered(1))


def _lam_bar(lr, li, ldt):
    dt = jnp.exp(ldt)
    e = jnp.exp(lr * dt)
    return e * jnp.cos(li * dt), e * jnp.sin(li * dt)


def _s5_prep_kernel(*refs):
    for j in range(refs[0].shape[0]):
        _s5_prep_tile(*[r.at[pl.ds(j, 1)] for r in refs])


def _s5_prep_tile(lr_ref, li_ref, ldt_ref, br_ref, bi_ref, cr_ref, ci_ref,
                  a2r_ref, a2i_ref, b2_ref, c2_ref, k2_ref):
    lr = lr_ref[0]
    li = li_ref[0]
    ar, ai = _lam_bar(lr, li, ldt_ref[0])
    a2r = ar * ar - ai * ai
    a2i = 2.0 * ar * ai
    a2r_ref[0] = a2r
    a2i_ref[0] = a2i
    nr = ar - 1.0
    den = lr * lr + li * li
    kr = (nr * lr + ai * li) / den
    ki = (ai * lr - nr * li) / den
    shape = (LANES, S5_HALF)
    own = (lax.broadcasted_iota(jnp.int32, shape, 0) // SSM_GCH
           == lax.broadcasted_iota(jnp.int32, shape, 1) // SSM_STATE)

    def own_blocks(ref):
        return jnp.where(own, jnp.concatenate([ref[0]] * S5_TILE_GROUPS, axis=0), 0.0)

    br, bi, cr, ci = own_blocks(br_ref), own_blocks(bi_ref), own_blocks(cr_ref), own_blocks(ci_ref)
    bbr = kr * br - ki * bi
    bbi = kr * bi + ki * br
    b_tok1 = jnp.concatenate([bbr, bbi], axis=1).astype(BF16)
    b_tok0 = jnp.concatenate([ar * bbr - ai * bbi, ar * bbi + ai * bbr], axis=1).astype(BF16)
    b2_ref[0, 0:LANES, :] = b_tok0
    b2_ref[0, LANES:, :] = b_tok1
    c_tok0 = jnp.concatenate([cr * ar - ci * ai, -(cr * ai + ci * ar)], axis=1)
    c_tok1 = jnp.concatenate([cr * a2r - ci * a2i, -(cr * a2i + ci * a2r)], axis=1)
    c2_ref[0, :, 0:LANES] = c_tok0.T.astype(BF16)
    c2_ref[0, :, LANES:] = c_tok1.T.astype(BF16)
    c_plain = jnp.concatenate([cr, -ci], axis=1).astype(BF16)
    k0 = _dot_nt(b_tok1, c_plain).astype(BF16)
    k1 = _dot_nt(b_tok0, c_plain).astype(BF16)
    k2_ref[0, 0:LANES, 0:LANES] = k0
    k2_ref[0, 0:LANES, LANES:] = k1
    k2_ref[0, LANES:, 0:LANES] = jnp.zeros_like(k0)
    k2_ref[0, LANES:, LANES:] = k0


def _s5_prep(lam_re, lam_im, log_dt, b_re, b_im, c_re, c_im):
    nt, tg = S5_TILES, S5_TILE_GROUPS
    lr = lam_re.reshape(nt, 1, S5_HALF)
    li = lam_im.reshape(nt, 1, S5_HALF)
    ldt = jnp.broadcast_to(log_dt[:, None], (SSM_GROUPS, SSM_STATE)).reshape(nt, 1, S5_HALF)

    def b_cp(b):
        return b.reshape(nt, tg, SSM_STATE, SSM_GCH).transpose(0, 3, 1, 2).reshape(nt, SSM_GCH, S5_HALF)

    def c_cp(c):
        return c.reshape(nt, tg, SSM_GCH, SSM_STATE).transpose(0, 2, 1, 3).reshape(nt, SSM_GCH, S5_HALF)

    per_step = 3

    def spec(*shape):
        return pl.BlockSpec((per_step,) + shape, lambda j: (j, 0, 0))

    vec, chan = spec(1, S5_HALF), spec(SSM_GCH, S5_HALF)
    out_shapes = [(1, S5_HALF), (1, S5_HALF), (2 * LANES, 2 * S5_HALF), (2 * S5_HALF, 2 * LANES),
                  (2 * LANES, 2 * LANES)]
    out_dtypes = [F32, F32, BF16, BF16, BF16]
    return pl.pallas_call(
        _s5_prep_kernel,
        grid=(nt // per_step,),
        in_specs=[vec, vec, vec, chan, chan, chan, chan],
        out_specs=[spec(*s) for s in out_shapes],
        out_shape=[jax.ShapeDtypeStruct((nt,) + s, d) for s, d in zip(out_shapes, out_dtypes)],
        name="s5_prep",
    )(lr, li, ldt, b_cp(b_re), b_cp(b_im), c_cp(c_re), c_cp(c_im))


def _mem_kv_kernel(mem_ref, g_ref, wk_ref, wv_ref, k_ref, v_ref, kt_ref, vb_ref):
    mem = mem_ref[...]
    scaled = mem * lax.rsqrt(jnp.mean(mem * mem, axis=-1, keepdims=True) + EPS)
    for i in range(g_ref.shape[0]):
        m = (scaled * g_ref[i]).astype(BF16)
        k = _dot(m, wk_ref[i])
        v = _dot(m, wv_ref[i])
        vb_ref[i] = v.astype(BF16)
        for b in range(k_ref.shape[1]):
            kb = k[b * N_MEM:(b + 1) * N_MEM]
            vb = v[b * N_MEM:(b + 1) * N_MEM]
            kt_ref[i, b] = kb.T.astype(BF16)
            for h in range(N_XHEADS):
                rows = pl.ds(h, N_MEM, stride=N_XHEADS)
                k_ref[i, b, rows, :] = _head(kb, h)
                v_ref[i, b, rows, :] = _head(vb, h)


def _mem_kv(mem, g, wk, wv):
    rows = mem.shape[0]
    depth = g.shape[0]
    nb = 2
    tm = nb * N_MEM
    return pl.pallas_call(
        _mem_kv_kernel,
        grid=(rows // tm,),
        in_specs=[pl.BlockSpec((tm, D_MODEL), lambda r: (r, 0)),
                  _const_spec(g.shape), _const_spec(wk.shape), _const_spec(wv.shape)],
        out_specs=[pl.BlockSpec((depth, nb, N_MEM * N_XHEADS, XHEAD_DIM), lambda r: (0, r, 0, 0))] * 2
        + [pl.BlockSpec((depth, nb, XATT, N_MEM), lambda r: (0, r, 0, 0)),
           pl.BlockSpec((depth, tm, XATT), lambda r: (0, r, 0))],
        out_shape=[jax.ShapeDtypeStruct((depth, rows // N_MEM, N_MEM * N_XHEADS, XHEAD_DIM), F32)] * 2
        + [jax.ShapeDtypeStruct((depth, rows // N_MEM, XATT, N_MEM), BF16),
           jax.ShapeDtypeStruct((depth, rows, XATT), BF16)],
        compiler_params=pltpu.CompilerParams(
            dimension_semantics=("arbitrary",), vmem_limit_bytes=VMEM_LIMIT),
        name="mem_kv",
    )(mem, g, wk, wv)


def _masked_spatial(ws_ref):
    row = lax.broadcasted_iota(jnp.int32, (CHUNK, CHUNK), 0)
    col = lax.broadcasted_iota(jnp.int32, (CHUNK, CHUNK), 1)
    return [jnp.where(row >= col, ws_ref[g], 0.0).astype(BF16) for g in range(A_GROUPS)]


def _gate_chunk(wm, vc, pair):
    base = pair * 2 * A_GDIM
    even = _dot(wm[2 * pair], vc[:, base:base + 2 * LANES])
    odd = _dot(wm[2 * pair + 1], vc[:, base + LANES:base + 3 * LANES])
    lane = lax.broadcasted_iota(jnp.int32, (CHUNK, LANES), 1)
    mid = jnp.where(lane < A_GDIM - LANES, even[:, LANES:], odd[:, :LANES])
    return jnp.concatenate([even[:, :LANES], mid, odd[:, LANES:]], axis=1)


def _s5_tile(u, hr, hi, a2r, a2i, b2, c2, k2, bu_ref, hs_ref, steps, nb):
    pairs = steps // 2
    u3 = u.reshape(pairs, 2 * nb, LANES)
    u2 = jnp.concatenate([u3[:, 0:nb, :].reshape(pairs * nb, LANES),
                          u3[:, nb:, :].reshape(pairs * nb, LANES)], axis=1).astype(BF16)
    bu_ref[...] = _dot(u2, b2)
    a2r = jnp.broadcast_to(a2r, (nb, S5_HALF))
    a2i = jnp.broadcast_to(a2i, (nb, S5_HALF))
    for k in range(pairs):
        rows = slice(k * nb, (k + 1) * nb)
        hs_ref[rows, 0:S5_HALF] = hr
        hs_ref[rows, S5_HALF:] = hi
        hr, hi = (a2r * hr - a2i * hi + bu_ref[rows, 0:S5_HALF],
                  a2r * hi + a2i * hr + bu_ref[rows, S5_HALF:])
    y2 = _dot(hs_ref[...].astype(BF16), c2) + _dot(u2, k2)
    return y2, hr, hi


def _layer_a_prompt_kernel(*refs, n_casts):
    (x_ref, kt_ref, v_ref, win_ref, lng_ref, lnb_ref, ws_ref, bias_ref,
     wout_ref, pre_ref, post_ref, sq_ref, sk_ref, sv_ref) = refs[:14]
    cast_in = refs[14:14 + n_casts]
    o_ref, satt_ref = refs[14 + n_casts:16 + n_casts]
    cast_out = refs[16 + n_casts:16 + 2 * n_casts]
    m_ref, g_ref = refs[16 + 2 * n_casts:]
    _sample_attn_kernel(sq_ref, sk_ref, sv_ref, satt_ref)
    for src, dst in zip(cast_in, cast_out):
        dst[...] = src[...].astype(BF16)
    wm = _masked_spatial(ws_ref)
    heads = range(N_XHEADS)
    n_streams = x_ref.shape[0] // A_STREAM_ROWS

    def project(s):
        x = x_ref[pl.ds(s * A_STREAM_ROWS, A_STREAM_ROWS), :]
        h = _rms(x, pre_ref[...]).astype(BF16)
        v = _layer_norm(_gelu(_dot(h, win_ref[:, BRANCH:2 * BRANCH])), lng_ref[...], lnb_ref[...])
        u = _gelu(_dot(h, win_ref[:, 0:BRANCH]))
        gate = _silu(_dot(h, win_ref[:, 2 * BRANCH + XATT:]))
        g_ref[pl.ds(s * A_STREAM_ROWS, A_STREAM_ROWS), 0:BRANCH] = u * gate[:, 0:BRANCH]
        g_ref[pl.ds(s * A_STREAM_ROWS, A_STREAM_ROWS), BRANCH:] = gate[:, BRANCH:]
        q = _dot(h, win_ref[:, 2 * BRANCH:2 * BRANCH + XATT])
        return x, v.astype(BF16), q

    def mix_and_emit(s, x, vb, q):
        m = m_ref.at[pl.ds(s * A_STREAM_ROWS, A_STREAM_ROWS)]
        g = g_ref.at[pl.ds(s * A_STREAM_ROWS, A_STREAM_ROWS)]
        for c in range(A_STREAM_ROWS // CHUNK):
            rs = slice(c * CHUNK, (c + 1) * CHUNK)
            vc = vb[rs]
            for pair in range(A_GROUPS // 2):
                cs = slice(pair * 2 * A_GDIM, (pair + 1) * 2 * A_GDIM)
                mixed = _gate_chunk(wm, vc, pair) + bias_ref[:, cs]
                m[rs, cs] = (g[rs, cs] * mixed).astype(BF16)
        kv = v_ref[...]
        att = _attend([_head(q, h) for h in heads],
                      [kt_ref[h * XHEAD_DIM:(h + 1) * XHEAD_DIM, :] for h in heads],
                      [_head(kv, h) for h in heads])
        for h in heads:
            cs = slice(BRANCH + h * XHEAD_DIM, BRANCH + (h + 1) * XHEAD_DIM)
            m[:, cs] = (att[h] * g[:, cs]).astype(BF16)
        out = _dot(m[...], wout_ref[...])
        o_ref[pl.ds(s * A_STREAM_ROWS, A_STREAM_ROWS), :] = x + _rms(out, post_ref[...])

    pending = project(0)
    for s in range(n_streams):
        following = project(s + 1) if s + 1 < n_streams else None
        mix_and_emit(s, *pending)
        pending = following


def _layer_a_prompt(x, b0, nb, kt, vb, win, lng, lnb, ws, bias, wout, pre, post, layer,
                    sq, cache_k, cache_v, sample_layer, casts=()):
    tm = 512
    seq = x.shape[1]
    nt = seq // tm
    steps = nb * nt
    bb = sq.shape[0] // steps
    cast_specs = [pl.BlockSpec((None, w.shape[1] // steps, w.shape[2]),
                               functools.partial(lambda b, t, l: (l, b * nt + t, 0), l=l))
                  for w, l in casts]
    cast_out_specs = [pl.BlockSpec((w.shape[1] // steps, w.shape[2]), lambda b, t: (b * nt + t, 0))
                      for w, _ in casts]
    xin = pl.BlockSpec((None, tm, D_MODEL), lambda b, t: (b + b0, t, 0))
    xout = pl.BlockSpec((None, tm, D_MODEL), lambda b, t: (b, t, 0))
    kts = pl.BlockSpec((None, None, XATT, N_MEM), lambda b, t: (layer, b + b0, 0, 0))
    vs = pl.BlockSpec((None, N_MEM, XATT), lambda b, t: (layer, b + b0, 0))
    sqs = pl.BlockSpec((bb,) + sq.shape[1:], lambda b, t: (b * nt + t, 0, 0))
    scs = pl.BlockSpec((None, bb) + cache_k.shape[2:], lambda b, t: (sample_layer, b * nt + t, 0, 0))
    return pl.pallas_call(
        functools.partial(_layer_a_prompt_kernel, n_casts=len(casts)),
        grid=(nb, nt),
        in_specs=[xin, kts, vs, _const_spec(win.shape), _const_spec(lng.shape), _const_spec(lnb.shape),
                  _const_spec(ws.shape), _const_spec(bias.shape), _const_spec(wout.shape),
                  _layer_spec(pre.shape, layer), _layer_spec(post.shape, layer), sqs, scs, scs] + cast_specs,
        out_specs=[xout, sqs] + cast_out_specs,
        out_shape=[jax.ShapeDtypeStruct((nb, seq, D_MODEL), F32), jax.ShapeDtypeStruct(sq.shape, F32)]
        + [jax.ShapeDtypeStruct(w.shape[1:], BF16) for w, _ in casts],
        scratch_shapes=[pltpu.VMEM((tm, MIX_WIDTH), BF16), pltpu.VMEM((tm, MIX_WIDTH), F32)],
        compiler_params=pltpu.CompilerParams(
            dimension_semantics=("arbitrary", "arbitrary"), vmem_limit_bytes=VMEM_LIMIT),
        name="layer_a_prompt",
    )(x, kt, vb, win, lng, lnb, ws, bias, wout, pre, post, sq, cache_k, cache_v, *[w for w, _ in casts])


def _layer_b_prompt_kernel(xa_ref, xb_ref, kt_ref, v_ref, win_ref, a2r_ref, a2i_ref, b2_ref, c2_ref, k2_ref,
                           d_ref, wglu_ref, bglu_ref, wout_ref, pre_ref, post_ref,
                           o_ref, hre_ref, him_ref,
                           ut_ref, bu_ref, hs_ref, yt_ref, y_ref, m_ref):
    nb, tc, _ = o_ref.shape
    tm = nb * tc
    half = tc // 2

    @pl.when(pl.program_id(0) == 0)
    def _():
        hre_ref[...] = jnp.zeros_like(hre_ref)
        him_ref[...] = jnp.zeros_like(him_ref)

    x = jnp.concatenate([xa_ref[...], xb_ref[...]], axis=0).reshape(tm, D_MODEL)
    h = _rms(x, pre_ref[...]).astype(BF16)
    u = _dot(h, win_ref[:, 0:BRANCH])
    for j in range(S5_TILES):
        for b in range(nb):
            ut_ref[j, pl.ds(b, tc, stride=nb), :] = u[b * tc:(b + 1) * tc, j * LANES:(j + 1) * LANES]
    for j in range(S5_TILES):
        ss = slice(j * S5_HALF, (j + 1) * S5_HALF)
        par = j % 2
        y2, hr, hi = _s5_tile(ut_ref[j], hre_ref[:, ss], him_ref[:, ss], a2r_ref[j], a2i_ref[j],
                              b2_ref[j], c2_ref[j], k2_ref[j], bu_ref.at[par], hs_ref.at[par], tc, nb)
        hre_ref[:, ss] = hr
        him_ref[:, ss] = hi
        for tp in range(2):
            yt_ref[par, tp] = y2[:, tp * LANES:(tp + 1) * LANES]
            for b in range(nb):
                y_ref[j, pl.ds(b * tc + tp, half, stride=2), :] = yt_ref[par, tp, pl.ds(b, half, stride=nb), :]
    y = _gelu(jnp.concatenate([y_ref[j] for j in range(S5_TILES)], axis=1) + d_ref[...] * u)
    branch = y * _sigmoid(_dot(y.astype(BF16), wglu_ref[...]) + bglu_ref[...])
    gate = _silu(_dot(h, win_ref[:, BRANCH + XATT:]))
    m_ref[:, 0:BRANCH] = (branch * gate[:, 0:BRANCH]).astype(BF16)
    q = _dot(h, win_ref[:, BRANCH:BRANCH + XATT])
    pieces = [(b, h) for b in range(nb) for h in range(N_XHEADS)]
    att = _attend([_head(q[b * tc:(b + 1) * tc], h) for b, h in pieces],
                  [kt_ref[b, h * XHEAD_DIM:(h + 1) * XHEAD_DIM, :] for b, h in pieces],
                  [_head(v_ref[b * N_MEM:(b + 1) * N_MEM, :], h) for b, h in pieces])
    for (b, h), a in zip(pieces, att):
        rs = slice(b * tc, (b + 1) * tc)
        cs = slice(BRANCH + h * XHEAD_DIM, BRANCH + (h + 1) * XHEAD_DIM)
        m_ref[rs, cs] = (a * gate[rs, cs]).astype(BF16)
    out = _dot(m_ref[...], wout_ref[...])
    o_ref[...] = (x + _rms(out, post_ref[...])).reshape(nb, tc, D_MODEL)


def _layer_b_prompt(xa, xb, kb, vb, win, s5, dvec, wglu, bglu, wout, pre, post, layer):
    tc = 64
    nb, seq = xa.shape[0] + xb.shape[0], xa.shape[1]
    tm = nb * tc
    xin = pl.BlockSpec((xa.shape[0], tc, D_MODEL), lambda t: (0, t, 0))
    xs = pl.BlockSpec((nb, tc, D_MODEL), lambda t: (0, t, 0))
    hs = pl.BlockSpec((nb, SSM_GROUPS * SSM_STATE), lambda t: (0, 0))
    ins = [kb, vb, win, *s5, dvec, wglu, bglu, wout, pre, post]
    stacked = [True, True] + [False] * (len(s5) + 5) + [True, True]
    return pl.pallas_call(
        _layer_b_prompt_kernel,
        grid=(seq // tc,),
        in_specs=[xin, xin] + [_layer_spec(a.shape, layer) if st else _const_spec(a.shape)
                               for a, st in zip(ins, stacked)],
        out_specs=[xs, hs, hs],
        out_shape=[jax.ShapeDtypeStruct((nb, seq, D_MODEL), F32),
                   jax.ShapeDtypeStruct((nb, SSM_GROUPS * SSM_STATE), F32),
                   jax.ShapeDtypeStruct((nb, SSM_GROUPS * SSM_STATE), F32)],
        scratch_shapes=[pltpu.VMEM((S5_TILES, tm, LANES), F32),
                        pltpu.VMEM((2, tm // 2, 2 * S5_HALF), F32),
                        pltpu.VMEM((2, tm // 2, 2 * S5_HALF), F32),
                        pltpu.VMEM((2, 2, tm // 2, LANES), F32),
                        pltpu.VMEM((S5_TILES, tm, LANES), F32),
                        pltpu.VMEM((tm, MIX_WIDTH), BF16)],
        compiler_params=pltpu.CompilerParams(
            dimension_semantics=("arbitrary",), vmem_limit_bytes=VMEM_LIMIT),
        name="layer_b_prompt",
    )(xa, xb, *ins)


Q_ROWS = N_XHEADS * DEC_SEQ


def _store_q_by_sequence(q_ref, q):
    for h in range(N_XHEADS):
        for t in range(DEC_SEQ):
            q_ref[pl.ds(h * DEC_SEQ + t, DEC_BATCH, stride=Q_ROWS), :] = _head(q[t * DEC_BATCH:(t + 1) * DEC_BATCH], h)


def _sample_pre_a_kernel(x_ref, win_ref, lng_ref, lnb_ref, wrow_ref, brow_ref, pre_ref,
                         v_ref, mbr_ref, q_ref, gatt_ref):
    nb = DEC_BATCH
    x = jnp.concatenate([x_ref[:, t, :] for t in range(DEC_SEQ)], axis=0)
    h = _rms(x, pre_ref[...]).astype(BF16)
    gate = _silu(_dot(h, win_ref[:, 2 * BRANCH + XATT:]))
    u = _gelu(_dot(h, win_ref[:, 0:BRANCH]))
    v = _layer_norm(_gelu(_dot(h, win_ref[:, BRANCH:2 * BRANCH])), lng_ref[...], lnb_ref[...])
    for t in range(DEC_SEQ):
        v_ref[:, t, :] = v[t * nb:(t + 1) * nb]
    for t in range(DEC_SEQ):
        rs = slice(t * nb, (t + 1) * nb)
        mixed = brow_ref[t:t + 1, :]
        for s in range(t + 1):
            mixed = mixed + wrow_ref[t * DEC_SEQ + s:t * DEC_SEQ + s + 1, :] * v[s * nb:(s + 1) * nb]
        mbr_ref[rs, :] = u[rs] * mixed * gate[rs, 0:BRANCH]
    _store_q_by_sequence(q_ref, _dot(h, win_ref[:, 2 * BRANCH:2 * BRANCH + XATT]))
    gatt_ref[...] = gate[:, BRANCH:]


def _sample_pre_a(x, win, lng, lnb, wrow, brow, pre, layer):
    rows = DEC_SEQ * DEC_BATCH
    ins = [x, win, lng, lnb, wrow, brow, pre]
    shapes = [(DEC_BATCH, DEC_SEQ, BRANCH), (rows, BRANCH), (DEC_BATCH * Q_ROWS, XHEAD_DIM), (rows, XATT)]
    return pl.pallas_call(
        _sample_pre_a_kernel,
        grid=(1,),
        in_specs=[_const_spec(a.shape) for a in ins[:-1]] + [_layer_spec(pre.shape, layer)],
        out_specs=[pl.BlockSpec(s, functools.partial(lambda i, n: (0,) * n, n=len(s))) for s in shapes],
        out_shape=[jax.ShapeDtypeStruct(s, F32) for s in shapes],
        compiler_params=pltpu.CompilerParams(
            dimension_semantics=("arbitrary",), vmem_limit_bytes=VMEM_LIMIT),
        name="sample_pre_a",
    )(*ins)


def _sample_pre_b_kernel(x_ref, win_ref, sre_ref, sim_ref, a2r_ref, a2i_ref, b2_ref, c2_ref, k2_ref, d_ref,
                         wglu_ref, bglu_ref, pre_ref,
                         mbr_ref, q_ref, gatt_ref, hre_ref, him_ref,
                         bu_ref, hs_ref, y_ref):
    nb = DEC_BATCH
    h = _rms(x_ref[...], pre_ref[...]).astype(BF16)
    u = _dot(h, win_ref[:, 0:BRANCH])
    for j in range(S5_TILES):
        ss = slice(j * S5_HALF, (j + 1) * S5_HALF)
        ls = slice(j * LANES, (j + 1) * LANES)
        y2, hr, hi = _s5_tile(u[:, ls], sre_ref[:, ss], sim_ref[:, ss], a2r_ref[j], a2i_ref[j],
                              b2_ref[j], c2_ref[j], k2_ref[j], bu_ref, hs_ref, DEC_SEQ, nb)
        hre_ref[:, ss] = hr
        him_ref[:, ss] = hi
        for pair in range(DEC_SEQ // 2):
            for tp in range(2):
                t = 2 * pair + tp
                y_ref[t * nb:(t + 1) * nb, ls] = y2[pair * nb:(pair + 1) * nb, tp * LANES:(tp + 1) * LANES]
    y = _gelu(y_ref[...] + d_ref[...] * u)
    branch = y * _sigmoid(_dot(y.astype(BF16), wglu_ref[...]) + bglu_ref[...])
    gate = _silu(_dot(h, win_ref[:, BRANCH + XATT:]))
    mbr_ref[...] = branch * gate[:, 0:BRANCH]
    _store_q_by_sequence(q_ref, _dot(h, win_ref[:, BRANCH:BRANCH + XATT]))
    gatt_ref[...] = gate[:, BRANCH:]


def _sample_pre_b(x, win, sre, sim, s5, dvec, wglu, bglu, pre, layer):
    rows = x.shape[0]
    ins = [x, win, sre, sim, *s5, dvec, wglu, bglu]
    shapes = [(rows, BRANCH), (DEC_BATCH * Q_ROWS, XHEAD_DIM), (rows, XATT), sre.shape, sim.shape]
    return pl.pallas_call(
        _sample_pre_b_kernel,
        grid=(1,),
        in_specs=[_const_spec(a.shape) for a in ins] + [_layer_spec(pre.shape, layer)],
        out_specs=[pl.BlockSpec(s, lambda i: (0, 0)) for s in shapes],
        out_shape=[jax.ShapeDtypeStruct(s, F32) for s in shapes],
        scratch_shapes=[pltpu.VMEM((rows // 2, 2 * S5_HALF), F32),
                        pltpu.VMEM((rows // 2, 2 * S5_HALF), F32),
                        pltpu.VMEM((rows, BRANCH), F32)],
        compiler_params=pltpu.CompilerParams(
            dimension_semantics=("arbitrary",), vmem_limit_bytes=VMEM_LIMIT),
        name="sample_pre_b",
    )(*ins, pre)


def _sample_attn_kernel(q_ref, k_ref, v_ref, o_ref):
    bb, rows, _ = q_ref.shape
    cols = k_ref.shape[1]
    row_h = (lax.broadcasted_iota(jnp.int32, (bb * rows, cols), 0) // DEC_SEQ) % N_XHEADS
    col_h = lax.broadcasted_iota(jnp.int32, (bb * rows, cols), 1) % N_XHEADS
    s = jnp.concatenate([_dot_nt(q_ref[j].astype(BF16), k_ref[j].astype(BF16)) for j in range(bb)], axis=0)
    s = jnp.where(row_h == col_h, s * (XHEAD_DIM ** -0.5), -1e30)
    p = jnp.exp(s - jnp.max(s, axis=-1, keepdims=True))
    inv = 1.0 / jnp.sum(p, axis=-1, keepdims=True)
    pb = p.astype(BF16)
    for j in range(bb):
        rs = slice(j * rows, (j + 1) * rows)
        o_ref[j] = _dot(pb[rs], v_ref[j].astype(BF16)) * inv[rs]


def _sample_post_kernel(x_ref, mbr_ref, att_ref, gatt_ref, wout_ref, post_ref, o_ref, m_ref):
    nb = DEC_BATCH
    for h in range(N_XHEADS):
        cs = slice(h * XHEAD_DIM, (h + 1) * XHEAD_DIM)
        for t in range(DEC_SEQ):
            rs = slice(t * nb, (t + 1) * nb)
            att = att_ref[pl.ds(h * DEC_SEQ + t, nb, stride=Q_ROWS), :]
            m_ref[rs, cs] = (att * gatt_ref[rs, cs]).astype(BF16)
    out = (_dot(mbr_ref[...].astype(BF16), wout_ref[0:BRANCH, :])
           + _dot(m_ref[...], wout_ref[BRANCH:, :]))
    y = _rms(out, post_ref[...])
    for t in range(DEC_SEQ):
        rs = slice(t * nb, (t + 1) * nb)
        x = x_ref[:, t, :] if len(x_ref.shape) == 3 else x_ref[rs, :]
        if len(o_ref.shape) == 3:
            o_ref[:, t, :] = x + y[rs]
        else:
            o_ref[rs, :] = x + y[rs]


def _sample_post(x, mbr, att, gatt, wout, post, layer, out_shape):
    ins = [x, mbr, att, gatt, wout, post]
    n_out = len(out_shape)
    return pl.pallas_call(
        _sample_post_kernel,
        grid=(1,),
        in_specs=[_const_spec(a.shape) for a in ins[:5]] + [_layer_spec(post.shape, layer)],
        out_specs=pl.BlockSpec(out_shape, lambda i: (0,) * n_out),
        out_shape=jax.ShapeDtypeStruct(out_shape, F32),
        scratch_shapes=[pltpu.VMEM((mbr.shape[0], XATT), BF16)],
        compiler_params=pltpu.CompilerParams(
            dimension_semantics=("arbitrary",), vmem_limit_bytes=VMEM_LIMIT),
        name="sample_post",
    )(*ins)


def kernel(x_prompt, x_sample, cache_mem_k, cache_mem_v, state_ssm_re, state_ssm_im, mem_prompt,
           w_in_a, ln_v_g, ln_v_b, w_spatial, b_spatial,
           w_in_b, ssm_lambda_re, ssm_lambda_im, ssm_log_dt, ssm_b_re, ssm_b_im, ssm_c_re, ssm_c_im,
           ssm_d, w_glu, b_glu,
           mem_norm_g, w_mem_k, w_mem_v, w_out, pre_norm_g, post_norm_g):
    depth = w_out.shape[0]
    win_a = w_in_a[0].astype(BF16)
    wout0 = w_out[0].astype(BF16)
    wglu = w_glu[0].astype(BF16)
    wk = w_mem_k.astype(BF16)
    wv = w_mem_v.astype(BF16)
    pre = pre_norm_g.reshape(depth, 1, D_MODEL)
    post = post_norm_g.reshape(depth, 1, D_MODEL)
    lng = ln_v_g[0].reshape(1, BRANCH)
    lnb = ln_v_b[0].reshape(1, BRANCH)
    bglu = b_glu[0].reshape(1, BRANCH)
    dvec = ssm_d[0].reshape(1, BRANCH)
    bias = jnp.repeat(b_spatial[0].T, A_GDIM, axis=1)
    ws4 = w_spatial[0][:, :DEC_SEQ, :DEC_SEQ]
    wrow = jnp.repeat(ws4.transpose(1, 2, 0).reshape(DEC_SEQ * DEC_SEQ, A_GROUPS), A_GDIM, axis=1)
    brow = bias[:DEC_SEQ]

    s5 = _s5_prep(ssm_lambda_re[0], ssm_lambda_im[0], ssm_log_dt[0],
                  ssm_b_re[0], ssm_b_im[0], ssm_c_re[0], ssm_c_im[0])

    mem = mem_prompt.reshape(BATCH * N_MEM, D_MODEL)
    mk, mv, mkt, mvb = _mem_kv(mem, mem_norm_g.reshape(depth, 1, D_MODEL), wk, wv)

    ck = cache_mem_k.reshape(depth, DEC_BATCH, N_MEM * N_XHEADS, XHEAD_DIM)
    cv = cache_mem_v.reshape(depth, DEC_BATCH, N_MEM * N_XHEADS, XHEAD_DIM)
    q_shape = (DEC_BATCH, Q_ROWS, XHEAD_DIM)
    half = BATCH // 2
    layer_a = functools.partial(_layer_a_prompt, x_prompt, kt=mkt, vb=mvb, win=win_a, lng=lng, lnb=lnb,
                                ws=w_spatial[0], bias=bias, wout=wout0, pre=pre, post=post, layer=0,
                                cache_k=ck, cache_v=cv)
    rows_tb = (DEC_SEQ * DEC_BATCH, D_MODEL)
    v_rows, mbr, q, gatt = _sample_pre_a(x_sample, win_a, lng, lnb, wrow, brow, pre, 0)
    yp_a, att, win_b = layer_a(b0=0, nb=half, sq=q.reshape(q_shape), sample_layer=0, casts=[(w_in_b, 0)])
    xs = _sample_post(x_sample, mbr, att.reshape(q.shape), gatt, wout0, post, 0, rows_tb)
    sre = state_ssm_re[0].reshape(DEC_BATCH, SSM_GROUPS * SSM_STATE)
    sim = state_ssm_im[0].reshape(DEC_BATCH, SSM_GROUPS * SSM_STATE)
    mbr, q, gatt, hs_re, hs_im = _sample_pre_b(xs, win_b, sre, sim, s5, dvec, wglu, bglu, pre, 1)
    yp_b, att, wout1 = layer_a(b0=half, nb=half, sq=q.reshape(q_shape), sample_layer=1, casts=[(w_out, 1)])
    ys = _sample_post(xs, mbr, att.reshape(q.shape), gatt, wout1, post, 1, x_sample.shape)
    yp, hp_re, hp_im = _layer_b_prompt(yp_a, yp_b, mkt, mvb, win_b, s5, dvec, wglu, bglu, wout1, pre, post, 1)

    kv_shape = (depth, BATCH, N_MEM, N_XHEADS, XHEAD_DIM)
    st_p = (1, BATCH, SSM_GROUPS, SSM_STATE)
    st_s = (1, DEC_BATCH, SSM_GROUPS, SSM_STATE)
    return (yp, ys, mk.reshape(kv_shape), mv.reshape(kv_shape),
            hp_re.reshape(st_p), hp_im.reshape(st_p), hs_re.reshape(st_s), hs_im.reshape(st_s),
            v_rows[None])
```

```python
import functools
import math

import jax
import jax.numpy as jnp
from jax import lax
from jax.experimental import pallas as pl
from jax.experimental.pallas import tpu as pltpu

D_MODEL = 1024
BATCH = 8
SEQ = 2048
DEC_BATCH = 128
DEC_SEQ = 4
BRANCH = 1536
N_XHEADS = 4
XHEAD_DIM = 128
XATT = N_XHEADS * XHEAD_DIM
MIX_WIDTH = BRANCH + XATT
N_MEM = 256
CHUNK = 128
A_GROUPS = 8
A_GDIM = BRANCH // A_GROUPS
SSM_GCH = 16
SSM_GROUPS = 96
SSM_STATE = 64
EPS = 1e-6

LANES = 128
S5_TILE_GROUPS = LANES // SSM_GCH
S5_TILES = SSM_GROUPS // S5_TILE_GROUPS
S5_HALF = S5_TILE_GROUPS * SSM_STATE
VMEM_LIMIT = 60 * 1024 * 1024
A_STREAM_ROWS = 256

F32 = jnp.float32
BF16 = jnp.bfloat16


def _dot(a, b):
    return jnp.dot(a, b, preferred_element_type=F32)


def _rms(x, g):
    return x * lax.rsqrt(jnp.mean(x * x, axis=-1, keepdims=True) + EPS) * g


def _layer_norm(x, g, b):
    mu = jnp.mean(x, axis=-1, keepdims=True)
    xc = x - mu
    var = jnp.mean(xc * xc, axis=-1, keepdims=True)
    return xc * lax.rsqrt(var + EPS) * g + b


def _gelu(x):
    c = math.sqrt(2.0 / math.pi)
    hx = 0.5 * x
    return hx + hx * jnp.tanh(x * (c + (c * 0.044715) * (x * x)))


def _sigmoid(x):
    return 0.5 + 0.5 * jnp.tanh(0.5 * x)


def _silu(x):
    hx = 0.5 * x
    return hx + hx * jnp.tanh(hx)


def _dot_nt(a, b):
    return lax.dot_general(a, b, (((1,), (1,)), ((), ())), preferred_element_type=F32)


def _attend(qs, kts, vs):
    r = qs[0].shape[0]
    s = jnp.concatenate([_dot(q.astype(BF16), kt) for q, kt in zip(qs, kts)], axis=0)
    s = s * (XHEAD_DIM ** -0.5)
    p = jnp.exp(s - jnp.max(s, axis=-1, keepdims=True))
    inv = 1.0 / jnp.sum(p, axis=-1, keepdims=True)
    pb = p.astype(BF16)
    return [_dot(pb[i * r:(i + 1) * r], v) * inv[i * r:(i + 1) * r] for i, v in enumerate(vs)]


def _head(a, h):
    return a[:, h * XHEAD_DIM:(h + 1) * XHEAD_DIM]


def _const_spec(shape):
    nd = len(shape)
    return pl.BlockSpec(shape, lambda *_: (0,) * nd, pipeline_mode=pl.Buffered(1))


def _layer_spec(shape, layer):
    nd = len(shape)
    return pl.BlockSpec((None,) + tuple(shape[1:]), lambda *_: (layer,) + (0,) * (nd - 1),
                        pipeline_mode=pl.Buffered(1))


def _lam_bar(lr, li, ldt):
    dt = jnp.exp(ldt)
    e = jnp.exp(lr * dt)
    return e * jnp.cos(li * dt), e * jnp.sin(li * dt)


def _cast_rider_specs(casts, steps):
    in_specs = [pl.BlockSpec((None, w.shape[1] // steps, w.shape[2]),
                             functools.partial(lambda i, l: (l, i, 0), l=l)) for w, l in casts]
    out_specs = [pl.BlockSpec((w.shape[1] // steps, w.shape[2]), lambda i: (i, 0)) for w, _ in casts]
    out_shape = [jax.ShapeDtypeStruct(w.shape[1:], BF16) for w, _ in casts]
    return in_specs, out_specs, out_shape


def _s5_prep_kernel(*refs, n_casts):
    n_in, n_out = 7, 5
    ins, cast_in = refs[:n_in], refs[n_in:n_in + n_casts]
    outs, cast_out = refs[n_in + n_casts:n_in + n_casts + n_out], refs[n_in + n_casts + n_out:]
    for src, dst in zip(cast_in, cast_out):
        dst[...] = src[...].astype(BF16)
    for j in range(ins[0].shape[0]):
        _s5_prep_tile(*[r.at[pl.ds(j, 1)] for r in (*ins, *outs)])


def _s5_prep_tile(lr_ref, li_ref, ldt_ref, br_ref, bi_ref, cr_ref, ci_ref,
                  a2r_ref, a2i_ref, b2_ref, c2_ref, k2_ref):
    lr = lr_ref[0]
    li = li_ref[0]
    ar, ai = _lam_bar(lr, li, ldt_ref[0])
    a2r = ar * ar - ai * ai
    a2i = 2.0 * ar * ai
    a2r_ref[0] = a2r
    a2i_ref[0] = a2i
    nr = ar - 1.0
    den = lr * lr + li * li
    kr = (nr * lr + ai * li) / den
    ki = (ai * lr - nr * li) / den
    shape = (LANES, S5_HALF)
    own = (lax.broadcasted_iota(jnp.int32, shape, 0) // SSM_GCH
           == lax.broadcasted_iota(jnp.int32, shape, 1) // SSM_STATE)

    def own_blocks(ref):
        return jnp.where(own, jnp.concatenate([ref[0]] * S5_TILE_GROUPS, axis=0), 0.0)

    br, bi, cr, ci = own_blocks(br_ref), own_blocks(bi_ref), own_blocks(cr_ref), own_blocks(ci_ref)
    bbr = kr * br - ki * bi
    bbi = kr * bi + ki * br
    b_tok1 = jnp.concatenate([bbr, bbi], axis=1).astype(BF16)
    b_tok0 = jnp.concatenate([ar * bbr - ai * bbi, ar * bbi + ai * bbr], axis=1).astype(BF16)
    b2_ref[0, 0:LANES, :] = b_tok0
    b2_ref[0, LANES:, :] = b_tok1
    c_tok0 = jnp.concatenate([cr * ar - ci * ai, -(cr * ai + ci * ar)], axis=1)
    c_tok1 = jnp.concatenate([cr * a2r - ci * a2i, -(cr * a2i + ci * a2r)], axis=1)
    c2_ref[0, :, 0:LANES] = c_tok0.T.astype(BF16)
    c2_ref[0, :, LANES:] = c_tok1.T.astype(BF16)
    c_plain = jnp.concatenate([cr, -ci], axis=1).astype(BF16)
    k0 = _dot_nt(b_tok1, c_plain).astype(BF16)
    k1 = _dot_nt(b_tok0, c_plain).astype(BF16)
    k2_ref[0, 0:LANES, 0:LANES] = k0
    k2_ref[0, 0:LANES, LANES:] = k1
    k2_ref[0, LANES:, 0:LANES] = jnp.zeros_like(k0)
    k2_ref[0, LANES:, LANES:] = k0


def _s5_prep(lam_re, lam_im, log_dt, b_re, b_im, c_re, c_im, casts=()):
    nt, tg = S5_TILES, S5_TILE_GROUPS
    lr = lam_re.reshape(nt, 1, S5_HALF)
    li = lam_im.reshape(nt, 1, S5_HALF)
    ldt = jnp.broadcast_to(log_dt[:, None], (SSM_GROUPS, SSM_STATE)).reshape(nt, 1, S5_HALF)

    def b_cp(b):
        return b.reshape(nt, tg, SSM_STATE, SSM_GCH).transpose(0, 3, 1, 2).reshape(nt, SSM_GCH, S5_HALF)

    def c_cp(c):
        return c.reshape(nt, tg, SSM_GCH, SSM_STATE).transpose(0, 2, 1, 3).reshape(nt, SSM_GCH, S5_HALF)

    per_step = 3

    def spec(*shape):
        return pl.BlockSpec((per_step,) + shape, lambda j: (j, 0, 0))

    vec, chan = spec(1, S5_HALF), spec(SSM_GCH, S5_HALF)
    out_shapes = [(1, S5_HALF), (1, S5_HALF), (2 * LANES, 2 * S5_HALF), (2 * S5_HALF, 2 * LANES),
                  (2 * LANES, 2 * LANES)]
    out_dtypes = [F32, F32, BF16, BF16, BF16]
    steps = nt // per_step
    cast_in, cast_out, cast_shape = _cast_rider_specs(casts, steps)
    return pl.pallas_call(
        functools.partial(_s5_prep_kernel, n_casts=len(casts)),
        grid=(steps,),
        in_specs=[vec, vec, vec, chan, chan, chan, chan] + cast_in,
        out_specs=[spec(*s) for s in out_shapes] + cast_out,
        out_shape=[jax.ShapeDtypeStruct((nt,) + s, d) for s, d in zip(out_shapes, out_dtypes)] + cast_shape,
        compiler_params=pltpu.CompilerParams(
            dimension_semantics=("arbitrary",), vmem_limit_bytes=VMEM_LIMIT),
        name="s5_prep",
    )(lr, li, ldt, b_cp(b_re), b_cp(b_im), c_cp(c_re), c_cp(c_im), *[w for w, _ in casts])


def _mem_kv_kernel(mem_ref, g_ref, wk_ref, wv_ref, *rest, n_casts):
    cast_in, (k_ref, v_ref, kt_ref, vb_ref), cast_out = rest[:n_casts], rest[n_casts:n_casts + 4], rest[n_casts + 4:]
    for src, dst in zip(cast_in, cast_out):
        dst[...] = src[...].astype(BF16)
    mem = mem_ref[...]
    scaled = mem * lax.rsqrt(jnp.mean(mem * mem, axis=-1, keepdims=True) + EPS)
    for i in range(g_ref.shape[0]):
        m = (scaled * g_ref[i]).astype(BF16)
        k = _dot(m, wk_ref[i])
        v = _dot(m, wv_ref[i])
        vb_ref[i] = v.astype(BF16)
        for b in range(k_ref.shape[1]):
            kb = k[b * N_MEM:(b + 1) * N_MEM]
            vb = v[b * N_MEM:(b + 1) * N_MEM]
            kt_ref[i, b] = kb.T.astype(BF16)
            for h in range(N_XHEADS):
                rows = pl.ds(h, N_MEM, stride=N_XHEADS)
                k_ref[i, b, rows, :] = _head(kb, h)
                v_ref[i, b, rows, :] = _head(vb, h)


def _mem_kv(mem, g, wk, wv, casts=()):
    rows = mem.shape[0]
    depth = g.shape[0]
    nb = 2
    tm = nb * N_MEM
    steps = rows // tm
    cast_in, cast_out, cast_shape = _cast_rider_specs(casts, steps)
    return pl.pallas_call(
        functools.partial(_mem_kv_kernel, n_casts=len(casts)),
        grid=(steps,),
        in_specs=[pl.BlockSpec((tm, D_MODEL), lambda r: (r, 0)),
                  _const_spec(g.shape), _const_spec(wk.shape), _const_spec(wv.shape)] + cast_in,
        out_specs=[pl.BlockSpec((depth, nb, N_MEM * N_XHEADS, XHEAD_DIM), lambda r: (0, r, 0, 0))] * 2
        + [pl.BlockSpec((depth, nb, XATT, N_MEM), lambda r: (0, r, 0, 0)),
           pl.BlockSpec((depth, tm, XATT), lambda r: (0, r, 0))] + cast_out,
        out_shape=[jax.ShapeDtypeStruct((depth, rows // N_MEM, N_MEM * N_XHEADS, XHEAD_DIM), F32)] * 2
        + [jax.ShapeDtypeStruct((depth, rows // N_MEM, XATT, N_MEM), BF16),
           jax.ShapeDtypeStruct((depth, rows, XATT), BF16)] + cast_shape,
        compiler_params=pltpu.CompilerParams(
            dimension_semantics=("arbitrary",), vmem_limit_bytes=VMEM_LIMIT),
        name="mem_kv",
    )(mem, g, wk, wv, *[w for w, _ in casts])


def _masked_spatial(ws_ref):
    row = lax.broadcasted_iota(jnp.int32, (CHUNK, CHUNK), 0)
    col = lax.broadcasted_iota(jnp.int32, (CHUNK, CHUNK), 1)
    return [jnp.where(row >= col, ws_ref[g], 0.0).astype(BF16) for g in range(A_GROUPS)]


def _gate_chunk(wm, vc, pair):
    base = pair * 2 * A_GDIM
    even = _dot(wm[2 * pair], vc[:, base:base + 2 * LANES])
    odd = _dot(wm[2 * pair + 1], vc[:, base + LANES:base + 3 * LANES])
    lane = lax.broadcasted_iota(jnp.int32, (CHUNK, LANES), 1)
    mid = jnp.where(lane < A_GDIM - LANES, even[:, LANES:], odd[:, :LANES])
    return jnp.concatenate([even[:, :LANES], mid, odd[:, LANES:]], axis=1)


def _s5_tile(u, hr, hi, a2r, a2i, b2, c2, k2, bu_ref, hs_ref, steps, nb):
    pairs = steps // 2
    u3 = u.reshape(pairs, 2 * nb, LANES)
    u2 = jnp.concatenate([u3[:, 0:nb, :].reshape(pairs * nb, LANES),
                          u3[:, nb:, :].reshape(pairs * nb, LANES)], axis=1).astype(BF16)
    bu_ref[...] = _dot(u2, b2)
    a2r = jnp.broadcast_to(a2r, (nb, S5_HALF))
    a2i = jnp.broadcast_to(a2i, (nb, S5_HALF))
    for k in range(pairs):
        rows = slice(k * nb, (k + 1) * nb)
        hs_ref[rows, 0:S5_HALF] = hr
        hs_ref[rows, S5_HALF:] = hi
        hr, hi = (a2r * hr - a2i * hi + bu_ref[rows, 0:S5_HALF],
                  a2r * hi + a2i * hr + bu_ref[rows, S5_HALF:])
    y2 = _dot(hs_ref[...].astype(BF16), c2) + _dot(u2, k2)
    return y2, hr, hi


def _layer_a_prompt_kernel(*refs, n_casts):
    (x_ref, kt_ref, v_ref, win_ref, lng_ref, lnb_ref, ws_ref, bias_ref,
     wout_ref, pre_ref, post_ref, sq_ref, sk_ref, sv_ref) = refs[:14]
    cast_in = refs[14:14 + n_casts]
    o_ref, satt_ref = refs[14 + n_casts:16 + n_casts]
    cast_out = refs[16 + n_casts:16 + 2 * n_casts]
    m_ref, g_ref = refs[16 + 2 * n_casts:]
    _sample_attn_kernel(sq_ref, sk_ref, sv_ref, satt_ref)
    for src, dst in zip(cast_in, cast_out):
        dst[...] = src[...].astype(BF16)
    wm = _masked_spatial(ws_ref)
    heads = range(N_XHEADS)
    n_streams = x_ref.shape[0] // A_STREAM_ROWS

    def project(s):
        x = x_ref[pl.ds(s * A_STREAM_ROWS, A_STREAM_ROWS), :]
        h = _rms(x, pre_ref[...]).astype(BF16)
        v = _layer_norm(_gelu(_dot(h, win_ref[:, BRANCH:2 * BRANCH])), lng_ref[...], lnb_ref[...])
        u = _gelu(_dot(h, win_ref[:, 0:BRANCH]))
        gate = _silu(_dot(h, win_ref[:, 2 * BRANCH + XATT:]))
        g_ref[pl.ds(s * A_STREAM_ROWS, A_STREAM_ROWS), 0:BRANCH] = u * gate[:, 0:BRANCH]
        g_ref[pl.ds(s * A_STREAM_ROWS, A_STREAM_ROWS), BRANCH:] = gate[:, BRANCH:]
        q = _dot(h, win_ref[:, 2 * BRANCH:2 * BRANCH + XATT])
        return x, v.astype(BF16), q

    def mix_and_emit(s, x, vb, q):
        m = m_ref.at[pl.ds(s * A_STREAM_ROWS, A_STREAM_ROWS)]
        g = g_ref.at[pl.ds(s * A_STREAM_ROWS, A_STREAM_ROWS)]
        for c in range(A_STREAM_ROWS // CHUNK):
            rs = slice(c * CHUNK, (c + 1) * CHUNK)
            vc = vb[rs]
            for pair in range(A_GROUPS // 2):
                cs = slice(pair * 2 * A_GDIM, (pair + 1) * 2 * A_GDIM)
                mixed = _gate_chunk(wm, vc, pair) + bias_ref[:, cs]
                m[rs, cs] = (g[rs, cs] * mixed).astype(BF16)
        kv = v_ref[...]
        att = _attend([_head(q, h) for h in heads],
                      [kt_ref[h * XHEAD_DIM:(h + 1) * XHEAD_DIM, :] for h in heads],
                      [_head(kv, h) for h in heads])
        for h in heads:
            cs = slice(BRANCH + h * XHEAD_DIM, BRANCH + (h + 1) * XHEAD_DIM)
            m[:, cs] = (att[h] * g[:, cs]).astype(BF16)
        out = _dot(m[...], wout_ref[...])
        o_ref[pl.ds(s * A_STREAM_ROWS, A_STREAM_ROWS), :] = x + _rms(out, post_ref[...])

    pending = project(0)
    for s in range(n_streams):
        following = project(s + 1) if s + 1 < n_streams else None
        mix_and_emit(s, *pending)
        pending = following


def _layer_a_prompt(x, b0, nb, kt, vb, win, lng, lnb, ws, bias, wout, pre, post, layer,
                    sq, cache_k, cache_v, sample_layer, casts=()):
    tm = 512
    seq = x.shape[1]
    nt = seq // tm
    steps = nb * nt
    bb = sq.shape[0] // steps
    cast_specs = [pl.BlockSpec((None, w.shape[1] // steps, w.shape[2]),
                               functools.partial(lambda b, t, l: (l, b * nt + t, 0), l=l))
                  for w, l in casts]
    cast_out_specs = [pl.BlockSpec((w.shape[1] // steps, w.shape[2]), lambda b, t: (b * nt + t, 0))
                      for w, _ in casts]
    xin = pl.BlockSpec((None, tm, D_MODEL), lambda b, t: (b + b0, t, 0))
    xout = pl.BlockSpec((None, tm, D_MODEL), lambda b, t: (b, t, 0))
    kts = pl.BlockSpec((None, None, XATT, N_MEM), lambda b, t: (layer, b + b0, 0, 0))
    vs = pl.BlockSpec((None, N_MEM, XATT), lambda b, t: (layer, b + b0, 0))
    sqs = pl.BlockSpec((bb,) + sq.shape[1:], lambda b, t: (b * nt + t, 0, 0))
    scs = pl.BlockSpec((None, bb) + cache_k.shape[2:], lambda b, t: (sample_layer, b * nt + t, 0, 0))
    return pl.pallas_call(
        functools.partial(_layer_a_prompt_kernel, n_casts=len(casts)),
        grid=(nb, nt),
        in_specs=[xin, kts, vs, _const_spec(win.shape), _const_spec(lng.shape), _const_spec(lnb.shape),
                  _const_spec(ws.shape), _const_spec(bias.shape), _const_spec(wout.shape),
                  _layer_spec(pre.shape, layer), _layer_spec(post.shape, layer), sqs, scs, scs] + cast_specs,
        out_specs=[xout, sqs] + cast_out_specs,
        out_shape=[jax.ShapeDtypeStruct((nb, seq, D_MODEL), F32), jax.ShapeDtypeStruct(sq.shape, F32)]
        + [jax.ShapeDtypeStruct(w.shape[1:], BF16) for w, _ in casts],
        scratch_shapes=[pltpu.VMEM((tm, MIX_WIDTH), BF16), pltpu.VMEM((tm, MIX_WIDTH), F32)],
        compiler_params=pltpu.CompilerParams(
            dimension_semantics=("arbitrary", "arbitrary"), vmem_limit_bytes=VMEM_LIMIT),
        name="layer_a_prompt",
    )(x, kt, vb, win, lng, lnb, ws, bias, wout, pre, post, sq, cache_k, cache_v, *[w for w, _ in casts])


def _layer_b_prompt_kernel(xa_ref, xb_ref, kt_ref, v_ref, win_ref, a2r_ref, a2i_ref, b2_ref, c2_ref, k2_ref,
                           d_ref, wglu_ref, bglu_ref, wout_ref, pre_ref, post_ref,
                           o_ref, hre_ref, him_ref,
                           ut_ref, bu_ref, hs_ref, yt_ref, y_ref, m_ref):
    nb, tc, _ = o_ref.shape
    tm = nb * tc
    half = tc // 2

    @pl.when(pl.program_id(0) == 0)
    def _():
        hre_ref[...] = jnp.zeros_like(hre_ref)
        him_ref[...] = jnp.zeros_like(him_ref)

    x = jnp.concatenate([xa_ref[...], xb_ref[...]], axis=0).reshape(tm, D_MODEL)
    h = _rms(x, pre_ref[...]).astype(BF16)
    u = _dot(h, win_ref[:, 0:BRANCH])
    for j in range(S5_TILES):
        for b in range(nb):
            ut_ref[j, pl.ds(b, tc, stride=nb), :] = u[b * tc:(b + 1) * tc, j * LANES:(j + 1) * LANES]
    for j in range(S5_TILES):
        ss = slice(j * S5_HALF, (j + 1) * S5_HALF)
        par = j % 2
        y2, hr, hi = _s5_tile(ut_ref[j], hre_ref[:, ss], him_ref[:, ss], a2r_ref[j], a2i_ref[j],
                              b2_ref[j], c2_ref[j], k2_ref[j], bu_ref.at[par], hs_ref.at[par], tc, nb)
        hre_ref[:, ss] = hr
        him_ref[:, ss] = hi
        for tp in range(2):
            yt_ref[par, tp] = y2[:, tp * LANES:(tp + 1) * LANES]
            for b in range(nb):
                y_ref[j, pl.ds(b * tc + tp, half, stride=2), :] = yt_ref[par, tp, pl.ds(b, half, stride=nb), :]
    y = _gelu(jnp.concatenate([y_ref[j] for j in range(S5_TILES)], axis=1) + d_ref[...] * u)
    branch = y * _sigmoid(_dot(y.astype(BF16), wglu_ref[...]) + bglu_ref[...])
    gate = _silu(_dot(h, win_ref[:, BRANCH + XATT:]))
    m_ref[:, 0:BRANCH] = (branch * gate[:, 0:BRANCH]).astype(BF16)
    q = _dot(h, win_ref[:, BRANCH:BRANCH + XATT])
    pieces = [(b, h) for b in range(nb) for h in range(N_XHEADS)]
    att = _attend([_head(q[b * tc:(b + 1) * tc], h) for b, h in pieces],
                  [kt_ref[b, h * XHEAD_DIM:(h + 1) * XHEAD_DIM, :] for b, h in pieces],
                  [_head(v_ref[b * N_MEM:(b + 1) * N_MEM, :], h) for b, h in pieces])
    for (b, h), a in zip(pieces, att):
        rs = slice(b * tc, (b + 1) * tc)
        cs = slice(BRANCH + h * XHEAD_DIM, BRANCH + (h + 1) * XHEAD_DIM)
        m_ref[rs, cs] = (a * gate[rs, cs]).astype(BF16)
    out = _dot(m_ref[...], wout_ref[...])
    o_ref[...] = (x + _rms(out, post_ref[...])).reshape(nb, tc, D_MODEL)


def _layer_b_prompt(xa, xb, kb, vb, win, s5, dvec, wglu, bglu, wout, pre, post, layer):
    tc = 64
    nb, seq = xa.shape[0] + xb.shape[0], xa.shape[1]
    tm = nb * tc
    xin = pl.BlockSpec((xa.shape[0], tc, D_MODEL), lambda t: (0, t, 0))
    xs = pl.BlockSpec((nb, tc, D_MODEL), lambda t: (0, t, 0))
    hs = pl.BlockSpec((nb, SSM_GROUPS * SSM_STATE), lambda t: (0, 0))
    ins = [kb, vb, win, *s5, dvec, wglu, bglu, wout, pre, post]
    stacked = [True, True] + [False] * (len(s5) + 5) + [True, True]
    return pl.pallas_call(
        _layer_b_prompt_kernel,
        grid=(seq // tc,),
        in_specs=[xin, xin] + [_layer_spec(a.shape, layer) if st else _const_spec(a.shape)
                               for a, st in zip(ins, stacked)],
        out_specs=[xs, hs, hs],
        out_shape=[jax.ShapeDtypeStruct((nb, seq, D_MODEL), F32),
                   jax.ShapeDtypeStruct((nb, SSM_GROUPS * SSM_STATE), F32),
                   jax.ShapeDtypeStruct((nb, SSM_GROUPS * SSM_STATE), F32)],
        scratch_shapes=[pltpu.VMEM((S5_TILES, tm, LANES), F32),
                        pltpu.VMEM((2, tm // 2, 2 * S5_HALF), F32),
                        pltpu.VMEM((2, tm // 2, 2 * S5_HALF), F32),
                        pltpu.VMEM((2, 2, tm // 2, LANES), F32),
                        pltpu.VMEM((S5_TILES, tm, LANES), F32),
                        pltpu.VMEM((tm, MIX_WIDTH), BF16)],
        compiler_params=pltpu.CompilerParams(
            dimension_semantics=("arbitrary",), vmem_limit_bytes=VMEM_LIMIT),
        name="layer_b_prompt",
    )(xa, xb, *ins)


Q_ROWS = N_XHEADS * DEC_SEQ


def _store_q_by_sequence(q_ref, q):
    for h in range(N_XHEADS):
        for t in range(DEC_SEQ):
            q_ref[pl.ds(h * DEC_SEQ + t, DEC_BATCH, stride=Q_ROWS), :] = _head(q[t * DEC_BATCH:(t + 1) * DEC_BATCH], h)


def _sample_pre_a_kernel(x_ref, win_ref, lng_ref, lnb_ref, wrow_ref, brow_ref, pre_ref,
                         v_ref, mbr_ref, q_ref, gatt_ref):
    nb = DEC_BATCH
    x = jnp.concatenate([x_ref[:, t, :] for t in range(DEC_SEQ)], axis=0)
    h = _rms(x, pre_ref[...]).astype(BF16)
    gate = _silu(_dot(h, win_ref[:, 2 * BRANCH + XATT:]))
    u = _gelu(_dot(h, win_ref[:, 0:BRANCH]))
    v = _layer_norm(_gelu(_dot(h, win_ref[:, BRANCH:2 * BRANCH])), lng_ref[...], lnb_ref[...])
    for t in range(DEC_SEQ):
        v_ref[:, t, :] = v[t * nb:(t + 1) * nb]
    for t in range(DEC_SEQ):
        rs = slice(t * nb, (t + 1) * nb)
        mixed = brow_ref[t:t + 1, :]
        for s in range(t + 1):
            mixed = mixed + wrow_ref[t * DEC_SEQ + s:t * DEC_SEQ + s + 1, :] * v[s * nb:(s + 1) * nb]
        mbr_ref[rs, :] = u[rs] * mixed * gate[rs, 0:BRANCH]
    _store_q_by_sequence(q_ref, _dot(h, win_ref[:, 2 * BRANCH:2 * BRANCH + XATT]))
    gatt_ref[...] = gate[:, BRANCH:]


def _sample_pre_a(x, win, lng, lnb, wrow, brow, pre, layer):
    rows = DEC_SEQ * DEC_BATCH
    ins = [x, win, lng, lnb, wrow, brow, pre]
    shapes = [(DEC_BATCH, DEC_SEQ, BRANCH), (rows, BRANCH), (DEC_BATCH * Q_ROWS, XHEAD_DIM), (rows, XATT)]
    return pl.pallas_call(
        _sample_pre_a_kernel,
        grid=(1,),
        in_specs=[_const_spec(a.shape) for a in ins[:-1]] + [_layer_spec(pre.shape, layer)],
        out_specs=[pl.BlockSpec(s, functools.partial(lambda i, n: (0,) * n, n=len(s))) for s in shapes],
        out_shape=[jax.ShapeDtypeStruct(s, F32) for s in shapes],
        compiler_params=pltpu.CompilerParams(
            dimension_semantics=("arbitrary",), vmem_limit_bytes=VMEM_LIMIT),
        name="sample_pre_a",
    )(*ins)


def _sample_pre_b_kernel(x_ref, win_ref, sre_ref, sim_ref, a2r_ref, a2i_ref, b2_ref, c2_ref, k2_ref, d_ref,
                         wglu_ref, bglu_ref, pre_ref,
                         mbr_ref, q_ref, gatt_ref, hre_ref, him_ref,
                         bu_ref, hs_ref, y_ref):
    nb = DEC_BATCH
    h = _rms(x_ref[...], pre_ref[...]).astype(BF16)
    u = _dot(h, win_ref[:, 0:BRANCH])
    for j in range(S5_TILES):
        ss = slice(j * S5_HALF, (j + 1) * S5_HALF)
        ls = slice(j * LANES, (j + 1) * LANES)
        y2, hr, hi = _s5_tile(u[:, ls], sre_ref[:, ss], sim_ref[:, ss], a2r_ref[j], a2i_ref[j],
                              b2_ref[j], c2_ref[j], k2_ref[j], bu_ref, hs_ref, DEC_SEQ, nb)
        hre_ref[:, ss] = hr
        him_ref[:, ss] = hi
        for pair in range(DEC_SEQ // 2):
            for tp in range(2):
                t = 2 * pair + tp
                y_ref[t * nb:(t + 1) * nb, ls] = y2[pair * nb:(pair + 1) * nb, tp * LANES:(tp + 1) * LANES]
    y = _gelu(y_ref[...] + d_ref[...] * u)
    branch = y * _sigmoid(_dot(y.astype(BF16), wglu_ref[...]) + bglu_ref[...])
    gate = _silu(_dot(h, win_ref[:, BRANCH + XATT:]))
    mbr_ref[...] = branch * gate[:, 0:BRANCH]
    _store_q_by_sequence(q_ref, _dot(h, win_ref[:, BRANCH:BRANCH + XATT]))
    gatt_ref[...] = gate[:, BRANCH:]


def _sample_pre_b(x, win, sre, sim, s5, dvec, wglu, bglu, pre, layer):
    rows = x.shape[0]
    ins = [x, win, sre, sim, *s5, dvec, wglu, bglu]
    shapes = [(rows, BRANCH), (DEC_BATCH * Q_ROWS, XHEAD_DIM), (rows, XATT), sre.shape, sim.shape]
    return pl.pallas_call(
        _sample_pre_b_kernel,
        grid=(1,),
        in_specs=[_const_spec(a.shape) for a in ins] + [_layer_spec(pre.shape, layer)],
        out_specs=[pl.BlockSpec(s, lambda i: (0, 0)) for s in shapes],
        out_shape=[jax.ShapeDtypeStruct(s, F32) for s in shapes],
        scratch_shapes=[pltpu.VMEM((rows // 2, 2 * S5_HALF), F32),
                        pltpu.VMEM((rows // 2, 2 * S5_HALF), F32),
                        pltpu.VMEM((rows, BRANCH), F32)],
        compiler_params=pltpu.CompilerParams(
            dimension_semantics=("arbitrary",), vmem_limit_bytes=VMEM_LIMIT),
        name="sample_pre_b",
    )(*ins, pre)


def _sample_attn_kernel(q_ref, k_ref, v_ref, o_ref):
    bb, rows, _ = q_ref.shape
    cols = k_ref.shape[1]
    row_h = (lax.broadcasted_iota(jnp.int32, (bb * rows, cols), 0) // DEC_SEQ) % N_XHEADS
    col_h = lax.broadcasted_iota(jnp.int32, (bb * rows, cols), 1) % N_XHEADS
    s = jnp.concatenate([_dot_nt(q_ref[j].astype(BF16), k_ref[j].astype(BF16)) for j in range(bb)], axis=0)
    s = jnp.where(row_h == col_h, s * (XHEAD_DIM ** -0.5), -1e30)
    p = jnp.exp(s - jnp.max(s, axis=-1, keepdims=True))
    inv = 1.0 / jnp.sum(p, axis=-1, keepdims=True)
    pb = p.astype(BF16)
    for j in range(bb):
        rs = slice(j * rows, (j + 1) * rows)
        o_ref[j] = _dot(pb[rs], v_ref[j].astype(BF16)) * inv[rs]


def _sample_post_kernel(x_ref, mbr_ref, att_ref, gatt_ref, wout_ref, post_ref, o_ref, m_ref):
    nb = DEC_BATCH
    for h in range(N_XHEADS):
        cs = slice(h * XHEAD_DIM, (h + 1) * XHEAD_DIM)
        for t in range(DEC_SEQ):
            rs = slice(t * nb, (t + 1) * nb)
            att = att_ref[pl.ds(h * DEC_SEQ + t, nb, stride=Q_ROWS), :]
            m_ref[rs, cs] = (att * gatt_ref[rs, cs]).astype(BF16)
    out = (_dot(mbr_ref[...].astype(BF16), wout_ref[0:BRANCH, :])
           + _dot(m_ref[...], wout_ref[BRANCH:, :]))
    y = _rms(out, post_ref[...])
    for t in range(DEC_SEQ):
        rs = slice(t * nb, (t + 1) * nb)
        x = x_ref[:, t, :] if len(x_ref.shape) == 3 else x_ref[rs, :]
        if len(o_ref.shape) == 3:
            o_ref[:, t, :] = x + y[rs]
        else:
            o_ref[rs, :] = x + y[rs]


def _sample_post(x, mbr, att, gatt, wout, post, layer, out_shape):
    ins = [x, mbr, att, gatt, wout, post]
    n_out = len(out_shape)
    return pl.pallas_call(
        _sample_post_kernel,
        grid=(1,),
        in_specs=[_const_spec(a.shape) for a in ins[:5]] + [_layer_spec(post.shape, layer)],
        out_specs=pl.BlockSpec(out_shape, lambda i: (0,) * n_out),
        out_shape=jax.ShapeDtypeStruct(out_shape, F32),
        scratch_shapes=[pltpu.VMEM((mbr.shape[0], XATT), BF16)],
        compiler_params=pltpu.CompilerParams(
            dimension_semantics=("arbitrary",), vmem_limit_bytes=VMEM_LIMIT),
        name="sample_post",
    )(*ins)


def kernel(x_prompt, x_sample, cache_mem_k, cache_mem_v, state_ssm_re, state_ssm_im, mem_prompt,
           w_in_a, ln_v_g, ln_v_b, w_spatial, b_spatial,
           w_in_b, ssm_lambda_re, ssm_lambda_im, ssm_log_dt, ssm_b_re, ssm_b_im, ssm_c_re, ssm_c_im,
           ssm_d, w_glu, b_glu,
           mem_norm_g, w_mem_k, w_mem_v, w_out, pre_norm_g, post_norm_g):
    depth = w_out.shape[0]
    wk = w_mem_k.astype(BF16)
    wv = w_mem_v.astype(BF16)
    pre = pre_norm_g.reshape(depth, 1, D_MODEL)
    post = post_norm_g.reshape(depth, 1, D_MODEL)
    lng = ln_v_g[0].reshape(1, BRANCH)
    lnb = ln_v_b[0].reshape(1, BRANCH)
    bglu = b_glu[0].reshape(1, BRANCH)
    dvec = ssm_d[0].reshape(1, BRANCH)
    bias = jnp.repeat(b_spatial[0].T, A_GDIM, axis=1)
    ws4 = w_spatial[0][:, :DEC_SEQ, :DEC_SEQ]
    wrow = jnp.repeat(ws4.transpose(1, 2, 0).reshape(DEC_SEQ * DEC_SEQ, A_GROUPS), A_GDIM, axis=1)
    brow = bias[:DEC_SEQ]

    *s5, win_a = _s5_prep(ssm_lambda_re[0], ssm_lambda_im[0], ssm_log_dt[0],
                          ssm_b_re[0], ssm_b_im[0], ssm_c_re[0], ssm_c_im[0], casts=[(w_in_a, 0)])

    mem = mem_prompt.reshape(BATCH * N_MEM, D_MODEL)
    mk, mv, mkt, mvb, wglu, wout0 = _mem_kv(mem, mem_norm_g.reshape(depth, 1, D_MODEL), wk, wv,
                                            casts=[(w_glu, 0), (w_out, 0)])

    ck = cache_mem_k.reshape(depth, DEC_BATCH, N_MEM * N_XHEADS, XHEAD_DIM)
    cv = cache_mem_v.reshape(depth, DEC_BATCH, N_MEM * N_XHEADS, XHEAD_DIM)
    q_shape = (DEC_BATCH, Q_ROWS, XHEAD_DIM)
    half = BATCH // 2
    layer_a = functools.partial(_layer_a_prompt, x_prompt, kt=mkt, vb=mvb, win=win_a, lng=lng, lnb=lnb,
                                ws=w_spatial[0], bias=bias, wout=wout0, pre=pre, post=post, layer=0,
                                cache_k=ck, cache_v=cv)
    rows_tb = (DEC_SEQ * DEC_BATCH, D_MODEL)
    v_rows, mbr, q, gatt = _sample_pre_a(x_sample, win_a, lng, lnb, wrow, brow, pre, 0)
    yp_a, att, win_b = layer_a(b0=0, nb=half, sq=q.reshape(q_shape), sample_layer=0, casts=[(w_in_b, 0)])
    xs = _sample_post(x_sample, mbr, att.reshape(q.shape), gatt, wout0, post, 0, rows_tb)
    sre = state_ssm_re[0].reshape(DEC_BATCH, SSM_GROUPS * SSM_STATE)
    sim = state_ssm_im[0].reshape(DEC_BATCH, SSM_GROUPS * SSM_STATE)
    mbr, q, gatt, hs_re, hs_im = _sample_pre_b(xs, win_b, sre, sim, s5, dvec, wglu, bglu, pre, 1)
    yp_b, att, wout1 = layer_a(b0=half, nb=half, sq=q.reshape(q_shape), sample_layer=1, casts=[(w_out, 1)])
    ys = _sample_post(xs, mbr, att.reshape(q.shape), gatt, wout1, post, 1, x_sample.shape)
    yp, hp_re, hp_im = _layer_b_prompt(yp_a, yp_b, mkt, mvb, win_b, s5, dvec, wglu, bglu, wout1, pre, post, 1)

    kv_shape = (depth, BATCH, N_MEM, N_XHEADS, XHEAD_DIM)
    st_p = (1, BATCH, SSM_GROUPS, SSM_STATE)
    st_s = (1, DEC_BATCH, SSM_GROUPS, SSM_STATE)
    return (yp, ys, mk.reshape(kv_shape), mv.reshape(kv_shape),
            hp_re.reshape(st_p), hp_im.reshape(st_p), hs_re.reshape(st_s), hs_im.reshape(st_s),
            v_rows[None])
```

```python
import functools
import math

import jax
import jax.numpy as jnp
from jax import lax
from jax.experimental import pallas as pl
from jax.experimental.pallas import tpu as pltpu

D_MODEL = 1024
BATCH = 8
SEQ = 2048
DEC_BATCH = 128
DEC_SEQ = 4
BRANCH = 1536
N_XHEADS = 4
XHEAD_DIM = 128
XATT = N_XHEADS * XHEAD_DIM
MIX_WIDTH = BRANCH + XATT
N_MEM = 256
CHUNK = 128
A_GROUPS = 8
A_GDIM = BRANCH // A_GROUPS
SSM_GCH = 16
SSM_GROUPS = 96
SSM_STATE = 64
EPS = 1e-6

LANES = 128
S5_TILE_GROUPS = LANES // SSM_GCH
S5_TILES = SSM_GROUPS // S5_TILE_GROUPS
S5_HALF = S5_TILE_GROUPS * SSM_STATE
VMEM_LIMIT = 60 * 1024 * 1024
A_STREAM_ROWS = 256

F32 = jnp.float32
BF16 = jnp.bfloat16


def _dot(a, b):
    return jnp.dot(a, b, preferred_element_type=F32)


def _rms(x, g):
    return x * lax.rsqrt(jnp.mean(x * x, axis=-1, keepdims=True) + EPS) * g


def _layer_norm(x, g, b):
    mu = jnp.mean(x, axis=-1, keepdims=True)
    xc = x - mu
    var = jnp.mean(xc * xc, axis=-1, keepdims=True)
    return xc * lax.rsqrt(var + EPS) * g + b


def _gelu(x):
    c = math.sqrt(2.0 / math.pi)
    hx = 0.5 * x
    return hx + hx * jnp.tanh(x * (c + (c * 0.044715) * (x * x)))


def _sigmoid(x):
    return 0.5 + 0.5 * jnp.tanh(0.5 * x)


def _silu(x):
    hx = 0.5 * x
    return hx + hx * jnp.tanh(hx)


def _dot_nt(a, b):
    return lax.dot_general(a, b, (((1,), (1,)), ((), ())), preferred_element_type=F32)


def _attend(qs, kts, vs):
    r = qs[0].shape[0]
    s = jnp.concatenate([_dot(q.astype(BF16), kt) for q, kt in zip(qs, kts)], axis=0)
    s = s * (XHEAD_DIM ** -0.5)
    p = jnp.exp(s - jnp.max(s, axis=-1, keepdims=True))
    inv = 1.0 / jnp.sum(p, axis=-1, keepdims=True)
    pb = p.astype(BF16)
    return [_dot(pb[i * r:(i + 1) * r], v) * inv[i * r:(i + 1) * r] for i, v in enumerate(vs)]


def _head(a, h):
    return a[:, h * XHEAD_DIM:(h + 1) * XHEAD_DIM]


def _const_spec(shape):
    nd = len(shape)
    return pl.BlockSpec(shape, lambda *_: (0,) * nd, pipeline_mode=pl.Buffered(1))


def _layer_spec(shape, layer):
    nd = len(shape)
    return pl.BlockSpec((None,) + tuple(shape[1:]), lambda *_: (layer,) + (0,) * (nd - 1),
                        pipeline_mode=pl.Buffered(1))


def _lam_bar(lr, li, ldt):
    dt = jnp.exp(ldt)
    e = jnp.exp(lr * dt)
    return e * jnp.cos(li * dt), e * jnp.sin(li * dt)


def _cast_rider_specs(casts, steps):
    in_specs = [pl.BlockSpec((None, w.shape[1] // steps, w.shape[2]),
                             functools.partial(lambda i, l: (l, i, 0), l=l)) for w, l in casts]
    out_specs = [pl.BlockSpec((w.shape[1] // steps, w.shape[2]), lambda i: (i, 0)) for w, _ in casts]
    out_shape = [jax.ShapeDtypeStruct(w.shape[1:], BF16) for w, _ in casts]
    return in_specs, out_specs, out_shape


def _s5_prep_kernel(*refs, n_casts):
    n_in, n_out = 7, 5
    ins, cast_in = refs[:n_in], refs[n_in:n_in + n_casts]
    outs, cast_out = refs[n_in + n_casts:n_in + n_casts + n_out], refs[n_in + n_casts + n_out:]
    for src, dst in zip(cast_in, cast_out):
        dst[...] = src[...].astype(BF16)
    for j in range(ins[0].shape[0]):
        _s5_prep_tile(*[r.at[pl.ds(j, 1)] for r in (*ins, *outs)])


def _s5_prep_tile(lr_ref, li_ref, ldt_ref, br_ref, bi_ref, cr_ref, ci_ref,
                  a2r_ref, a2i_ref, b2_ref, c2_ref, k2_ref):
    lr = lr_ref[0]
    li = li_ref[0]
    ar, ai = _lam_bar(lr, li, ldt_ref[0])
    a2r = ar * ar - ai * ai
    a2i = 2.0 * ar * ai
    a2r_ref[0] = a2r
    a2i_ref[0] = a2i
    nr = ar - 1.0
    den = lr * lr + li * li
    kr = (nr * lr + ai * li) / den
    ki = (ai * lr - nr * li) / den
    shape = (LANES, S5_HALF)
    own = (lax.broadcasted_iota(jnp.int32, shape, 0) // SSM_GCH
           == lax.broadcasted_iota(jnp.int32, shape, 1) // SSM_STATE)

    def own_blocks(ref):
        return jnp.where(own, jnp.concatenate([ref[0]] * S5_TILE_GROUPS, axis=0), 0.0)

    br, bi, cr, ci = own_blocks(br_ref), own_blocks(bi_ref), own_blocks(cr_ref), own_blocks(ci_ref)
    bbr = kr * br - ki * bi
    bbi = kr * bi + ki * br
    b_tok1 = jnp.concatenate([bbr, bbi], axis=1).astype(BF16)
    b_tok0 = jnp.concatenate([ar * bbr - ai * bbi, ar * bbi + ai * bbr], axis=1).astype(BF16)
    b2_ref[0, 0:LANES, :] = b_tok0
    b2_ref[0, LANES:, :] = b_tok1
    c_tok0 = jnp.concatenate([cr * ar - ci * ai, -(cr * ai + ci * ar)], axis=1)
    c_tok1 = jnp.concatenate([cr * a2r - ci * a2i, -(cr * a2i + ci * a2r)], axis=1)
    c2_ref[0, :, 0:LANES] = c_tok0.T.astype(BF16)
    c2_ref[0, :, LANES:] = c_tok1.T.astype(BF16)
    c_plain = jnp.concatenate([cr, -ci], axis=1).astype(BF16)
    k0 = _dot_nt(b_tok1, c_plain).astype(BF16)
    k1 = _dot_nt(b_tok0, c_plain).astype(BF16)
    k2_ref[0, 0:LANES, 0:LANES] = k0
    k2_ref[0, 0:LANES, LANES:] = k1
    k2_ref[0, LANES:, 0:LANES] = jnp.zeros_like(k0)
    k2_ref[0, LANES:, LANES:] = k0


def _s5_prep(lam_re, lam_im, log_dt, b_re, b_im, c_re, c_im, casts=()):
    nt, tg = S5_TILES, S5_TILE_GROUPS
    lr = lam_re.reshape(nt, 1, S5_HALF)
    li = lam_im.reshape(nt, 1, S5_HALF)
    ldt = jnp.broadcast_to(log_dt[:, None], (SSM_GROUPS, SSM_STATE)).reshape(nt, 1, S5_HALF)

    def b_cp(b):
        return b.reshape(nt, tg, SSM_STATE, SSM_GCH).transpose(0, 3, 1, 2).reshape(nt, SSM_GCH, S5_HALF)

    def c_cp(c):
        return c.reshape(nt, tg, SSM_GCH, SSM_STATE).transpose(0, 2, 1, 3).reshape(nt, SSM_GCH, S5_HALF)

    per_step = 3

    def spec(*shape):
        return pl.BlockSpec((per_step,) + shape, lambda j: (j, 0, 0))

    vec, chan = spec(1, S5_HALF), spec(SSM_GCH, S5_HALF)
    out_shapes = [(1, S5_HALF), (1, S5_HALF), (2 * LANES, 2 * S5_HALF), (2 * S5_HALF, 2 * LANES),
                  (2 * LANES, 2 * LANES)]
    out_dtypes = [F32, F32, BF16, BF16, BF16]
    steps = nt // per_step
    cast_in, cast_out, cast_shape = _cast_rider_specs(casts, steps)
    return pl.pallas_call(
        functools.partial(_s5_prep_kernel, n_casts=len(casts)),
        grid=(steps,),
        in_specs=[vec, vec, vec, chan, chan, chan, chan] + cast_in,
        out_specs=[spec(*s) for s in out_shapes] + cast_out,
        out_shape=[jax.ShapeDtypeStruct((nt,) + s, d) for s, d in zip(out_shapes, out_dtypes)] + cast_shape,
        compiler_params=pltpu.CompilerParams(
            dimension_semantics=("arbitrary",), vmem_limit_bytes=VMEM_LIMIT),
        name="s5_prep",
    )(lr, li, ldt, b_cp(b_re), b_cp(b_im), c_cp(c_re), c_cp(c_im), *[w for w, _ in casts])


def _mem_kv_kernel(mem_ref, g_ref, wk_ref, wv_ref, *rest, n_casts):
    cast_in, (k_ref, v_ref, kt_ref, vb_ref), cast_out = rest[:n_casts], rest[n_casts:n_casts + 4], rest[n_casts + 4:]
    for src, dst in zip(cast_in, cast_out):
        dst[...] = src[...].astype(BF16)
    mem = mem_ref[...]
    scaled = mem * lax.rsqrt(jnp.mean(mem * mem, axis=-1, keepdims=True) + EPS)
    for i in range(g_ref.shape[0]):
        m = (scaled * g_ref[i]).astype(BF16)
        k = _dot(m, wk_ref[i])
        v = _dot(m, wv_ref[i])
        vb_ref[i] = v.astype(BF16)
        for b in range(k_ref.shape[1]):
            kb = k[b * N_MEM:(b + 1) * N_MEM]
            vb = v[b * N_MEM:(b + 1) * N_MEM]
            kt_ref[i, b] = kb.T.astype(BF16)
            for h in range(N_XHEADS):
                rows = pl.ds(h, N_MEM, stride=N_XHEADS)
                k_ref[i, b, rows, :] = _head(kb, h)
                v_ref[i, b, rows, :] = _head(vb, h)


def _mem_kv(mem, g, wk, wv, casts=()):
    rows = mem.shape[0]
    depth = g.shape[0]
    nb = 1
    tm = nb * N_MEM
    steps = rows // tm
    cast_in, cast_out, cast_shape = _cast_rider_specs(casts, steps)
    return pl.pallas_call(
        functools.partial(_mem_kv_kernel, n_casts=len(casts)),
        grid=(steps,),
        in_specs=[pl.BlockSpec((tm, D_MODEL), lambda r: (r, 0)),
                  _const_spec(g.shape), _const_spec(wk.shape), _const_spec(wv.shape)] + cast_in,
        out_specs=[pl.BlockSpec((depth, nb, N_MEM * N_XHEADS, XHEAD_DIM), lambda r: (0, r, 0, 0))] * 2
        + [pl.BlockSpec((depth, nb, XATT, N_MEM), lambda r: (0, r, 0, 0)),
           pl.BlockSpec((depth, tm, XATT), lambda r: (0, r, 0))] + cast_out,
        out_shape=[jax.ShapeDtypeStruct((depth, rows // N_MEM, N_MEM * N_XHEADS, XHEAD_DIM), F32)] * 2
        + [jax.ShapeDtypeStruct((depth, rows // N_MEM, XATT, N_MEM), BF16),
           jax.ShapeDtypeStruct((depth, rows, XATT), BF16)] + cast_shape,
        compiler_params=pltpu.CompilerParams(
            dimension_semantics=("arbitrary",), vmem_limit_bytes=VMEM_LIMIT),
        name="mem_kv",
    )(mem, g, wk, wv, *[w for w, _ in casts])


def _masked_spatial(ws_ref):
    row = lax.broadcasted_iota(jnp.int32, (CHUNK, CHUNK), 0)
    col = lax.broadcasted_iota(jnp.int32, (CHUNK, CHUNK), 1)
    return [jnp.where(row >= col, ws_ref[g], 0.0).astype(BF16) for g in range(A_GROUPS)]


def _gate_chunk(wm, vc, pair):
    base = pair * 2 * A_GDIM
    even = _dot(wm[2 * pair], vc[:, base:base + 2 * LANES])
    odd = _dot(wm[2 * pair + 1], vc[:, base + LANES:base + 3 * LANES])
    lane = lax.broadcasted_iota(jnp.int32, (CHUNK, LANES), 1)
    mid = jnp.where(lane < A_GDIM - LANES, even[:, LANES:], odd[:, :LANES])
    return jnp.concatenate([even[:, :LANES], mid, odd[:, LANES:]], axis=1)


def _s5_tile(u, hr, hi, a2r, a2i, b2, c2, k2, bu_ref, hs_ref, steps, nb):
    pairs = steps // 2
    u3 = u.reshape(pairs, 2 * nb, LANES)
    u2 = jnp.concatenate([u3[:, 0:nb, :].reshape(pairs * nb, LANES),
                          u3[:, nb:, :].reshape(pairs * nb, LANES)], axis=1).astype(BF16)
    bu_ref[...] = _dot(u2, b2)
    a2r = jnp.broadcast_to(a2r, (nb, S5_HALF))
    a2i = jnp.broadcast_to(a2i, (nb, S5_HALF))
    for k in range(pairs):
        rows = slice(k * nb, (k + 1) * nb)
        hs_ref[rows, 0:S5_HALF] = hr
        hs_ref[rows, S5_HALF:] = hi
        hr, hi = (a2r * hr - a2i * hi + bu_ref[rows, 0:S5_HALF],
                  a2r * hi + a2i * hr + bu_ref[rows, S5_HALF:])
    y2 = _dot(hs_ref[...].astype(BF16), c2) + _dot(u2, k2)
    return y2, hr, hi


def _layer_a_prompt_kernel(*refs, n_casts):
    (x_ref, kt_ref, v_ref, win_ref, lng_ref, lnb_ref, ws_ref, bias_ref,
     wout_ref, pre_ref, post_ref, sq_ref, sk_ref, sv_ref) = refs[:14]
    cast_in = refs[14:14 + n_casts]
    o_ref, satt_ref = refs[14 + n_casts:16 + n_casts]
    cast_out = refs[16 + n_casts:16 + 2 * n_casts]
    m_ref, g_ref = refs[16 + 2 * n_casts:]
    _sample_attn_kernel(sq_ref, sk_ref, sv_ref, satt_ref)
    for src, dst in zip(cast_in, cast_out):
        dst[...] = src[...].astype(BF16)
    wm = _masked_spatial(ws_ref)
    heads = range(N_XHEADS)
    n_streams = x_ref.shape[0] // A_STREAM_ROWS

    def project(s):
        x = x_ref[pl.ds(s * A_STREAM_ROWS, A_STREAM_ROWS), :]
        h = _rms(x, pre_ref[...]).astype(BF16)
        v = _layer_norm(_gelu(_dot(h, win_ref[:, BRANCH:2 * BRANCH])), lng_ref[...], lnb_ref[...])
        u = _gelu(_dot(h, win_ref[:, 0:BRANCH]))
        gate = _silu(_dot(h, win_ref[:, 2 * BRANCH + XATT:]))
        g_ref[pl.ds(s * A_STREAM_ROWS, A_STREAM_ROWS), 0:BRANCH] = u * gate[:, 0:BRANCH]
        g_ref[pl.ds(s * A_STREAM_ROWS, A_STREAM_ROWS), BRANCH:] = gate[:, BRANCH:]
        q = _dot(h, win_ref[:, 2 * BRANCH:2 * BRANCH + XATT])
        return x, v.astype(BF16), q

    def mix_and_emit(s, x, vb, q):
        m = m_ref.at[pl.ds(s * A_STREAM_ROWS, A_STREAM_ROWS)]
        g = g_ref.at[pl.ds(s * A_STREAM_ROWS, A_STREAM_ROWS)]
        for c in range(A_STREAM_ROWS // CHUNK):
            rs = slice(c * CHUNK, (c + 1) * CHUNK)
            vc = vb[rs]
            for pair in range(A_GROUPS // 2):
                cs = slice(pair * 2 * A_GDIM, (pair + 1) * 2 * A_GDIM)
                mixed = _gate_chunk(wm, vc, pair) + bias_ref[:, cs]
                m[rs, cs] = (g[rs, cs] * mixed).astype(BF16)
        kv = v_ref[...]
        att = _attend([_head(q, h) for h in heads],
                      [kt_ref[h * XHEAD_DIM:(h + 1) * XHEAD_DIM, :] for h in heads],
                      [_head(kv, h) for h in heads])
        for h in heads:
            cs = slice(BRANCH + h * XHEAD_DIM, BRANCH + (h + 1) * XHEAD_DIM)
            m[:, cs] = (att[h] * g[:, cs]).astype(BF16)
        out = _dot(m[...], wout_ref[...])
        o_ref[pl.ds(s * A_STREAM_ROWS, A_STREAM_ROWS), :] = x + _rms(out, post_ref[...])

    pending = project(0)
    for s in range(n_streams):
        following = project(s + 1) if s + 1 < n_streams else None
        mix_and_emit(s, *pending)
        pending = following


def _layer_a_prompt(x, b0, nb, kt, vb, win, lng, lnb, ws, bias, wout, pre, post, layer,
                    sq, cache_k, cache_v, sample_layer, casts=()):
    tm = 512
    seq = x.shape[1]
    nt = seq // tm
    steps = nb * nt
    bb = sq.shape[0] // steps
    cast_specs = [pl.BlockSpec((None, w.shape[1] // steps, w.shape[2]),
                               functools.partial(lambda b, t, l: (l, b * nt + t, 0), l=l))
                  for w, l in casts]
    cast_out_specs = [pl.BlockSpec((w.shape[1] // steps, w.shape[2]), lambda b, t: (b * nt + t, 0))
                      for w, _ in casts]
    xin = pl.BlockSpec((None, tm, D_MODEL), lambda b, t: (b + b0, t, 0))
    xout = pl.BlockSpec((None, tm, D_MODEL), lambda b, t: (b, t, 0))
    kts = pl.BlockSpec((None, None, XATT, N_MEM), lambda b, t: (layer, b + b0, 0, 0))
    vs = pl.BlockSpec((None, N_MEM, XATT), lambda b, t: (layer, b + b0, 0))
    sqs = pl.BlockSpec((bb,) + sq.shape[1:], lambda b, t: (b * nt + t, 0, 0))
    scs = pl.BlockSpec((None, bb) + cache_k.shape[2:], lambda b, t: (sample_layer, b * nt + t, 0, 0))
    return pl.pallas_call(
        functools.partial(_layer_a_prompt_kernel, n_casts=len(casts)),
        grid=(nb, nt),
        in_specs=[xin, kts, vs, _const_spec(win.shape), _const_spec(lng.shape), _const_spec(lnb.shape),
                  _const_spec(ws.shape), _const_spec(bias.shape), _const_spec(wout.shape),
                  _layer_spec(pre.shape, layer), _layer_spec(post.shape, layer), sqs, scs, scs] + cast_specs,
        out_specs=[xout, sqs] + cast_out_specs,
        out_shape=[jax.ShapeDtypeStruct((nb, seq, D_MODEL), F32), jax.ShapeDtypeStruct(sq.shape, F32)]
        + [jax.ShapeDtypeStruct(w.shape[1:], BF16) for w, _ in casts],
        scratch_shapes=[pltpu.VMEM((tm, MIX_WIDTH), BF16), pltpu.VMEM((tm, MIX_WIDTH), F32)],
        compiler_params=pltpu.CompilerParams(
            dimension_semantics=("arbitrary", "arbitrary"), vmem_limit_bytes=VMEM_LIMIT),
        name="layer_a_prompt",
    )(x, kt, vb, win, lng, lnb, ws, bias, wout, pre, post, sq, cache_k, cache_v, *[w for w, _ in casts])


def _layer_b_prompt_kernel(xa_ref, xb_ref, kt_ref, v_ref, win_ref, a2r_ref, a2i_ref, b2_ref, c2_ref, k2_ref,
                           d_ref, wglu_ref, bglu_ref, wout_ref, pre_ref, post_ref,
                           o_ref, hre_ref, him_ref,
                           ut_ref, bu_ref, hs_ref, yt_ref, y_ref, m_ref):
    nb, tc, _ = o_ref.shape
    tm = nb * tc
    half = tc // 2

    @pl.when(pl.program_id(0) == 0)
    def _():
        hre_ref[...] = jnp.zeros_like(hre_ref)
        him_ref[...] = jnp.zeros_like(him_ref)

    x = jnp.concatenate([xa_ref[...], xb_ref[...]], axis=0).reshape(tm, D_MODEL)
    h = _rms(x, pre_ref[...]).astype(BF16)
    u = _dot(h, win_ref[:, 0:BRANCH])
    for j in range(S5_TILES):
        for b in range(nb):
            ut_ref[j, pl.ds(b, tc, stride=nb), :] = u[b * tc:(b + 1) * tc, j * LANES:(j + 1) * LANES]
    for j in range(S5_TILES):
        ss = slice(j * S5_HALF, (j + 1) * S5_HALF)
        par = j % 2
        y2, hr, hi = _s5_tile(ut_ref[j], hre_ref[:, ss], him_ref[:, ss], a2r_ref[j], a2i_ref[j],
                              b2_ref[j], c2_ref[j], k2_ref[j], bu_ref.at[par], hs_ref.at[par], tc, nb)
        hre_ref[:, ss] = hr
        him_ref[:, ss] = hi
        for tp in range(2):
            yt_ref[par, tp] = y2[:, tp * LANES:(tp + 1) * LANES]
            for b in range(nb):
                y_ref[j, pl.ds(b * tc + tp, half, stride=2), :] = yt_ref[par, tp, pl.ds(b, half, stride=nb), :]
    y = _gelu(jnp.concatenate([y_ref[j] for j in range(S5_TILES)], axis=1) + d_ref[...] * u)
    branch = y * _sigmoid(_dot(y.astype(BF16), wglu_ref[...]) + bglu_ref[...])
    gate = _silu(_dot(h, win_ref[:, BRANCH + XATT:]))
    m_ref[:, 0:BRANCH] = (branch * gate[:, 0:BRANCH]).astype(BF16)
    q = _dot(h, win_ref[:, BRANCH:BRANCH + XATT])
    pieces = [(b, h) for b in range(nb) for h in range(N_XHEADS)]
    att = _attend([_head(q[b * tc:(b + 1) * tc], h) for b, h in pieces],
                  [kt_ref[b, h * XHEAD_DIM:(h + 1) * XHEAD_DIM, :] for b, h in pieces],
                  [_head(v_ref[b * N_MEM:(b + 1) * N_MEM, :], h) for b, h in pieces])
    for (b, h), a in zip(pieces, att):
        rs = slice(b * tc, (b + 1) * tc)
        cs = slice(BRANCH + h * XHEAD_DIM, BRANCH + (h + 1) * XHEAD_DIM)
        m_ref[rs, cs] = (a * gate[rs, cs]).astype(BF16)
    out = _dot(m_ref[...], wout_ref[...])
    o_ref[...] = (x + _rms(out, post_ref[...])).reshape(nb, tc, D_MODEL)


def _layer_b_prompt(xa, xb, kb, vb, win, s5, dvec, wglu, bglu, wout, pre, post, layer):
    tc = 64
    nb, seq = xa.shape[0] + xb.shape[0], xa.shape[1]
    tm = nb * tc
    xin = pl.BlockSpec((xa.shape[0], tc, D_MODEL), lambda t: (0, t, 0))
    xs = pl.BlockSpec((nb, tc, D_MODEL), lambda t: (0, t, 0))
    hs = pl.BlockSpec((nb, SSM_GROUPS * SSM_STATE), lambda t: (0, 0))
    ins = [kb, vb, win, *s5, dvec, wglu, bglu, wout, pre, post]
    stacked = [True, True] + [False] * (len(s5) + 5) + [True, True]
    return pl.pallas_call(
        _layer_b_prompt_kernel,
        grid=(seq // tc,),
        in_specs=[xin, xin] + [_layer_spec(a.shape, layer) if st else _const_spec(a.shape)
                               for a, st in zip(ins, stacked)],
        out_specs=[xs, hs, hs],
        out_shape=[jax.ShapeDtypeStruct((nb, seq, D_MODEL), F32),
                   jax.ShapeDtypeStruct((nb, SSM_GROUPS * SSM_STATE), F32),
                   jax.ShapeDtypeStruct((nb, SSM_GROUPS * SSM_STATE), F32)],
        scratch_shapes=[pltpu.VMEM((S5_TILES, tm, LANES), F32),
                        pltpu.VMEM((2, tm // 2, 2 * S5_HALF), F32),
                        pltpu.VMEM((2, tm // 2, 2 * S5_HALF), F32),
                        pltpu.VMEM((2, 2, tm // 2, LANES), F32),
                        pltpu.VMEM((S5_TILES, tm, LANES), F32),
                        pltpu.VMEM((tm, MIX_WIDTH), BF16)],
        compiler_params=pltpu.CompilerParams(
            dimension_semantics=("arbitrary",), vmem_limit_bytes=VMEM_LIMIT),
        name="layer_b_prompt",
    )(xa, xb, *ins)


Q_ROWS = N_XHEADS * DEC_SEQ


def _store_q_by_sequence(q_ref, q):
    for h in range(N_XHEADS):
        for t in range(DEC_SEQ):
            q_ref[pl.ds(h * DEC_SEQ + t, DEC_BATCH, stride=Q_ROWS), :] = _head(q[t * DEC_BATCH:(t + 1) * DEC_BATCH], h)


def _sample_pre_a_kernel(x_ref, win_ref, lng_ref, lnb_ref, wrow_ref, brow_ref, pre_ref,
                         v_ref, mbr_ref, q_ref, gatt_ref):
    nb = DEC_BATCH
    x = jnp.concatenate([x_ref[:, t, :] for t in range(DEC_SEQ)], axis=0)
    h = _rms(x, pre_ref[...]).astype(BF16)
    gate = _silu(_dot(h, win_ref[:, 2 * BRANCH + XATT:]))
    u = _gelu(_dot(h, win_ref[:, 0:BRANCH]))
    v = _layer_norm(_gelu(_dot(h, win_ref[:, BRANCH:2 * BRANCH])), lng_ref[...], lnb_ref[...])
    for t in range(DEC_SEQ):
        v_ref[:, t, :] = v[t * nb:(t + 1) * nb]
    for t in range(DEC_SEQ):
        rs = slice(t * nb, (t + 1) * nb)
        mixed = brow_ref[t:t + 1, :]
        for s in range(t + 1):
            mixed = mixed + wrow_ref[t * DEC_SEQ + s:t * DEC_SEQ + s + 1, :] * v[s * nb:(s + 1) * nb]
        mbr_ref[rs, :] = u[rs] * mixed * gate[rs, 0:BRANCH]
    _store_q_by_sequence(q_ref, _dot(h, win_ref[:, 2 * BRANCH:2 * BRANCH + XATT]))
    gatt_ref[...] = gate[:, BRANCH:]


def _sample_pre_a(x, win, lng, lnb, wrow, brow, pre, layer):
    rows = DEC_SEQ * DEC_BATCH
    ins = [x, win, lng, lnb, wrow, brow, pre]
    shapes = [(DEC_BATCH, DEC_SEQ, BRANCH), (rows, BRANCH), (DEC_BATCH * Q_ROWS, XHEAD_DIM), (rows, XATT)]
    return pl.pallas_call(
        _sample_pre_a_kernel,
        grid=(1,),
        in_specs=[_const_spec(a.shape) for a in ins[:-1]] + [_layer_spec(pre.shape, layer)],
        out_specs=[pl.BlockSpec(s, functools.partial(lambda i, n: (0,) * n, n=len(s))) for s in shapes],
        out_shape=[jax.ShapeDtypeStruct(s, F32) for s in shapes],
        compiler_params=pltpu.CompilerParams(
            dimension_semantics=("arbitrary",), vmem_limit_bytes=VMEM_LIMIT),
        name="sample_pre_a",
    )(*ins)


def _sample_pre_b_kernel(x_ref, win_ref, sre_ref, sim_ref, a2r_ref, a2i_ref, b2_ref, c2_ref, k2_ref, d_ref,
                         wglu_ref, bglu_ref, pre_ref,
                         mbr_ref, q_ref, gatt_ref, hre_ref, him_ref,
                         bu_ref, hs_ref, y_ref):
    nb = DEC_BATCH
    h = _rms(x_ref[...], pre_ref[...]).astype(BF16)
    u = _dot(h, win_ref[:, 0:BRANCH])
    for j in range(S5_TILES):
        ss = slice(j * S5_HALF, (j + 1) * S5_HALF)
        ls = slice(j * LANES, (j + 1) * LANES)
        y2, hr, hi = _s5_tile(u[:, ls], sre_ref[:, ss], sim_ref[:, ss], a2r_ref[j], a2i_ref[j],
                              b2_ref[j], c2_ref[j], k2_ref[j], bu_ref, hs_ref, DEC_SEQ, nb)
        hre_ref[:, ss] = hr
        him_ref[:, ss] = hi
        for pair in range(DEC_SEQ // 2):
            for tp in range(2):
                t = 2 * pair + tp
                y_ref[t * nb:(t + 1) * nb, ls] = y2[pair * nb:(pair + 1) * nb, tp * LANES:(tp + 1) * LANES]
    y = _gelu(y_ref[...] + d_ref[...] * u)
    branch = y * _sigmoid(_dot(y.astype(BF16), wglu_ref[...]) + bglu_ref[...])
    gate = _silu(_dot(h, win_ref[:, BRANCH + XATT:]))
    mbr_ref[...] = branch * gate[:, 0:BRANCH]
    _store_q_by_sequence(q_ref, _dot(h, win_ref[:, BRANCH:BRANCH + XATT]))
    gatt_ref[...] = gate[:, BRANCH:]


def _sample_pre_b(x, win, sre, sim, s5, dvec, wglu, bglu, pre, layer):
    rows = x.shape[0]
    ins = [x, win, sre, sim, *s5, dvec, wglu, bglu]
    shapes = [(rows, BRANCH), (DEC_BATCH * Q_ROWS, XHEAD_DIM), (rows, XATT), sre.shape, sim.shape]
    return pl.pallas_call(
        _sample_pre_b_kernel,
        grid=(1,),
        in_specs=[_const_spec(a.shape) for a in ins] + [_layer_spec(pre.shape, layer)],
        out_specs=[pl.BlockSpec(s, lambda i: (0, 0)) for s in shapes],
        out_shape=[jax.ShapeDtypeStruct(s, F32) for s in shapes],
        scratch_shapes=[pltpu.VMEM((rows // 2, 2 * S5_HALF), F32),
                        pltpu.VMEM((rows // 2, 2 * S5_HALF), F32),
                        pltpu.VMEM((rows, BRANCH), F32)],
        compiler_params=pltpu.CompilerParams(
            dimension_semantics=("arbitrary",), vmem_limit_bytes=VMEM_LIMIT),
        name="sample_pre_b",
    )(*ins, pre)


def _sample_attn_kernel(q_ref, k_ref, v_ref, o_ref):
    bb, rows, _ = q_ref.shape
    cols = k_ref.shape[1]
    row_h = (lax.broadcasted_iota(jnp.int32, (bb * rows, cols), 0) // DEC_SEQ) % N_XHEADS
    col_h = lax.broadcasted_iota(jnp.int32, (bb * rows, cols), 1) % N_XHEADS
    s = jnp.concatenate([_dot_nt(q_ref[j].astype(BF16), k_ref[j].astype(BF16)) for j in range(bb)], axis=0)
    s = jnp.where(row_h == col_h, s * (XHEAD_DIM ** -0.5), -1e30)
    p = jnp.exp(s - jnp.max(s, axis=-1, keepdims=True))
    inv = 1.0 / jnp.sum(p, axis=-1, keepdims=True)
    pb = p.astype(BF16)
    for j in range(bb):
        rs = slice(j * rows, (j + 1) * rows)
        o_ref[j] = _dot(pb[rs], v_ref[j].astype(BF16)) * inv[rs]


def _sample_post_kernel(x_ref, mbr_ref, att_ref, gatt_ref, wout_ref, post_ref, o_ref, m_ref):
    nb = DEC_BATCH
    for h in range(N_XHEADS):
        cs = slice(h * XHEAD_DIM, (h + 1) * XHEAD_DIM)
        for t in range(DEC_SEQ):
            rs = slice(t * nb, (t + 1) * nb)
            att = att_ref[pl.ds(h * DEC_SEQ + t, nb, stride=Q_ROWS), :]
            m_ref[rs, cs] = (att * gatt_ref[rs, cs]).astype(BF16)
    out = (_dot(mbr_ref[...].astype(BF16), wout_ref[0:BRANCH, :])
           + _dot(m_ref[...], wout_ref[BRANCH:, :]))
    y = _rms(out, post_ref[...])
    for t in range(DEC_SEQ):
        rs = slice(t * nb, (t + 1) * nb)
        x = x_ref[:, t, :] if len(x_ref.shape) == 3 else x_ref[rs, :]
        if len(o_ref.shape) == 3:
            o_ref[:, t, :] = x + y[rs]
        else:
            o_ref[rs, :] = x + y[rs]


def _sample_post(x, mbr, att, gatt, wout, post, layer, out_shape):
    ins = [x, mbr, att, gatt, wout, post]
    n_out = len(out_shape)
    return pl.pallas_call(
        _sample_post_kernel,
        grid=(1,),
        in_specs=[_const_spec(a.shape) for a in ins[:5]] + [_layer_spec(post.shape, layer)],
        out_specs=pl.BlockSpec(out_shape, lambda i: (0,) * n_out),
        out_shape=jax.ShapeDtypeStruct(out_shape, F32),
        scratch_shapes=[pltpu.VMEM((mbr.shape[0], XATT), BF16)],
        compiler_params=pltpu.CompilerParams(
            dimension_semantics=("arbitrary",), vmem_limit_bytes=VMEM_LIMIT),
        name="sample_post",
    )(*ins)


def kernel(x_prompt, x_sample, cache_mem_k, cache_mem_v, state_ssm_re, state_ssm_im, mem_prompt,
           w_in_a, ln_v_g, ln_v_b, w_spatial, b_spatial,
           w_in_b, ssm_lambda_re, ssm_lambda_im, ssm_log_dt, ssm_b_re, ssm_b_im, ssm_c_re, ssm_c_im,
           ssm_d, w_glu, b_glu,
           mem_norm_g, w_mem_k, w_mem_v, w_out, pre_norm_g, post_norm_g):
    depth = w_out.shape[0]
    wk = w_mem_k.astype(BF16)
    wv = w_mem_v.astype(BF16)
    pre = pre_norm_g.reshape(depth, 1, D_MODEL)
    post = post_norm_g.reshape(depth, 1, D_MODEL)
    lng = ln_v_g[0].reshape(1, BRANCH)
    lnb = ln_v_b[0].reshape(1, BRANCH)
    bglu = b_glu[0].reshape(1, BRANCH)
    dvec = ssm_d[0].reshape(1, BRANCH)
    bias = jnp.repeat(b_spatial[0].T, A_GDIM, axis=1)
    ws4 = w_spatial[0][:, :DEC_SEQ, :DEC_SEQ]
    wrow = jnp.repeat(ws4.transpose(1, 2, 0).reshape(DEC_SEQ * DEC_SEQ, A_GROUPS), A_GDIM, axis=1)
    brow = bias[:DEC_SEQ]

    *s5, win_a = _s5_prep(ssm_lambda_re[0], ssm_lambda_im[0], ssm_log_dt[0],
                          ssm_b_re[0], ssm_b_im[0], ssm_c_re[0], ssm_c_im[0], casts=[(w_in_a, 0)])

    mem = mem_prompt.reshape(BATCH * N_MEM, D_MODEL)
    mk, mv, mkt, mvb, wglu, wout0 = _mem_kv(mem, mem_norm_g.reshape(depth, 1, D_MODEL), wk, wv,
                                            casts=[(w_glu, 0), (w_out, 0)])

    ck = cache_mem_k.reshape(depth, DEC_BATCH, N_MEM * N_XHEADS, XHEAD_DIM)
    cv = cache_mem_v.reshape(depth, DEC_BATCH, N_MEM * N_XHEADS, XHEAD_DIM)
    q_shape = (DEC_BATCH, Q_ROWS, XHEAD_DIM)
    half = BATCH // 2
    layer_a = functools.partial(_layer_a_prompt, x_prompt, kt=mkt, vb=mvb, win=win_a, lng=lng, lnb=lnb,
                                ws=w_spatial[0], bias=bias, wout=wout0, pre=pre, post=post, layer=0,
                                cache_k=ck, cache_v=cv)
    rows_tb = (DEC_SEQ * DEC_BATCH, D_MODEL)
    v_rows, mbr, q, gatt = _sample_pre_a(x_sample, win_a, lng, lnb, wrow, brow, pre, 0)
    yp_a, att, win_b = layer_a(b0=0, nb=half, sq=q.reshape(q_shape), sample_layer=0, casts=[(w_in_b, 0)])
    xs = _sample_post(x_sample, mbr, att.reshape(q.shape), gatt, wout0, post, 0, rows_tb)
    sre = state_ssm_re[0].reshape(DEC_BATCH, SSM_GROUPS * SSM_STATE)
    sim = state_ssm_im[0].reshape(DEC_BATCH, SSM_GROUPS * SSM_STATE)
    mbr, q, gatt, hs_re, hs_im = _sample_pre_b(xs, win_b, sre, sim, s5, dvec, wglu, bglu, pre, 1)
    yp_b, att, wout1 = layer_a(b0=half, nb=half, sq=q.reshape(q_shape), sample_layer=1, casts=[(w_out, 1)])
    ys = _sample_post(xs, mbr, att.reshape(q.shape), gatt, wout1, post, 1, x_sample.shape)
    yp, hp_re, hp_im = _layer_b_prompt(yp_a, yp_b, mkt, mvb, win_b, s5, dvec, wglu, bglu, wout1, pre, post, 1)

    kv_shape = (depth, BATCH, N_MEM, N_XHEADS, XHEAD_DIM)
    st_p = (1, BATCH, SSM_GROUPS, SSM_STATE)
    st_s = (1, DEC_BATCH, SSM_GROUPS, SSM_STATE)
    return (yp, ys, mk.reshape(kv_shape), mv.reshape(kv_shape),
            hp_re.reshape(st_p), hp_im.reshape(st_p), hs_re.reshape(st_s), hs_im.reshape(st_s),
            v_rows[None])
```

```python
import functools
import math

import jax
import jax.numpy as jnp
from jax import lax
from jax.experimental import pallas as pl
from jax.experimental.pallas import tpu as pltpu

D_MODEL = 1024
BATCH = 8
SEQ = 2048
DEC_BATCH = 128
DEC_SEQ = 4
BRANCH = 1536
N_XHEADS = 4
XHEAD_DIM = 128
XATT = N_XHEADS * XHEAD_DIM
MIX_WIDTH = BRANCH + XATT
N_MEM = 256
CHUNK = 128
A_GROUPS = 8
A_GDIM = BRANCH // A_GROUPS
SSM_GCH = 16
SSM_GROUPS = 96
SSM_STATE = 64
EPS = 1e-6

LANES = 128
S5_TILE_GROUPS = LANES // SSM_GCH
S5_TILES = SSM_GROUPS // S5_TILE_GROUPS
S5_HALF = S5_TILE_GROUPS * SSM_STATE
VMEM_LIMIT = 60 * 1024 * 1024
A_STREAM_ROWS = 256

F32 = jnp.float32
BF16 = jnp.bfloat16


def _dot(a, b):
    return jnp.dot(a, b, preferred_element_type=F32)


def _rms(x, g):
    return x * lax.rsqrt(jnp.mean(x * x, axis=-1, keepdims=True) + EPS) * g


def _layer_norm(x, g, b):
    mu = jnp.mean(x, axis=-1, keepdims=True)
    xc = x - mu
    var = jnp.mean(xc * xc, axis=-1, keepdims=True)
    return xc * lax.rsqrt(var + EPS) * g + b


def _gelu(x):
    c = math.sqrt(2.0 / math.pi)
    hx = 0.5 * x
    return hx + hx * jnp.tanh(x * (c + (c * 0.044715) * (x * x)))


def _sigmoid(x):
    return 0.5 + 0.5 * jnp.tanh(0.5 * x)


def _silu(x):
    hx = 0.5 * x
    return hx + hx * jnp.tanh(hx)


def _dot_nt(a, b):
    return lax.dot_general(a, b, (((1,), (1,)), ((), ())), preferred_element_type=F32)


def _attend(qs, kts, vs):
    r = qs[0].shape[0]
    s = jnp.concatenate([_dot(q.astype(BF16), kt) for q, kt in zip(qs, kts)], axis=0)
    s = s * (XHEAD_DIM ** -0.5)
    p = jnp.exp(s - jnp.max(s, axis=-1, keepdims=True))
    inv = 1.0 / jnp.sum(p, axis=-1, keepdims=True)
    pb = p.astype(BF16)
    return [_dot(pb[i * r:(i + 1) * r], v) * inv[i * r:(i + 1) * r] for i, v in enumerate(vs)]


def _head(a, h):
    return a[:, h * XHEAD_DIM:(h + 1) * XHEAD_DIM]


def _const_spec(shape):
    nd = len(shape)
    return pl.BlockSpec(shape, lambda *_: (0,) * nd, pipeline_mode=pl.Buffered(1))


def _layer_spec(shape, layer):
    nd = len(shape)
    return pl.BlockSpec((None,) + tuple(shape[1:]), lambda *_: (layer,) + (0,) * (nd - 1),
                        pipeline_mode=pl.Buffered(1))


def _lam_bar(lr, li, ldt):
    dt = jnp.exp(ldt)
    e = jnp.exp(lr * dt)
    return e * jnp.cos(li * dt), e * jnp.sin(li * dt)


def _cast_rider_specs(casts, steps):
    in_specs = [pl.BlockSpec((None, w.shape[1] // steps, w.shape[2]),
                             functools.partial(lambda i, l: (l, i, 0), l=l)) for w, l in casts]
    out_specs = [pl.BlockSpec((w.shape[1] // steps, w.shape[2]), lambda i: (i, 0)) for w, _ in casts]
    out_shape = [jax.ShapeDtypeStruct(w.shape[1:], BF16) for w, _ in casts]
    return in_specs, out_specs, out_shape


def _s5_prep_kernel(*refs, n_casts):
    n_in, n_out = 7, 5
    ins, cast_in = refs[:n_in], refs[n_in:n_in + n_casts]
    outs, cast_out = refs[n_in + n_casts:n_in + n_casts + n_out], refs[n_in + n_casts + n_out:]
    for src, dst in zip(cast_in, cast_out):
        dst[...] = src[...].astype(BF16)
    for j in range(ins[0].shape[0]):
        _s5_prep_tile(*[r.at[pl.ds(j, 1)] for r in (*ins, *outs)])


def _s5_prep_tile(lr_ref, li_ref, ldt_ref, br_ref, bi_ref, cr_ref, ci_ref,
                  a2r_ref, a2i_ref, b2_ref, c2_ref, k2_ref):
    lr = lr_ref[0]
    li = li_ref[0]
    ar, ai = _lam_bar(lr, li, ldt_ref[0])
    a2r = ar * ar - ai * ai
    a2i = 2.0 * ar * ai
    a2r_ref[0] = a2r
    a2i_ref[0] = a2i
    nr = ar - 1.0
    den = lr * lr + li * li
    kr = (nr * lr + ai * li) / den
    ki = (ai * lr - nr * li) / den
    shape = (LANES, S5_HALF)
    own = (lax.broadcasted_iota(jnp.int32, shape, 0) // SSM_GCH
           == lax.broadcasted_iota(jnp.int32, shape, 1) // SSM_STATE)

    def own_blocks(ref):
        return jnp.where(own, jnp.concatenate([ref[0]] * S5_TILE_GROUPS, axis=0), 0.0)

    br, bi, cr, ci = own_blocks(br_ref), own_blocks(bi_ref), own_blocks(cr_ref), own_blocks(ci_ref)
    bbr = kr * br - ki * bi
    bbi = kr * bi + ki * br
    b_tok1 = jnp.concatenate([bbr, bbi], axis=1).astype(BF16)
    b_tok0 = jnp.concatenate([ar * bbr - ai * bbi, ar * bbi + ai * bbr], axis=1).astype(BF16)
    b2_ref[0, 0:LANES, :] = b_tok0
    b2_ref[0, LANES:, :] = b_tok1
    c_tok0 = jnp.concatenate([cr * ar - ci * ai, -(cr * ai + ci * ar)], axis=1)
    c_tok1 = jnp.concatenate([cr * a2r - ci * a2i, -(cr * a2i + ci * a2r)], axis=1)
    c2_ref[0, :, 0:LANES] = c_tok0.T.astype(BF16)
    c2_ref[0, :, LANES:] = c_tok1.T.astype(BF16)
    c_plain = jnp.concatenate([cr, -ci], axis=1).astype(BF16)
    k0 = _dot_nt(b_tok1, c_plain).astype(BF16)
    k1 = _dot_nt(b_tok0, c_plain).astype(BF16)
    k2_ref[0, 0:LANES, 0:LANES] = k0
    k2_ref[0, 0:LANES, LANES:] = k1
    k2_ref[0, LANES:, 0:LANES] = jnp.zeros_like(k0)
    k2_ref[0, LANES:, LANES:] = k0


def _s5_prep(lam_re, lam_im, log_dt, b_re, b_im, c_re, c_im, casts=()):
    nt, tg = S5_TILES, S5_TILE_GROUPS
    lr = lam_re.reshape(nt, 1, S5_HALF)
    li = lam_im.reshape(nt, 1, S5_HALF)
    ldt = jnp.broadcast_to(log_dt[:, None], (SSM_GROUPS, SSM_STATE)).reshape(nt, 1, S5_HALF)

    def b_cp(b):
        return b.reshape(nt, tg, SSM_STATE, SSM_GCH).transpose(0, 3, 1, 2).reshape(nt, SSM_GCH, S5_HALF)

    def c_cp(c):
        return c.reshape(nt, tg, SSM_GCH, SSM_STATE).transpose(0, 2, 1, 3).reshape(nt, SSM_GCH, S5_HALF)

    per_step = 3

    def spec(*shape):
        return pl.BlockSpec((per_step,) + shape, lambda j: (j, 0, 0))

    vec, chan = spec(1, S5_HALF), spec(SSM_GCH, S5_HALF)
    out_shapes = [(1, S5_HALF), (1, S5_HALF), (2 * LANES, 2 * S5_HALF), (2 * S5_HALF, 2 * LANES),
                  (2 * LANES, 2 * LANES)]
    out_dtypes = [F32, F32, BF16, BF16, BF16]
    steps = nt // per_step
    cast_in, cast_out, cast_shape = _cast_rider_specs(casts, steps)
    return pl.pallas_call(
        functools.partial(_s5_prep_kernel, n_casts=len(casts)),
        grid=(steps,),
        in_specs=[vec, vec, vec, chan, chan, chan, chan] + cast_in,
        out_specs=[spec(*s) for s in out_shapes] + cast_out,
        out_shape=[jax.ShapeDtypeStruct((nt,) + s, d) for s, d in zip(out_shapes, out_dtypes)] + cast_shape,
        compiler_params=pltpu.CompilerParams(
            dimension_semantics=("arbitrary",), vmem_limit_bytes=VMEM_LIMIT),
        name="s5_prep",
    )(lr, li, ldt, b_cp(b_re), b_cp(b_im), c_cp(c_re), c_cp(c_im), *[w for w, _ in casts])


def _mem_kv_kernel(mem_ref, g_ref, wk_ref, wv_ref, *rest, n_casts):
    cast_in, (k_ref, v_ref, kt_ref, vb_ref), cast_out = rest[:n_casts], rest[n_casts:n_casts + 4], rest[n_casts + 4:]
    for src, dst in zip(cast_in, cast_out):
        dst[...] = src[...].astype(BF16)
    mem = mem_ref[...]
    scaled = mem * lax.rsqrt(jnp.mean(mem * mem, axis=-1, keepdims=True) + EPS)
    for i in range(g_ref.shape[0]):
        m = (scaled * g_ref[i]).astype(BF16)
        k = _dot(m, wk_ref[i])
        v = _dot(m, wv_ref[i])
        vb_ref[i] = v.astype(BF16)
        for b in range(k_ref.shape[1]):
            kb = k[b * N_MEM:(b + 1) * N_MEM]
            vb = v[b * N_MEM:(b + 1) * N_MEM]
            kt_ref[i, b] = kb.T.astype(BF16)
            for h in range(N_XHEADS):
                rows = pl.ds(h, N_MEM, stride=N_XHEADS)
                k_ref[i, b, rows, :] = _head(kb, h)
                v_ref[i, b, rows, :] = _head(vb, h)


def _mem_kv(mem, g, wk, wv, casts=()):
    rows = mem.shape[0]
    depth = g.shape[0]
    nb = 1
    tm = nb * N_MEM
    steps = rows // tm
    cast_in, cast_out, cast_shape = _cast_rider_specs(casts, steps)
    return pl.pallas_call(
        functools.partial(_mem_kv_kernel, n_casts=len(casts)),
        grid=(steps,),
        in_specs=[pl.BlockSpec((tm, D_MODEL), lambda r: (r, 0)),
                  _const_spec(g.shape), _const_spec(wk.shape), _const_spec(wv.shape)] + cast_in,
        out_specs=[pl.BlockSpec((depth, nb, N_MEM * N_XHEADS, XHEAD_DIM), lambda r: (0, r, 0, 0))] * 2
        + [pl.BlockSpec((depth, nb, XATT, N_MEM), lambda r: (0, r, 0, 0)),
           pl.BlockSpec((depth, tm, XATT), lambda r: (0, r, 0))] + cast_out,
        out_shape=[jax.ShapeDtypeStruct((depth, rows // N_MEM, N_MEM * N_XHEADS, XHEAD_DIM), F32)] * 2
        + [jax.ShapeDtypeStruct((depth, rows // N_MEM, XATT, N_MEM), BF16),
           jax.ShapeDtypeStruct((depth, rows, XATT), BF16)] + cast_shape,
        compiler_params=pltpu.CompilerParams(
            dimension_semantics=("arbitrary",), vmem_limit_bytes=VMEM_LIMIT),
        name="mem_kv",
    )(mem, g, wk, wv, *[w for w, _ in casts])


def _masked_spatial(ws_ref):
    row = lax.broadcasted_iota(jnp.int32, (CHUNK, CHUNK), 0)
    col = lax.broadcasted_iota(jnp.int32, (CHUNK, CHUNK), 1)
    return [jnp.where(row >= col, ws_ref[g], 0.0).astype(BF16) for g in range(A_GROUPS)]


def _gate_chunk(wm, vc, pair):
    base = pair * 2 * A_GDIM
    even = _dot(wm[2 * pair], vc[:, base:base + 2 * LANES])
    odd = _dot(wm[2 * pair + 1], vc[:, base + LANES:base + 3 * LANES])
    lane = lax.broadcasted_iota(jnp.int32, (CHUNK, LANES), 1)
    mid = jnp.where(lane < A_GDIM - LANES, even[:, LANES:], odd[:, :LANES])
    return jnp.concatenate([even[:, :LANES], mid, odd[:, LANES:]], axis=1)


def _s5_tile(u, hr, hi, a2r, a2i, b2, c2, k2, bu_ref, hs_ref, steps, nb):
    pairs = steps // 2
    u3 = u.reshape(pairs, 2 * nb, LANES)
    u2 = jnp.concatenate([u3[:, 0:nb, :].reshape(pairs * nb, LANES),
                          u3[:, nb:, :].reshape(pairs * nb, LANES)], axis=1).astype(BF16)
    bu_ref[...] = _dot(u2, b2)
    a2r = jnp.broadcast_to(a2r, (nb, S5_HALF))
    a2i = jnp.broadcast_to(a2i, (nb, S5_HALF))
    for k in range(pairs):
        rows = slice(k * nb, (k + 1) * nb)
        hs_ref[rows, 0:S5_HALF] = hr
        hs_ref[rows, S5_HALF:] = hi
        hr, hi = (a2r * hr - a2i * hi + bu_ref[rows, 0:S5_HALF],
                  a2r * hi + a2i * hr + bu_ref[rows, S5_HALF:])
    y2 = _dot(hs_ref[...].astype(BF16), c2) + _dot(u2, k2)
    return y2, hr, hi


def _layer_a_prompt_kernel(*refs, n_casts):
    (x_ref, kt_ref, v_ref, win_ref, lng_ref, lnb_ref, ws_ref, bias_ref,
     wout_ref, pre_ref, post_ref, sq_ref, sk_ref, sv_ref) = refs[:14]
    cast_in = refs[14:14 + n_casts]
    o_ref, satt_ref = refs[14 + n_casts:16 + n_casts]
    cast_out = refs[16 + n_casts:16 + 2 * n_casts]
    m_ref, g_ref = refs[16 + 2 * n_casts:]
    for src, dst in zip(cast_in, cast_out):
        dst[...] = src[...].astype(BF16)
    wm = _masked_spatial(ws_ref)
    heads = range(N_XHEADS)
    n_streams = x_ref.shape[0] // A_STREAM_ROWS

    def project(s):
        x = x_ref[pl.ds(s * A_STREAM_ROWS, A_STREAM_ROWS), :]
        h = _rms(x, pre_ref[...]).astype(BF16)
        v = _layer_norm(_gelu(_dot(h, win_ref[:, BRANCH:2 * BRANCH])), lng_ref[...], lnb_ref[...])
        u = _gelu(_dot(h, win_ref[:, 0:BRANCH]))
        gate = _silu(_dot(h, win_ref[:, 2 * BRANCH + XATT:]))
        g_ref[pl.ds(s * A_STREAM_ROWS, A_STREAM_ROWS), 0:BRANCH] = u * gate[:, 0:BRANCH]
        g_ref[pl.ds(s * A_STREAM_ROWS, A_STREAM_ROWS), BRANCH:] = gate[:, BRANCH:]
        q = _dot(h, win_ref[:, 2 * BRANCH:2 * BRANCH + XATT])
        return x, v.astype(BF16), q

    def mix_and_emit(s, x, vb, q):
        m = m_ref.at[pl.ds(s * A_STREAM_ROWS, A_STREAM_ROWS)]
        g = g_ref.at[pl.ds(s * A_STREAM_ROWS, A_STREAM_ROWS)]
        for c in range(A_STREAM_ROWS // CHUNK):
            rs = slice(c * CHUNK, (c + 1) * CHUNK)
            vc = vb[rs]
            for pair in range(A_GROUPS // 2):
                cs = slice(pair * 2 * A_GDIM, (pair + 1) * 2 * A_GDIM)
                mixed = _gate_chunk(wm, vc, pair) + bias_ref[:, cs]
                m[rs, cs] = (g[rs, cs] * mixed).astype(BF16)
        kv = v_ref[...]
        att = _attend([_head(q, h) for h in heads],
                      [kt_ref[h * XHEAD_DIM:(h + 1) * XHEAD_DIM, :] for h in heads],
                      [_head(kv, h) for h in heads])
        for h in heads:
            cs = slice(BRANCH + h * XHEAD_DIM, BRANCH + (h + 1) * XHEAD_DIM)
            m[:, cs] = (att[h] * g[:, cs]).astype(BF16)
        out = _dot(m[...], wout_ref[...])
        o_ref[pl.ds(s * A_STREAM_ROWS, A_STREAM_ROWS), :] = x + _rms(out, post_ref[...])

    pending = project(0)
    for s in range(n_streams):
        following = project(s + 1) if s + 1 < n_streams else None
        mix_and_emit(s, *pending)
        pending = following
    _sample_attn_kernel(sq_ref, sk_ref, sv_ref, satt_ref)


def _layer_a_prompt(x, b0, nb, kt, vb, win, lng, lnb, ws, bias, wout, pre, post, layer,
                    sq, cache_k, cache_v, sample_layer, casts=()):
    tm = 512
    seq = x.shape[1]
    nt = seq // tm
    steps = nb * nt
    bb = sq.shape[0] // steps
    cast_specs = [pl.BlockSpec((None, w.shape[1] // steps, w.shape[2]),
                               functools.partial(lambda b, t, l: (l, b * nt + t, 0), l=l))
                  for w, l in casts]
    cast_out_specs = [pl.BlockSpec((w.shape[1] // steps, w.shape[2]), lambda b, t: (b * nt + t, 0))
                      for w, _ in casts]
    xin = pl.BlockSpec((None, tm, D_MODEL), lambda b, t: (b + b0, t, 0))
    xout = pl.BlockSpec((None, tm, D_MODEL), lambda b, t: (b, t, 0))
    kts = pl.BlockSpec((None, None, XATT, N_MEM), lambda b, t: (layer, b + b0, 0, 0))
    vs = pl.BlockSpec((None, N_MEM, XATT), lambda b, t: (layer, b + b0, 0))
    sqs = pl.BlockSpec((bb,) + sq.shape[1:], lambda b, t: (b * nt + t, 0, 0))
    scs = pl.BlockSpec((None, bb) + cache_k.shape[2:], lambda b, t: (sample_layer, b * nt + t, 0, 0))
    return pl.pallas_call(
        functools.partial(_layer_a_prompt_kernel, n_casts=len(casts)),
        grid=(nb, nt),
        in_specs=[xin, kts, vs, _const_spec(win.shape), _const_spec(lng.shape), _const_spec(lnb.shape),
                  _const_spec(ws.shape), _const_spec(bias.shape), _const_spec(wout.shape),
                  _layer_spec(pre.shape, layer), _layer_spec(post.shape, layer), sqs, scs, scs] + cast_specs,
        out_specs=[xout, sqs] + cast_out_specs,
        out_shape=[jax.ShapeDtypeStruct((nb, seq, D_MODEL), F32), jax.ShapeDtypeStruct(sq.shape, F32)]
        + [jax.ShapeDtypeStruct(w.shape[1:], BF16) for w, _ in casts],
        scratch_shapes=[pltpu.VMEM((tm, MIX_WIDTH), BF16), pltpu.VMEM((tm, MIX_WIDTH), F32)],
        compiler_params=pltpu.CompilerParams(
            dimension_semantics=("arbitrary", "arbitrary"), vmem_limit_bytes=VMEM_LIMIT),
        name="layer_a_prompt",
    )(x, kt, vb, win, lng, lnb, ws, bias, wout, pre, post, sq, cache_k, cache_v, *[w for w, _ in casts])


def _layer_b_prompt_kernel(xa_ref, xb_ref, kt_ref, v_ref, win_ref, a2r_ref, a2i_ref, b2_ref, c2_ref, k2_ref,
                           d_ref, wglu_ref, bglu_ref, wout_ref, pre_ref, post_ref,
                           o_ref, hre_ref, him_ref,
                           ut_ref, bu_ref, hs_ref, yt_ref, y_ref, m_ref):
    nb, tc, _ = o_ref.shape
    tm = nb * tc
    half = tc // 2

    @pl.when(pl.program_id(0) == 0)
    def _():
        hre_ref[...] = jnp.zeros_like(hre_ref)
        him_ref[...] = jnp.zeros_like(him_ref)

    x = jnp.concatenate([xa_ref[...], xb_ref[...]], axis=0).reshape(tm, D_MODEL)
    h = _rms(x, pre_ref[...]).astype(BF16)
    u = _dot(h, win_ref[:, 0:BRANCH])
    for j in range(S5_TILES):
        for b in range(nb):
            ut_ref[j, pl.ds(b, tc, stride=nb), :] = u[b * tc:(b + 1) * tc, j * LANES:(j + 1) * LANES]
    for j in range(S5_TILES):
        ss = slice(j * S5_HALF, (j + 1) * S5_HALF)
        par = j % 2
        y2, hr, hi = _s5_tile(ut_ref[j], hre_ref[:, ss], him_ref[:, ss], a2r_ref[j], a2i_ref[j],
                              b2_ref[j], c2_ref[j], k2_ref[j], bu_ref.at[par], hs_ref.at[par], tc, nb)
        hre_ref[:, ss] = hr
        him_ref[:, ss] = hi
        for tp in range(2):
            yt_ref[par, tp] = y2[:, tp * LANES:(tp + 1) * LANES]
            for b in range(nb):
                y_ref[j, pl.ds(b * tc + tp, half, stride=2), :] = yt_ref[par, tp, pl.ds(b, half, stride=nb), :]
    y = _gelu(jnp.concatenate([y_ref[j] for j in range(S5_TILES)], axis=1) + d_ref[...] * u)
    branch = y * _sigmoid(_dot(y.astype(BF16), wglu_ref[...]) + bglu_ref[...])
    gate = _silu(_dot(h, win_ref[:, BRANCH + XATT:]))
    m_ref[:, 0:BRANCH] = (branch * gate[:, 0:BRANCH]).astype(BF16)
    q = _dot(h, win_ref[:, BRANCH:BRANCH + XATT])
    pieces = [(b, h) for b in range(nb) for h in range(N_XHEADS)]
    att = _attend([_head(q[b * tc:(b + 1) * tc], h) for b, h in pieces],
                  [kt_ref[b, h * XHEAD_DIM:(h + 1) * XHEAD_DIM, :] for b, h in pieces],
                  [_head(v_ref[b * N_MEM:(b + 1) * N_MEM, :], h) for b, h in pieces])
    for (b, h), a in zip(pieces, att):
        rs = slice(b * tc, (b + 1) * tc)
        cs = slice(BRANCH + h * XHEAD_DIM, BRANCH + (h + 1) * XHEAD_DIM)
        m_ref[rs, cs] = (a * gate[rs, cs]).astype(BF16)
    out = _dot(m_ref[...], wout_ref[...])
    o_ref[...] = (x + _rms(out, post_ref[...])).reshape(nb, tc, D_MODEL)


def _layer_b_prompt(xa, xb, kb, vb, win, s5, dvec, wglu, bglu, wout, pre, post, layer):
    tc = 64
    nb, seq = xa.shape[0] + xb.shape[0], xa.shape[1]
    tm = nb * tc
    xin = pl.BlockSpec((xa.shape[0], tc, D_MODEL), lambda t: (0, t, 0))
    xs = pl.BlockSpec((nb, tc, D_MODEL), lambda t: (0, t, 0))
    hs = pl.BlockSpec((nb, SSM_GROUPS * SSM_STATE), lambda t: (0, 0))
    ins = [kb, vb, win, *s5, dvec, wglu, bglu, wout, pre, post]
    stacked = [True, True] + [False] * (len(s5) + 5) + [True, True]
    return pl.pallas_call(
        _layer_b_prompt_kernel,
        grid=(seq // tc,),
        in_specs=[xin, xin] + [_layer_spec(a.shape, layer) if st else _const_spec(a.shape)
                               for a, st in zip(ins, stacked)],
        out_specs=[xs, hs, hs],
        out_shape=[jax.ShapeDtypeStruct((nb, seq, D_MODEL), F32),
                   jax.ShapeDtypeStruct((nb, SSM_GROUPS * SSM_STATE), F32),
                   jax.ShapeDtypeStruct((nb, SSM_GROUPS * SSM_STATE), F32)],
        scratch_shapes=[pltpu.VMEM((S5_TILES, tm, LANES), F32),
                        pltpu.VMEM((2, tm // 2, 2 * S5_HALF), F32),
                        pltpu.VMEM((2, tm // 2, 2 * S5_HALF), F32),
                        pltpu.VMEM((2, 2, tm // 2, LANES), F32),
                        pltpu.VMEM((S5_TILES, tm, LANES), F32),
                        pltpu.VMEM((tm, MIX_WIDTH), BF16)],
        compiler_params=pltpu.CompilerParams(
            dimension_semantics=("arbitrary",), vmem_limit_bytes=VMEM_LIMIT),
        name="layer_b_prompt",
    )(xa, xb, *ins)


Q_ROWS = N_XHEADS * DEC_SEQ


def _store_q_by_sequence(q_ref, q):
    for h in range(N_XHEADS):
        for t in range(DEC_SEQ):
            q_ref[pl.ds(h * DEC_SEQ + t, DEC_BATCH, stride=Q_ROWS), :] = _head(q[t * DEC_BATCH:(t + 1) * DEC_BATCH], h)


def _sample_pre_a_kernel(x_ref, win_ref, lng_ref, lnb_ref, wrow_ref, brow_ref, pre_ref,
                         v_ref, mbr_ref, q_ref, gatt_ref):
    nb = DEC_BATCH
    x = jnp.concatenate([x_ref[:, t, :] for t in range(DEC_SEQ)], axis=0)
    h = _rms(x, pre_ref[...]).astype(BF16)
    gate = _silu(_dot(h, win_ref[:, 2 * BRANCH + XATT:]))
    u = _gelu(_dot(h, win_ref[:, 0:BRANCH]))
    v = _layer_norm(_gelu(_dot(h, win_ref[:, BRANCH:2 * BRANCH])), lng_ref[...], lnb_ref[...])
    for t in range(DEC_SEQ):
        v_ref[:, t, :] = v[t * nb:(t + 1) * nb]
    for t in range(DEC_SEQ):
        rs = slice(t * nb, (t + 1) * nb)
        mixed = brow_ref[t:t + 1, :]
        for s in range(t + 1):
            mixed = mixed + wrow_ref[t * DEC_SEQ + s:t * DEC_SEQ + s + 1, :] * v[s * nb:(s + 1) * nb]
        mbr_ref[rs, :] = u[rs] * mixed * gate[rs, 0:BRANCH]
    _store_q_by_sequence(q_ref, _dot(h, win_ref[:, 2 * BRANCH:2 * BRANCH + XATT]))
    gatt_ref[...] = gate[:, BRANCH:]


def _sample_pre_a(x, win, lng, lnb, wrow, brow, pre, layer):
    rows = DEC_SEQ * DEC_BATCH
    ins = [x, win, lng, lnb, wrow, brow, pre]
    shapes = [(DEC_BATCH, DEC_SEQ, BRANCH), (rows, BRANCH), (DEC_BATCH * Q_ROWS, XHEAD_DIM), (rows, XATT)]
    return pl.pallas_call(
        _sample_pre_a_kernel,
        grid=(1,),
        in_specs=[_const_spec(a.shape) for a in ins[:-1]] + [_layer_spec(pre.shape, layer)],
        out_specs=[pl.BlockSpec(s, functools.partial(lambda i, n: (0,) * n, n=len(s))) for s in shapes],
        out_shape=[jax.ShapeDtypeStruct(s, F32) for s in shapes],
        compiler_params=pltpu.CompilerParams(
            dimension_semantics=("arbitrary",), vmem_limit_bytes=VMEM_LIMIT),
        name="sample_pre_a",
    )(*ins)


def _sample_pre_b_kernel(x_ref, win_ref, sre_ref, sim_ref, a2r_ref, a2i_ref, b2_ref, c2_ref, k2_ref, d_ref,
                         wglu_ref, bglu_ref, pre_ref,
                         mbr_ref, q_ref, gatt_ref, hre_ref, him_ref,
                         bu_ref, hs_ref, y_ref):
    nb = DEC_BATCH
    h = _rms(x_ref[...], pre_ref[...]).astype(BF16)
    u = _dot(h, win_ref[:, 0:BRANCH])
    for j in range(S5_TILES):
        ss = slice(j * S5_HALF, (j + 1) * S5_HALF)
        ls = slice(j * LANES, (j + 1) * LANES)
        y2, hr, hi = _s5_tile(u[:, ls], sre_ref[:, ss], sim_ref[:, ss], a2r_ref[j], a2i_ref[j],
                              b2_ref[j], c2_ref[j], k2_ref[j], bu_ref, hs_ref, DEC_SEQ, nb)
        hre_ref[:, ss] = hr
        him_ref[:, ss] = hi
        for pair in range(DEC_SEQ // 2):
            for tp in range(2):
                t = 2 * pair + tp
                y_ref[t * nb:(t + 1) * nb, ls] = y2[pair * nb:(pair + 1) * nb, tp * LANES:(tp + 1) * LANES]
    y = _gelu(y_ref[...] + d_ref[...] * u)
    branch = y * _sigmoid(_dot(y.astype(BF16), wglu_ref[...]) + bglu_ref[...])
    gate = _silu(_dot(h, win_ref[:, BRANCH + XATT:]))
    mbr_ref[...] = branch * gate[:, 0:BRANCH]
    _store_q_by_sequence(q_ref, _dot(h, win_ref[:, BRANCH:BRANCH + XATT]))
    gatt_ref[...] = gate[:, BRANCH:]


def _sample_pre_b(x, win, sre, sim, s5, dvec, wglu, bglu, pre, layer):
    rows = x.shape[0]
    ins = [x, win, sre, sim, *s5, dvec, wglu, bglu]
    shapes = [(rows, BRANCH), (DEC_BATCH * Q_ROWS, XHEAD_DIM), (rows, XATT), sre.shape, sim.shape]
    return pl.pallas_call(
        _sample_pre_b_kernel,
        grid=(1,),
        in_specs=[_const_spec(a.shape) for a in ins] + [_layer_spec(pre.shape, layer)],
        out_specs=[pl.BlockSpec(s, lambda i: (0, 0)) for s in shapes],
        out_shape=[jax.ShapeDtypeStruct(s, F32) for s in shapes],
        scratch_shapes=[pltpu.VMEM((rows // 2, 2 * S5_HALF), F32),
                        pltpu.VMEM((rows // 2, 2 * S5_HALF), F32),
                        pltpu.VMEM((rows, BRANCH), F32)],
        compiler_params=pltpu.CompilerParams(
            dimension_semantics=("arbitrary",), vmem_limit_bytes=VMEM_LIMIT),
        name="sample_pre_b",
    )(*ins, pre)


def _sample_attn_kernel(q_ref, k_ref, v_ref, o_ref):
    bb, rows, _ = q_ref.shape
    cols = k_ref.shape[1]
    row_h = (lax.broadcasted_iota(jnp.int32, (bb * rows, cols), 0) // DEC_SEQ) % N_XHEADS
    col_h = lax.broadcasted_iota(jnp.int32, (bb * rows, cols), 1) % N_XHEADS
    s = jnp.concatenate([_dot_nt(q_ref[j].astype(BF16), k_ref[j].astype(BF16)) for j in range(bb)], axis=0)
    s = jnp.where(row_h == col_h, s * (XHEAD_DIM ** -0.5), -1e30)
    p = jnp.exp(s - jnp.max(s, axis=-1, keepdims=True))
    inv = 1.0 / jnp.sum(p, axis=-1, keepdims=True)
    pb = p.astype(BF16)
    for j in range(bb):
        rs = slice(j * rows, (j + 1) * rows)
        o_ref[j] = _dot(pb[rs], v_ref[j].astype(BF16)) * inv[rs]


def _sample_post_kernel(x_ref, mbr_ref, att_ref, gatt_ref, wout_ref, post_ref, o_ref, m_ref):
    nb = DEC_BATCH
    for h in range(N_XHEADS):
        cs = slice(h * XHEAD_DIM, (h + 1) * XHEAD_DIM)
        for t in range(DEC_SEQ):
            rs = slice(t * nb, (t + 1) * nb)
            att = att_ref[pl.ds(h * DEC_SEQ + t, nb, stride=Q_ROWS), :]
            m_ref[rs, cs] = (att * gatt_ref[rs, cs]).astype(BF16)
    out = (_dot(mbr_ref[...].astype(BF16), wout_ref[0:BRANCH, :])
           + _dot(m_ref[...], wout_ref[BRANCH:, :]))
    y = _rms(out, post_ref[...])
    for t in range(DEC_SEQ):
        rs = slice(t * nb, (t + 1) * nb)
        x = x_ref[:, t, :] if len(x_ref.shape) == 3 else x_ref[rs, :]
        if len(o_ref.shape) == 3:
            o_ref[:, t, :] = x + y[rs]
        else:
            o_ref[rs, :] = x + y[rs]


def _sample_post(x, mbr, att, gatt, wout, post, layer, out_shape):
    ins = [x, mbr, att, gatt, wout, post]
    n_out = len(out_shape)
    return pl.pallas_call(
        _sample_post_kernel,
        grid=(1,),
        in_specs=[_const_spec(a.shape) for a in ins[:5]] + [_layer_spec(post.shape, layer)],
        out_specs=pl.BlockSpec(out_shape, lambda i: (0,) * n_out),
        out_shape=jax.ShapeDtypeStruct(out_shape, F32),
        scratch_shapes=[pltpu.VMEM((mbr.shape[0], XATT), BF16)],
        compiler_params=pltpu.CompilerParams(
            dimension_semantics=("arbitrary",), vmem_limit_bytes=VMEM_LIMIT),
        name="sample_post",
    )(*ins)


def kernel(x_prompt, x_sample, cache_mem_k, cache_mem_v, state_ssm_re, state_ssm_im, mem_prompt,
           w_in_a, ln_v_g, ln_v_b, w_spatial, b_spatial,
           w_in_b, ssm_lambda_re, ssm_lambda_im, ssm_log_dt, ssm_b_re, ssm_b_im, ssm_c_re, ssm_c_im,
           ssm_d, w_glu, b_glu,
           mem_norm_g, w_mem_k, w_mem_v, w_out, pre_norm_g, post_norm_g):
    depth = w_out.shape[0]
    wk = w_mem_k.astype(BF16)
    wv = w_mem_v.astype(BF16)
    pre = pre_norm_g.reshape(depth, 1, D_MODEL)
    post = post_norm_g.reshape(depth, 1, D_MODEL)
    lng = ln_v_g[0].reshape(1, BRANCH)
    lnb = ln_v_b[0].reshape(1, BRANCH)
    bglu = b_glu[0].reshape(1, BRANCH)
    dvec = ssm_d[0].reshape(1, BRANCH)
    bias = jnp.repeat(b_spatial[0].T, A_GDIM, axis=1)
    ws4 = w_spatial[0][:, :DEC_SEQ, :DEC_SEQ]
    wrow = jnp.repeat(ws4.transpose(1, 2, 0).reshape(DEC_SEQ * DEC_SEQ, A_GROUPS), A_GDIM, axis=1)
    brow = bias[:DEC_SEQ]

    *s5, win_a = _s5_prep(ssm_lambda_re[0], ssm_lambda_im[0], ssm_log_dt[0],
                          ssm_b_re[0], ssm_b_im[0], ssm_c_re[0], ssm_c_im[0], casts=[(w_in_a, 0)])

    mem = mem_prompt.reshape(BATCH * N_MEM, D_MODEL)
    mk, mv, mkt, mvb, wglu, wout0 = _mem_kv(mem, mem_norm_g.reshape(depth, 1, D_MODEL), wk, wv,
                                            casts=[(w_glu, 0), (w_out, 0)])

    ck = cache_mem_k.reshape(depth, DEC_BATCH, N_MEM * N_XHEADS, XHEAD_DIM)
    cv = cache_mem_v.reshape(depth, DEC_BATCH, N_MEM * N_XHEADS, XHEAD_DIM)
    q_shape = (DEC_BATCH, Q_ROWS, XHEAD_DIM)
    half = BATCH // 2
    layer_a = functools.partial(_layer_a_prompt, x_prompt, kt=mkt, vb=mvb, win=win_a, lng=lng, lnb=lnb,
                                ws=w_spatial[0], bias=bias, wout=wout0, pre=pre, post=post, layer=0,
                                cache_k=ck, cache_v=cv)
    rows_tb = (DEC_SEQ * DEC_BATCH, D_MODEL)
    v_rows, mbr, q, gatt = _sample_pre_a(x_sample, win_a, lng, lnb, wrow, brow, pre, 0)
    yp_a, att, win_b = layer_a(b0=0, nb=half, sq=q.reshape(q_shape), sample_layer=0, casts=[(w_in_b, 0)])
    xs = _sample_post(x_sample, mbr, att.reshape(q.shape), gatt, wout0, post, 0, rows_tb)
    sre = state_ssm_re[0].reshape(DEC_BATCH, SSM_GROUPS * SSM_STATE)
    sim = state_ssm_im[0].reshape(DEC_BATCH, SSM_GROUPS * SSM_STATE)
    mbr, q, gatt, hs_re, hs_im = _sample_pre_b(xs, win_b, sre, sim, s5, dvec, wglu, bglu, pre, 1)
    yp_b, att, wout1 = layer_a(b0=half, nb=half, sq=q.reshape(q_shape), sample_layer=1, casts=[(w_out, 1)])
    ys = _sample_post(xs, mbr, att.reshape(q.shape), gatt, wout1, post, 1, x_sample.shape)
    yp, hp_re, hp_im = _layer_b_prompt(yp_a, yp_b, mkt, mvb, win_b, s5, dvec, wglu, bglu, wout1, pre, post, 1)

    kv_shape = (depth, BATCH, N_MEM, N_XHEADS, XHEAD_DIM)
    st_p = (1, BATCH, SSM_GROUPS, SSM_STATE)
    st_s = (1, DEC_BATCH, SSM_GROUPS, SSM_STATE)
    return (yp, ys, mk.reshape(kv_shape), mv.reshape(kv_shape),
            hp_re.reshape(st_p), hp_im.reshape(st_p), hs_re.reshape(st_s), hs_im.reshape(st_s),
            v_rows[None])
```

```python
import functools
import math

import jax
import jax.numpy as jnp
from jax import lax
from jax.experimental import pallas as pl
from jax.experimental.pallas import tpu as pltpu

D_MODEL = 1024
BATCH = 8
SEQ = 2048
DEC_BATCH = 128
DEC_SEQ = 4
BRANCH = 1536
N_XHEADS = 4
XHEAD_DIM = 128
XATT = N_XHEADS * XHEAD_DIM
MIX_WIDTH = BRANCH + XATT
N_MEM = 256
CHUNK = 128
A_GROUPS = 8
A_GDIM = BRANCH // A_GROUPS
SSM_GCH = 16
SSM_GROUPS = 96
SSM_STATE = 64
EPS = 1e-6

LANES = 128
S5_TILE_GROUPS = LANES // SSM_GCH
S5_TILES = SSM_GROUPS // S5_TILE_GROUPS
S5_HALF = S5_TILE_GROUPS * SSM_STATE
VMEM_LIMIT = 60 * 1024 * 1024
A_STREAM_ROWS = 256

F32 = jnp.float32
BF16 = jnp.bfloat16


def _dot(a, b):
    return jnp.dot(a, b, preferred_element_type=F32)


def _rms(x, g):
    return x * lax.rsqrt(jnp.mean(x * x, axis=-1, keepdims=True) + EPS) * g


def _layer_norm(x, g, b):
    mu = jnp.mean(x, axis=-1, keepdims=True)
    xc = x - mu
    var = jnp.mean(xc * xc, axis=-1, keepdims=True)
    return xc * lax.rsqrt(var + EPS) * g + b


def _gelu(x):
    c = math.sqrt(2.0 / math.pi)
    hx = 0.5 * x
    return hx + hx * jnp.tanh(x * (c + (c * 0.044715) * (x * x)))


def _sigmoid(x):
    return 0.5 + 0.5 * jnp.tanh(0.5 * x)


def _silu(x):
    hx = 0.5 * x
    return hx + hx * jnp.tanh(hx)


def _dot_nt(a, b):
    return lax.dot_general(a, b, (((1,), (1,)), ((), ())), preferred_element_type=F32)


def _attend(qs, kts, vs):
    r = qs[0].shape[0]
    s = jnp.concatenate([_dot(q.astype(BF16), kt) for q, kt in zip(qs, kts)], axis=0)
    s = s * (XHEAD_DIM ** -0.5)
    p = jnp.exp(s - jnp.max(s, axis=-1, keepdims=True))
    inv = 1.0 / jnp.sum(p, axis=-1, keepdims=True)
    pb = p.astype(BF16)
    return [_dot(pb[i * r:(i + 1) * r], v) * inv[i * r:(i + 1) * r] for i, v in enumerate(vs)]


def _head(a, h):
    return a[:, h * XHEAD_DIM:(h + 1) * XHEAD_DIM]


def _const_spec(shape):
    nd = len(shape)
    return pl.BlockSpec(shape, lambda *_: (0,) * nd, pipeline_mode=pl.Buffered(1))


def _layer_spec(shape, layer):
    nd = len(shape)
    return pl.BlockSpec((None,) + tuple(shape[1:]), lambda *_: (layer,) + (0,) * (nd - 1),
                        pipeline_mode=pl.Buffered(1))


def _lam_bar(lr, li, ldt):
    dt = jnp.exp(ldt)
    e = jnp.exp(lr * dt)
    return e * jnp.cos(li * dt), e * jnp.sin(li * dt)


def _cast_rider_specs(casts, steps):
    in_specs = [pl.BlockSpec((None, w.shape[1] // steps, w.shape[2]),
                             functools.partial(lambda i, l: (l, i, 0), l=l)) for w, l in casts]
    out_specs = [pl.BlockSpec((w.shape[1] // steps, w.shape[2]), lambda i: (i, 0)) for w, _ in casts]
    out_shape = [jax.ShapeDtypeStruct(w.shape[1:], BF16) for w, _ in casts]
    return in_specs, out_specs, out_shape


def _s5_prep_kernel(*refs, n_casts):
    n_in, n_out = 7, 5
    ins, cast_in = refs[:n_in], refs[n_in:n_in + n_casts]
    outs, cast_out = refs[n_in + n_casts:n_in + n_casts + n_out], refs[n_in + n_casts + n_out:]
    for src, dst in zip(cast_in, cast_out):
        dst[...] = src[...].astype(BF16)
    for j in range(ins[0].shape[0]):
        _s5_prep_tile(*[r.at[pl.ds(j, 1)] for r in (*ins, *outs)])


def _s5_prep_tile(lr_ref, li_ref, ldt_ref, br_ref, bi_ref, cr_ref, ci_ref,
                  a2r_ref, a2i_ref, b2_ref, c2_ref, k2_ref):
    lr = lr_ref[0]
    li = li_ref[0]
    ar, ai = _lam_bar(lr, li, ldt_ref[0])
    a2r = ar * ar - ai * ai
    a2i = 2.0 * ar * ai
    a2r_ref[0] = a2r
    a2i_ref[0] = a2i
    nr = ar - 1.0
    den = lr * lr + li * li
    kr = (nr * lr + ai * li) / den
    ki = (ai * lr - nr * li) / den
    shape = (LANES, S5_HALF)
    own = (lax.broadcasted_iota(jnp.int32, shape, 0) // SSM_GCH
           == lax.broadcasted_iota(jnp.int32, shape, 1) // SSM_STATE)

    def own_blocks(ref):
        return jnp.where(own, jnp.concatenate([ref[0]] * S5_TILE_GROUPS, axis=0), 0.0)

    br, bi, cr, ci = own_blocks(br_ref), own_blocks(bi_ref), own_blocks(cr_ref), own_blocks(ci_ref)
    bbr = kr * br - ki * bi
    bbi = kr * bi + ki * br
    b_tok1 = jnp.concatenate([bbr, bbi], axis=1).astype(BF16)
    b_tok0 = jnp.concatenate([ar * bbr - ai * bbi, ar * bbi + ai * bbr], axis=1).astype(BF16)
    b2_ref[0, 0:LANES, :] = b_tok0
    b2_ref[0, LANES:, :] = b_tok1
    c_tok0 = jnp.concatenate([cr * ar - ci * ai, -(cr * ai + ci * ar)], axis=1)
    c_tok1 = jnp.concatenate([cr * a2r - ci * a2i, -(cr * a2i + ci * a2r)], axis=1)
    c2_ref[0, :, 0:LANES] = c_tok0.T.astype(BF16)
    c2_ref[0, :, LANES:] = c_tok1.T.astype(BF16)
    c_plain = jnp.concatenate([cr, -ci], axis=1).astype(BF16)
    k0 = _dot_nt(b_tok1, c_plain).astype(BF16)
    k1 = _dot_nt(b_tok0, c_plain).astype(BF16)
    k2_ref[0, 0:LANES, 0:LANES] = k0
    k2_ref[0, 0:LANES, LANES:] = k1
    k2_ref[0, LANES:, 0:LANES] = jnp.zeros_like(k0)
    k2_ref[0, LANES:, LANES:] = k0


def _s5_prep(lam_re, lam_im, log_dt, b_re, b_im, c_re, c_im, casts=()):
    nt, tg = S5_TILES, S5_TILE_GROUPS
    lr = lam_re.reshape(nt, 1, S5_HALF)
    li = lam_im.reshape(nt, 1, S5_HALF)
    ldt = jnp.broadcast_to(log_dt[:, None], (SSM_GROUPS, SSM_STATE)).reshape(nt, 1, S5_HALF)

    def b_cp(b):
        return b.reshape(nt, tg, SSM_STATE, SSM_GCH).transpose(0, 3, 1, 2).reshape(nt, SSM_GCH, S5_HALF)

    def c_cp(c):
        return c.reshape(nt, tg, SSM_GCH, SSM_STATE).transpose(0, 2, 1, 3).reshape(nt, SSM_GCH, S5_HALF)

    per_step = 3

    def spec(*shape):
        return pl.BlockSpec((per_step,) + shape, lambda j: (j, 0, 0))

    vec, chan = spec(1, S5_HALF), spec(SSM_GCH, S5_HALF)
    out_shapes = [(1, S5_HALF), (1, S5_HALF), (2 * LANES, 2 * S5_HALF), (2 * S5_HALF, 2 * LANES),
                  (2 * LANES, 2 * LANES)]
    out_dtypes = [F32, F32, BF16, BF16, BF16]
    steps = nt // per_step
    cast_in, cast_out, cast_shape = _cast_rider_specs(casts, steps)
    return pl.pallas_call(
        functools.partial(_s5_prep_kernel, n_casts=len(casts)),
        grid=(steps,),
        in_specs=[vec, vec, vec, chan, chan, chan, chan] + cast_in,
        out_specs=[spec(*s) for s in out_shapes] + cast_out,
        out_shape=[jax.ShapeDtypeStruct((nt,) + s, d) for s, d in zip(out_shapes, out_dtypes)] + cast_shape,
        compiler_params=pltpu.CompilerParams(
            dimension_semantics=("arbitrary",), vmem_limit_bytes=VMEM_LIMIT),
        name="s5_prep",
    )(lr, li, ldt, b_cp(b_re), b_cp(b_im), c_cp(c_re), c_cp(c_im), *[w for w, _ in casts])


def _mem_kv_kernel(mem_ref, g_ref, wk_ref, wv_ref, *rest, n_casts):
    cast_in, (k_ref, v_ref, kt_ref, vb_ref), cast_out = rest[:n_casts], rest[n_casts:n_casts + 4], rest[n_casts + 4:]
    for src, dst in zip(cast_in, cast_out):
        dst[...] = src[...].astype(BF16)
    mem = mem_ref[...]
    scaled = mem * lax.rsqrt(jnp.mean(mem * mem, axis=-1, keepdims=True) + EPS)
    for i in range(g_ref.shape[0]):
        m = (scaled * g_ref[i:i + 1, :]).astype(BF16)
        k = _dot(m, wk_ref[i])
        v = _dot(m, wv_ref[i])
        vb_ref[i] = v.astype(BF16)
        for b in range(k_ref.shape[1]):
            kb = k[b * N_MEM:(b + 1) * N_MEM]
            vb = v[b * N_MEM:(b + 1) * N_MEM]
            kt_ref[i, b] = kb.T.astype(BF16)
            for h in range(N_XHEADS):
                rows = pl.ds(h, N_MEM, stride=N_XHEADS)
                k_ref[i, b, rows, :] = _head(kb, h)
                v_ref[i, b, rows, :] = _head(vb, h)


def _mem_kv(mem, g, wk, wv, casts=()):
    rows = mem.shape[0]
    depth = g.shape[0]
    nb = 1
    tm = nb * N_MEM
    steps = rows // tm
    cast_in, cast_out, cast_shape = _cast_rider_specs(casts, steps)
    return pl.pallas_call(
        functools.partial(_mem_kv_kernel, n_casts=len(casts)),
        grid=(steps,),
        in_specs=[pl.BlockSpec((tm, D_MODEL), lambda r: (r, 0)),
                  _const_spec(g.shape), _const_spec(wk.shape), _const_spec(wv.shape)] + cast_in,
        out_specs=[pl.BlockSpec((depth, nb, N_MEM * N_XHEADS, XHEAD_DIM), lambda r: (0, r, 0, 0))] * 2
        + [pl.BlockSpec((depth, nb, XATT, N_MEM), lambda r: (0, r, 0, 0)),
           pl.BlockSpec((depth, tm, XATT), lambda r: (0, r, 0))] + cast_out,
        out_shape=[jax.ShapeDtypeStruct((depth, rows // N_MEM, N_MEM * N_XHEADS, XHEAD_DIM), F32)] * 2
        + [jax.ShapeDtypeStruct((depth, rows // N_MEM, XATT, N_MEM), BF16),
           jax.ShapeDtypeStruct((depth, rows, XATT), BF16)] + cast_shape,
        compiler_params=pltpu.CompilerParams(
            dimension_semantics=("arbitrary",), vmem_limit_bytes=VMEM_LIMIT),
        name="mem_kv",
    )(mem, g, wk, wv, *[w for w, _ in casts])


def _masked_spatial(ws_ref):
    row = lax.broadcasted_iota(jnp.int32, (CHUNK, CHUNK), 0)
    col = lax.broadcasted_iota(jnp.int32, (CHUNK, CHUNK), 1)
    return [jnp.where(row >= col, ws_ref[g], 0.0).astype(BF16) for g in range(A_GROUPS)]


def _gate_chunk(wm, vc, pair):
    base = pair * 2 * A_GDIM
    even = _dot(wm[2 * pair], vc[:, base:base + 2 * LANES])
    odd = _dot(wm[2 * pair + 1], vc[:, base + LANES:base + 3 * LANES])
    lane = lax.broadcasted_iota(jnp.int32, (CHUNK, LANES), 1)
    mid = jnp.where(lane < A_GDIM - LANES, even[:, LANES:], odd[:, :LANES])
    return jnp.concatenate([even[:, :LANES], mid, odd[:, LANES:]], axis=1)


def _s5_tile(u, hr, hi, a2r, a2i, b2, c2, k2, bu_ref, hs_ref, steps, nb):
    pairs = steps // 2
    u3 = u.reshape(pairs, 2 * nb, LANES)
    u2 = jnp.concatenate([u3[:, 0:nb, :].reshape(pairs * nb, LANES),
                          u3[:, nb:, :].reshape(pairs * nb, LANES)], axis=1).astype(BF16)
    bu_ref[...] = _dot(u2, b2)
    a2r = jnp.broadcast_to(a2r, (nb, S5_HALF))
    a2i = jnp.broadcast_to(a2i, (nb, S5_HALF))
    for k in range(pairs):
        rows = slice(k * nb, (k + 1) * nb)
        hs_ref[rows, 0:S5_HALF] = hr
        hs_ref[rows, S5_HALF:] = hi
        hr, hi = (a2r * hr - a2i * hi + bu_ref[rows, 0:S5_HALF],
                  a2r * hi + a2i * hr + bu_ref[rows, S5_HALF:])
    y2 = _dot(hs_ref[...].astype(BF16), c2) + _dot(u2, k2)
    return y2, hr, hi


def _layer_a_prompt_kernel(*refs, n_casts, layer):
    (x_ref, kt_ref, v_ref, win_ref, lng_ref, lnb_ref, ws_ref, bias_ref,
     wout_ref, pre_ref, post_ref, sq_ref, sk_ref, sv_ref) = refs[:14]
    cast_in = refs[14:14 + n_casts]
    o_ref, satt_ref = refs[14 + n_casts:16 + n_casts]
    cast_out = refs[16 + n_casts:16 + 2 * n_casts]
    m_ref, g_ref = refs[16 + 2 * n_casts:]
    for src, dst in zip(cast_in, cast_out):
        dst[...] = src[...].astype(BF16)
    wm = _masked_spatial(ws_ref)
    heads = range(N_XHEADS)
    n_streams = x_ref.shape[0] // A_STREAM_ROWS

    def project(s):
        x = x_ref[pl.ds(s * A_STREAM_ROWS, A_STREAM_ROWS), :]
        h = _rms(x, pre_ref[layer:layer + 1, :]).astype(BF16)
        v = _layer_norm(_gelu(_dot(h, win_ref[:, BRANCH:2 * BRANCH])), lng_ref[...], lnb_ref[...])
        u = _gelu(_dot(h, win_ref[:, 0:BRANCH]))
        gate = _silu(_dot(h, win_ref[:, 2 * BRANCH + XATT:]))
        g_ref[pl.ds(s * A_STREAM_ROWS, A_STREAM_ROWS), 0:BRANCH] = u * gate[:, 0:BRANCH]
        g_ref[pl.ds(s * A_STREAM_ROWS, A_STREAM_ROWS), BRANCH:] = gate[:, BRANCH:]
        q = _dot(h, win_ref[:, 2 * BRANCH:2 * BRANCH + XATT])
        return x, v.astype(BF16), q

    def mix_and_emit(s, x, vb, q):
        m = m_ref.at[pl.ds(s * A_STREAM_ROWS, A_STREAM_ROWS)]
        g = g_ref.at[pl.ds(s * A_STREAM_ROWS, A_STREAM_ROWS)]
        for c in range(A_STREAM_ROWS // CHUNK):
            rs = slice(c * CHUNK, (c + 1) * CHUNK)
            vc = vb[rs]
            for pair in range(A_GROUPS // 2):
                cs = slice(pair * 2 * A_GDIM, (pair + 1) * 2 * A_GDIM)
                mixed = _gate_chunk(wm, vc, pair) + bias_ref[:, cs]
                m[rs, cs] = (g[rs, cs] * mixed).astype(BF16)
        kv = v_ref[...]
        att = _attend([_head(q, h) for h in heads],
                      [kt_ref[h * XHEAD_DIM:(h + 1) * XHEAD_DIM, :] for h in heads],
                      [_head(kv, h) for h in heads])
        for h in heads:
            cs = slice(BRANCH + h * XHEAD_DIM, BRANCH + (h + 1) * XHEAD_DIM)
            m[:, cs] = (att[h] * g[:, cs]).astype(BF16)
        out = _dot(m[...], wout_ref[...])
        o_ref[pl.ds(s * A_STREAM_ROWS, A_STREAM_ROWS), :] = x + _rms(out, post_ref[layer:layer + 1, :])

    pending = project(0)
    for s in range(n_streams):
        following = project(s + 1) if s + 1 < n_streams else None
        mix_and_emit(s, *pending)
        pending = following
    _sample_attn_kernel(sq_ref, sk_ref, sv_ref, satt_ref)


def _layer_a_prompt(x, b0, nb, kt, vb, win, lng, lnb, ws, bias, wout, pre, post, layer,
                    sq, cache_k, cache_v, sample_layer, casts=()):
    tm = 512
    seq = x.shape[1]
    nt = seq // tm
    steps = nb * nt
    bb = sq.shape[0] // steps
    cast_specs = [pl.BlockSpec((None, w.shape[1] // steps, w.shape[2]),
                               functools.partial(lambda b, t, l: (l, b * nt + t, 0), l=l))
                  for w, l in casts]
    cast_out_specs = [pl.BlockSpec((w.shape[1] // steps, w.shape[2]), lambda b, t: (b * nt + t, 0))
                      for w, _ in casts]
    xin = pl.BlockSpec((None, tm, D_MODEL), lambda b, t: (b + b0, t, 0))
    xout = pl.BlockSpec((None, tm, D_MODEL), lambda b, t: (b, t, 0))
    kts = pl.BlockSpec((None, None, XATT, N_MEM), lambda b, t: (layer, b + b0, 0, 0))
    vs = pl.BlockSpec((None, N_MEM, XATT), lambda b, t: (layer, b + b0, 0))
    sqs = pl.BlockSpec((bb,) + sq.shape[1:], lambda b, t: (b * nt + t, 0, 0))
    scs = pl.BlockSpec((None, bb) + cache_k.shape[2:], lambda b, t: (sample_layer, b * nt + t, 0, 0))
    return pl.pallas_call(
        functools.partial(_layer_a_prompt_kernel, n_casts=len(casts), layer=layer),
        grid=(nb, nt),
        in_specs=[xin, kts, vs, _const_spec(win.shape), _const_spec(lng.shape), _const_spec(lnb.shape),
                  _const_spec(ws.shape), _const_spec(bias.shape), _const_spec(wout.shape),
                  _const_spec(pre.shape), _const_spec(post.shape), sqs, scs, scs] + cast_specs,
        out_specs=[xout, sqs] + cast_out_specs,
        out_shape=[jax.ShapeDtypeStruct((nb, seq, D_MODEL), F32), jax.ShapeDtypeStruct(sq.shape, F32)]
        + [jax.ShapeDtypeStruct(w.shape[1:], BF16) for w, _ in casts],
        scratch_shapes=[pltpu.VMEM((tm, MIX_WIDTH), BF16), pltpu.VMEM((tm, MIX_WIDTH), F32)],
        compiler_params=pltpu.CompilerParams(
            dimension_semantics=("arbitrary", "arbitrary"), vmem_limit_bytes=VMEM_LIMIT),
        name="layer_a_prompt",
    )(x, kt, vb, win, lng, lnb, ws, bias, wout, pre, post, sq, cache_k, cache_v, *[w for w, _ in casts])


def _layer_b_prompt_kernel(xa_ref, xb_ref, kt_ref, v_ref, win_ref, a2r_ref, a2i_ref, b2_ref, c2_ref, k2_ref,
                           d_ref, wglu_ref, bglu_ref, wout_ref, pre_ref, post_ref,
                           o_ref, hre_ref, him_ref,
                           ut_ref, bu_ref, hs_ref, yt_ref, y_ref, m_ref, *, layer):
    nb, tc, _ = o_ref.shape
    tm = nb * tc
    half = tc // 2

    @pl.when(pl.program_id(0) == 0)
    def _():
        hre_ref[...] = jnp.zeros_like(hre_ref)
        him_ref[...] = jnp.zeros_like(him_ref)

    x = jnp.concatenate([xa_ref[...], xb_ref[...]], axis=0).reshape(tm, D_MODEL)
    h = _rms(x, pre_ref[layer:layer + 1, :]).astype(BF16)
    u = _dot(h, win_ref[:, 0:BRANCH])
    for j in range(S5_TILES):
        for b in range(nb):
            ut_ref[j, pl.ds(b, tc, stride=nb), :] = u[b * tc:(b + 1) * tc, j * LANES:(j + 1) * LANES]
    for j in range(S5_TILES):
        ss = slice(j * S5_HALF, (j + 1) * S5_HALF)
        par = j % 2
        y2, hr, hi = _s5_tile(ut_ref[j], hre_ref[:, ss], him_ref[:, ss], a2r_ref[j], a2i_ref[j],
                              b2_ref[j], c2_ref[j], k2_ref[j], bu_ref.at[par], hs_ref.at[par], tc, nb)
        hre_ref[:, ss] = hr
        him_ref[:, ss] = hi
        for tp in range(2):
            yt_ref[par, tp] = y2[:, tp * LANES:(tp + 1) * LANES]
            for b in range(nb):
                y_ref[j, pl.ds(b * tc + tp, half, stride=2), :] = yt_ref[par, tp, pl.ds(b, half, stride=nb), :]
    y = _gelu(jnp.concatenate([y_ref[j] for j in range(S5_TILES)], axis=1) + d_ref[...] * u)
    branch = y * _sigmoid(_dot(y.astype(BF16), wglu_ref[...]) + bglu_ref[...])
    gate = _silu(_dot(h, win_ref[:, BRANCH + XATT:]))
    m_ref[:, 0:BRANCH] = (branch * gate[:, 0:BRANCH]).astype(BF16)
    q = _dot(h, win_ref[:, BRANCH:BRANCH + XATT])
    pieces = [(b, h) for b in range(nb) for h in range(N_XHEADS)]
    att = _attend([_head(q[b * tc:(b + 1) * tc], h) for b, h in pieces],
                  [kt_ref[b, h * XHEAD_DIM:(h + 1) * XHEAD_DIM, :] for b, h in pieces],
                  [_head(v_ref[b * N_MEM:(b + 1) * N_MEM, :], h) for b, h in pieces])
    for (b, h), a in zip(pieces, att):
        rs = slice(b * tc, (b + 1) * tc)
        cs = slice(BRANCH + h * XHEAD_DIM, BRANCH + (h + 1) * XHEAD_DIM)
        m_ref[rs, cs] = (a * gate[rs, cs]).astype(BF16)
    out = _dot(m_ref[...], wout_ref[...])
    o_ref[...] = (x + _rms(out, post_ref[layer:layer + 1, :])).reshape(nb, tc, D_MODEL)


def _layer_b_prompt(xa, xb, kb, vb, win, s5, dvec, wglu, bglu, wout, pre, post, layer):
    tc = 64
    nb, seq = xa.shape[0] + xb.shape[0], xa.shape[1]
    tm = nb * tc
    xin = pl.BlockSpec((xa.shape[0], tc, D_MODEL), lambda t: (0, t, 0))
    xs = pl.BlockSpec((nb, tc, D_MODEL), lambda t: (0, t, 0))
    hs = pl.BlockSpec((nb, SSM_GROUPS * SSM_STATE), lambda t: (0, 0))
    ins = [kb, vb, win, *s5, dvec, wglu, bglu, wout, pre, post]
    stacked = [True, True] + [False] * (len(s5) + 7)
    return pl.pallas_call(
        functools.partial(_layer_b_prompt_kernel, layer=layer),
        grid=(seq // tc,),
        in_specs=[xin, xin] + [_layer_spec(a.shape, layer) if st else _const_spec(a.shape)
                               for a, st in zip(ins, stacked)],
        out_specs=[xs, hs, hs],
        out_shape=[jax.ShapeDtypeStruct((nb, seq, D_MODEL), F32),
                   jax.ShapeDtypeStruct((nb, SSM_GROUPS * SSM_STATE), F32),
                   jax.ShapeDtypeStruct((nb, SSM_GROUPS * SSM_STATE), F32)],
        scratch_shapes=[pltpu.VMEM((S5_TILES, tm, LANES), F32),
                        pltpu.VMEM((2, tm // 2, 2 * S5_HALF), F32),
                        pltpu.VMEM((2, tm // 2, 2 * S5_HALF), F32),
                        pltpu.VMEM((2, 2, tm // 2, LANES), F32),
                        pltpu.VMEM((S5_TILES, tm, LANES), F32),
                        pltpu.VMEM((tm, MIX_WIDTH), BF16)],
        compiler_params=pltpu.CompilerParams(
            dimension_semantics=("arbitrary",), vmem_limit_bytes=VMEM_LIMIT),
        name="layer_b_prompt",
    )(xa, xb, *ins)


Q_ROWS = N_XHEADS * DEC_SEQ


def _store_q_by_sequence(q_ref, q):
    for h in range(N_XHEADS):
        for t in range(DEC_SEQ):
            q_ref[pl.ds(h * DEC_SEQ + t, DEC_BATCH, stride=Q_ROWS), :] = _head(q[t * DEC_BATCH:(t + 1) * DEC_BATCH], h)


def _sample_pre_a_kernel(x_ref, win_ref, lng_ref, lnb_ref, wrow_ref, brow_ref, pre_ref,
                         v_ref, mbr_ref, q_ref, gatt_ref, *, layer):
    nb = DEC_BATCH
    x = jnp.concatenate([x_ref[:, t, :] for t in range(DEC_SEQ)], axis=0)
    h = _rms(x, pre_ref[layer:layer + 1, :]).astype(BF16)
    gate = _silu(_dot(h, win_ref[:, 2 * BRANCH + XATT:]))
    u = _gelu(_dot(h, win_ref[:, 0:BRANCH]))
    v = _layer_norm(_gelu(_dot(h, win_ref[:, BRANCH:2 * BRANCH])), lng_ref[...], lnb_ref[...])
    for t in range(DEC_SEQ):
        v_ref[:, t, :] = v[t * nb:(t + 1) * nb]
    for t in range(DEC_SEQ):
        rs = slice(t * nb, (t + 1) * nb)
        mixed = brow_ref[t:t + 1, :]
        for s in range(t + 1):
            mixed = mixed + wrow_ref[t * DEC_SEQ + s:t * DEC_SEQ + s + 1, :] * v[s * nb:(s + 1) * nb]
        mbr_ref[rs, :] = u[rs] * mixed * gate[rs, 0:BRANCH]
    _store_q_by_sequence(q_ref, _dot(h, win_ref[:, 2 * BRANCH:2 * BRANCH + XATT]))
    gatt_ref[...] = gate[:, BRANCH:]


def _sample_pre_a(x, win, lng, lnb, wrow, brow, pre, layer):
    rows = DEC_SEQ * DEC_BATCH
    ins = [x, win, lng, lnb, wrow, brow, pre]
    shapes = [(DEC_BATCH, DEC_SEQ, BRANCH), (rows, BRANCH), (DEC_BATCH * Q_ROWS, XHEAD_DIM), (rows, XATT)]
    return pl.pallas_call(
        functools.partial(_sample_pre_a_kernel, layer=layer),
        grid=(1,),
        in_specs=[_const_spec(a.shape) for a in ins],
        out_specs=[pl.BlockSpec(s, functools.partial(lambda i, n: (0,) * n, n=len(s))) for s in shapes],
        out_shape=[jax.ShapeDtypeStruct(s, F32) for s in shapes],
        compiler_params=pltpu.CompilerParams(
            dimension_semantics=("arbitrary",), vmem_limit_bytes=VMEM_LIMIT),
        name="sample_pre_a",
    )(*ins)


def _sample_pre_b_kernel(x_ref, win_ref, sre_ref, sim_ref, a2r_ref, a2i_ref, b2_hbm, c2_hbm, k2_hbm, d_ref,
                         wglu_hbm, bglu_ref, pre_ref,
                         mbr_ref, q_ref, gatt_ref, hre_ref, him_ref,
                         bu_ref, hs_ref, y_ref, b2_ref, c2_ref, k2_ref, wglu_ref, sem, *, layer):
    nb = DEC_BATCH
    late = [pltpu.make_async_copy(src, dst, sem.at[i]) for i, (src, dst) in enumerate(
        [(b2_hbm, b2_ref), (c2_hbm, c2_ref), (k2_hbm, k2_ref), (wglu_hbm, wglu_ref)])]
    for copy in late:
        copy.start()
    h = _rms(x_ref[...], pre_ref[layer:layer + 1, :]).astype(BF16)
    u = _dot(h, win_ref[:, 0:BRANCH])
    for copy in late[:3]:
        copy.wait()
    for j in range(S5_TILES):
        ss = slice(j * S5_HALF, (j + 1) * S5_HALF)
        ls = slice(j * LANES, (j + 1) * LANES)
        y2, hr, hi = _s5_tile(u[:, ls], sre_ref[:, ss], sim_ref[:, ss], a2r_ref[j], a2i_ref[j],
                              b2_ref[j], c2_ref[j], k2_ref[j], bu_ref, hs_ref, DEC_SEQ, nb)
        hre_ref[:, ss] = hr
        him_ref[:, ss] = hi
        for pair in range(DEC_SEQ // 2):
            for tp in range(2):
                t = 2 * pair + tp
                y_ref[t * nb:(t + 1) * nb, ls] = y2[pair * nb:(pair + 1) * nb, tp * LANES:(tp + 1) * LANES]
    y = _gelu(y_ref[...] + d_ref[...] * u)
    late[3].wait()
    branch = y * _sigmoid(_dot(y.astype(BF16), wglu_ref[...]) + bglu_ref[...])
    gate = _silu(_dot(h, win_ref[:, BRANCH + XATT:]))
    mbr_ref[...] = branch * gate[:, 0:BRANCH]
    _store_q_by_sequence(q_ref, _dot(h, win_ref[:, BRANCH:BRANCH + XATT]))
    gatt_ref[...] = gate[:, BRANCH:]


def _sample_pre_b(x, win, sre, sim, s5, dvec, wglu, bglu, pre, layer):
    rows = x.shape[0]
    a2r, a2i, b2, c2, k2 = s5
    ins = [x, win, sre, sim, a2r, a2i, b2, c2, k2, dvec, wglu, bglu]
    by_hand = [b2, c2, k2, wglu]
    shapes = [(rows, BRANCH), (DEC_BATCH * Q_ROWS, XHEAD_DIM), (rows, XATT), sre.shape, sim.shape]
    return pl.pallas_call(
        functools.partial(_sample_pre_b_kernel, layer=layer),
        grid=(1,),
        in_specs=[pl.BlockSpec(memory_space=pl.ANY) if any(a is w for w in by_hand) else _const_spec(a.shape)
                  for a in ins] + [_const_spec(pre.shape)],
        out_specs=[pl.BlockSpec(s, lambda i: (0, 0)) for s in shapes],
        out_shape=[jax.ShapeDtypeStruct(s, F32) for s in shapes],
        scratch_shapes=[pltpu.VMEM((rows // 2, 2 * S5_HALF), F32),
                        pltpu.VMEM((rows // 2, 2 * S5_HALF), F32),
                        pltpu.VMEM((rows, BRANCH), F32)]
        + [pltpu.VMEM(w.shape, w.dtype) for w in by_hand] + [pltpu.SemaphoreType.DMA((len(by_hand),))],
        compiler_params=pltpu.CompilerParams(
            dimension_semantics=("arbitrary",), vmem_limit_bytes=VMEM_LIMIT),
        name="sample_pre_b",
    )(*ins, pre)


def _sample_attn_kernel(q_ref, k_ref, v_ref, o_ref):
    bb, rows, _ = q_ref.shape
    cols = k_ref.shape[1]
    row_h = (lax.broadcasted_iota(jnp.int32, (bb * rows, cols), 0) // DEC_SEQ) % N_XHEADS
    col_h = lax.broadcasted_iota(jnp.int32, (bb * rows, cols), 1) % N_XHEADS
    s = jnp.concatenate([_dot_nt(q_ref[j].astype(BF16), k_ref[j].astype(BF16)) for j in range(bb)], axis=0)
    s = jnp.where(row_h == col_h, s * (XHEAD_DIM ** -0.5), -1e30)
    p = jnp.exp(s - jnp.max(s, axis=-1, keepdims=True))
    inv = 1.0 / jnp.sum(p, axis=-1, keepdims=True)
    pb = p.astype(BF16)
    for j in range(bb):
        rs = slice(j * rows, (j + 1) * rows)
        o_ref[j] = _dot(pb[rs], v_ref[j].astype(BF16)) * inv[rs]


def _sample_post_kernel(x_ref, mbr_ref, att_ref, gatt_ref, wout_ref, post_ref, o_ref, m_ref, *, layer):
    nb = DEC_BATCH
    for h in range(N_XHEADS):
        cs = slice(h * XHEAD_DIM, (h + 1) * XHEAD_DIM)
        for t in range(DEC_SEQ):
            rs = slice(t * nb, (t + 1) * nb)
            att = att_ref[pl.ds(h * DEC_SEQ + t, nb, stride=Q_ROWS), :]
            m_ref[rs, cs] = (att * gatt_ref[rs, cs]).astype(BF16)
    out = (_dot(mbr_ref[...].astype(BF16), wout_ref[0:BRANCH, :])
           + _dot(m_ref[...], wout_ref[BRANCH:, :]))
    y = _rms(out, post_ref[layer:layer + 1, :])
    for t in range(DEC_SEQ):
        rs = slice(t * nb, (t + 1) * nb)
        x = x_ref[:, t, :] if len(x_ref.shape) == 3 else x_ref[rs, :]
        if len(o_ref.shape) == 3:
            o_ref[:, t, :] = x + y[rs]
        else:
            o_ref[rs, :] = x + y[rs]


def _sample_post(x, mbr, att, gatt, wout, post, layer, out_shape):
    ins = [x, mbr, att, gatt, wout, post]
    n_out = len(out_shape)
    return pl.pallas_call(
        functools.partial(_sample_post_kernel, layer=layer),
        grid=(1,),
        in_specs=[_const_spec(a.shape) for a in ins],
        out_specs=pl.BlockSpec(out_shape, lambda i: (0,) * n_out),
        out_shape=jax.ShapeDtypeStruct(out_shape, F32),
        scratch_shapes=[pltpu.VMEM((mbr.shape[0], XATT), BF16)],
        compiler_params=pltpu.CompilerParams(
            dimension_semantics=("arbitrary",), vmem_limit_bytes=VMEM_LIMIT),
        name="sample_post",
    )(*ins)


def kernel(x_prompt, x_sample, cache_mem_k, cache_mem_v, state_ssm_re, state_ssm_im, mem_prompt,
           w_in_a, ln_v_g, ln_v_b, w_spatial, b_spatial,
           w_in_b, ssm_lambda_re, ssm_lambda_im, ssm_log_dt, ssm_b_re, ssm_b_im, ssm_c_re, ssm_c_im,
           ssm_d, w_glu, b_glu,
           mem_norm_g, w_mem_k, w_mem_v, w_out, pre_norm_g, post_norm_g):
    depth = w_out.shape[0]
    wk = w_mem_k.astype(BF16)
    wv = w_mem_v.astype(BF16)
    pre, post = pre_norm_g, post_norm_g
    lng = ln_v_g[0].reshape(1, BRANCH)
    lnb = ln_v_b[0].reshape(1, BRANCH)
    bglu = b_glu[0].reshape(1, BRANCH)
    dvec = ssm_d[0].reshape(1, BRANCH)
    bias = jnp.repeat(b_spatial[0].T, A_GDIM, axis=1)
    ws4 = w_spatial[0][:, :DEC_SEQ, :DEC_SEQ]
    wrow = jnp.repeat(ws4.transpose(1, 2, 0).reshape(DEC_SEQ * DEC_SEQ, A_GROUPS), A_GDIM, axis=1)
    brow = bias[:DEC_SEQ]

    *s5, win_a = _s5_prep(ssm_lambda_re[0], ssm_lambda_im[0], ssm_log_dt[0],
                          ssm_b_re[0], ssm_b_im[0], ssm_c_re[0], ssm_c_im[0], casts=[(w_in_a, 0)])

    mem = mem_prompt.reshape(BATCH * N_MEM, D_MODEL)
    mk, mv, mkt, mvb, wglu, wout0 = _mem_kv(mem, mem_norm_g, wk, wv,
                                            casts=[(w_glu, 0), (w_out, 0)])

    ck = cache_mem_k.reshape(depth, DEC_BATCH, N_MEM * N_XHEADS, XHEAD_DIM)
    cv = cache_mem_v.reshape(depth, DEC_BATCH, N_MEM * N_XHEADS, XHEAD_DIM)
    q_shape = (DEC_BATCH, Q_ROWS, XHEAD_DIM)
    half = BATCH // 2
    layer_a = functools.partial(_layer_a_prompt, x_prompt, kt=mkt, vb=mvb, win=win_a, lng=lng, lnb=lnb,
                                ws=w_spatial[0], bias=bias, wout=wout0, pre=pre, post=post, layer=0,
                                cache_k=ck, cache_v=cv)
    rows_tb = (DEC_SEQ * DEC_BATCH, D_MODEL)
    v_rows, mbr, q, gatt = _sample_pre_a(x_sample, win_a, lng, lnb, wrow, brow, pre, 0)
    yp_a, att, win_b = layer_a(b0=0, nb=half, sq=q.reshape(q_shape), sample_layer=0, casts=[(w_in_b, 0)])
    xs = _sample_post(x_sample, mbr, att.reshape(q.shape), gatt, wout0, post, 0, rows_tb)
    sre = state_ssm_re[0].reshape(DEC_BATCH, SSM_GROUPS * SSM_STATE)
    sim = state_ssm_im[0].reshape(DEC_BATCH, SSM_GROUPS * SSM_STATE)
    mbr, q, gatt, hs_re, hs_im = _sample_pre_b(xs, win_b, sre, sim, s5, dvec, wglu, bglu, pre, 1)
    yp_b, att, wout1 = layer_a(b0=half, nb=half, sq=q.reshape(q_shape), sample_layer=1, casts=[(w_out, 1)])
    ys = _sample_post(xs, mbr, att.reshape(q.shape), gatt, wout1, post, 1, x_sample.shape)
    yp, hp_re, hp_im = _layer_b_prompt(yp_a, yp_b, mkt, mvb, win_b, s5, dvec, wglu, bglu, wout1, pre, post, 1)

    kv_shape = (depth, BATCH, N_MEM, N_XHEADS, XHEAD_DIM)
    st_p = (1, BATCH, SSM_GROUPS, SSM_STATE)
    st_s = (1, DEC_BATCH, SSM_GROUPS, SSM_STATE)
    return (yp, ys, mk.reshape(kv_shape), mv.reshape(kv_shape),
            hp_re.reshape(st_p), hp_im.reshape(st_p), hs_re.reshape(st_s), hs_im.reshape(st_s),
            v_rows[None])
```
